```python
import jax
import jax.numpy as jnp
from jax import lax
import numpy as np


D_MODEL = 2048
BATCH = 4
SEQ = 4096
DEPTH = 4

GRID_W = 64
CTX_LEN = 256
EPS = 1e-6
ROPE_BASE = 10000.0
NEG_INF = -1e30
Q_BLOCK = 128
N_MOD = 6

A_NOPE = 128
A_ROPE = 64
A_V = 128
A_HEADS = (D_MODEL // 2) // A_V
A_Q_RANK = D_MODEL // 4
A_KV_RANK = D_MODEL // 8
A_OUT = A_HEADS * A_V

B_WINDOWS = (2, 4, 8, 16)
B_GROUPS = len(B_WINDOWS)
B_WIDTH = D_MODEL // 2
B_GROUP_W = B_WIDTH // B_GROUPS

OFF_KV = A_Q_RANK
OFF_POOL = A_Q_RANK + A_KV_RANK + A_ROPE
AB_IN = OFF_POOL + B_WIDTH

C_HEAD_DIM = 64
C_HEADS = D_MODEL // C_HEAD_DIM
C_KV_HEADS = C_HEADS // 8
C_GROUP = C_HEADS // C_KV_HEADS
C_WINDOW = 128
C_Q_W = C_HEADS * C_HEAD_DIM
C_KV_W = C_KV_HEADS * C_HEAD_DIM
C_IN = C_Q_W + 2 * C_KV_W
BAND_PARTS = 1 + 2 * C_WINDOW // Q_BLOCK
BAND = BAND_PARTS * Q_BLOCK

N_GROUPS = 4
EXPERTS_PER_GROUP = 8
N_EXPERTS = N_GROUPS * EXPERTS_PER_GROUP
TOP_K = 2
D_EXPERT = D_MODEL // 4
MOE_BLOCK = 128

N_AB_LAYERS = (DEPTH + 1) // 2
N_C_LAYERS = DEPTH // 2

kernel_name = "hybrid_mla_pool_swa_hmoe_diffusion_trunk"


def rmsnorm(x, g):
    xf = x.astype(jnp.float32)
    y = xf * lax.rsqrt(jnp.mean(xf * xf, axis=-1, keepdims=True) + EPS)
    return (y * g.astype(jnp.float32)).astype(x.dtype)


def modulate(x, shift, scale):
    return x * (1 + scale) + shift


def axial_rope_tables(rows, rot_dim, dtype):
    axis_dim = rot_dim // 2
    inv_freq = ROPE_BASE ** (-jnp.arange(axis_dim // 2, dtype=jnp.float32) * 2.0 / axis_dim)
    row = jnp.repeat(jnp.arange(rows, dtype=jnp.float32), GRID_W)
    col = jnp.tile(jnp.arange(GRID_W, dtype=jnp.float32), rows)
    ang_r = row[:, None] * inv_freq
    ang_c = col[:, None] * inv_freq
    return tuple(t[:, None, :].astype(dtype) for t in
                 (jnp.cos(ang_r), jnp.sin(ang_r), jnp.cos(ang_c), jnp.sin(ang_c)))


def _rotate(x, cos, sin):
    a, b = jnp.split(x, 2, axis=-1)
    return jnp.concatenate([a * cos - b * sin, a * sin + b * cos], axis=-1)


def apply_axial_rope(x, tabs):
    cr, sr, cc, sc = tabs
    xr, xc = jnp.split(x, 2, axis=-1)
    return jnp.concatenate([_rotate(xr, cr, sr), _rotate(xc, cc, sc)], axis=-1)


def mla_queries(cq, q_norm_g, w_uq, tabs):
    n_b, n_l, _ = cq.shape
    q = (rmsnorm(cq, q_norm_g) @ w_uq).reshape(n_b, n_l, A_HEADS, A_NOPE + A_ROPE)
    if tabs is None:
        return q
    return jnp.concatenate([q[..., :A_NOPE], apply_axial_rope(q[..., A_NOPE:], tabs)], axis=-1)


def mla_keys_values(kv_in, kv_norm_g, w_ukv, tabs):
    n_b, n_l, _ = kv_in.shape
    kv = (rmsnorm(kv_in[..., :A_KV_RANK], kv_norm_g) @ w_ukv).reshape(n_b, n_l, A_HEADS, A_NOPE + A_V)
    k_rope = kv_in[..., A_KV_RANK:][:, :, None, :]
    if tabs is not None:
        k_rope = apply_axial_rope(k_rope, tabs)
    k = jnp.concatenate([kv[..., :A_NOPE], jnp.broadcast_to(k_rope, (n_b, n_l, A_HEADS, A_ROPE))], axis=-1)
    return k, kv[..., A_NOPE:]


def softmax_attend(q, k, v):
    s = jnp.einsum('bqhd,bkhd->bhqk', q, k, preferred_element_type=jnp.float32) * (q.shape[-1] ** -0.5)
    p = jax.nn.softmax(s, axis=-1).astype(v.dtype)
    return jnp.einsum('bhqk,bkhd->bqhd', p, v)


def mla_attend_blocks(q, k_all, v_all):
    n_b, n_l, n_h, d_qk = q.shape
    nb = n_l // Q_BLOCK
    qb = q.reshape(n_b, nb, Q_BLOCK, n_h, d_qk).transpose(1, 0, 2, 3, 4)
    o = lax.map(lambda qblk: softmax_attend(qblk, k_all, v_all), qb)
    return o.transpose(1, 0, 2, 3, 4).reshape(n_b, n_l, n_h * A_V)


def multiscale_pool(p):
    n_l = p.shape[1]
    pf = p.astype(jnp.float32)
    prefix = jnp.concatenate([jnp.zeros_like(pf[:, :1]), lax.cumsum(pf, axis=1)], axis=1)
    t = jnp.arange(n_l)
    outs = []
    for gi, w in enumerate(B_WINDOWS):
        lo = jnp.clip(t - w // 2, 0, n_l - 1)
        hi = jnp.clip(t + w // 2 - 1, 0, n_l - 1)
        pre_g = prefix[..., gi * B_GROUP_W:(gi + 1) * B_GROUP_W]
        s = jnp.take(pre_g, hi + 1, axis=1) - jnp.take(pre_g, lo, axis=1)
        outs.append(s / (hi - lo + 1).astype(jnp.float32)[None, :, None])
    return (jnp.concatenate(outs, axis=-1) - pf).astype(p.dtype)


def pool_mixer(p, w_pool, pool_scale):
    n_b, n_l, _ = p.shape
    d = multiscale_pool(p).reshape(n_b, n_l, B_GROUPS, B_GROUP_W)
    y = jnp.einsum('blgc,gcd->blgd', d, w_pool).reshape(n_b, n_l, B_WIDTH)
    return y * pool_scale


def mixer_ab(h_lat, h_ctx, tabs, w_in, q_norm_g, kv_norm_g, w_uq, w_ukv, w_pool, pool_scale, w_out, ctx_out):
    p_lat = h_lat @ w_in
    q_lat = mla_queries(p_lat[..., :OFF_KV], q_norm_g, w_uq, tabs)
    k_lat, v_lat = mla_keys_values(p_lat[..., OFF_KV:OFF_POOL], kv_norm_g, w_ukv, tabs)
    p_ctx = h_ctx @ (w_in if ctx_out else w_in[:, OFF_KV:OFF_POOL])
    kv_ctx_in = p_ctx[..., OFF_KV:OFF_POOL] if ctx_out else p_ctx
    k_ctx, v_ctx = mla_keys_values(kv_ctx_in, kv_norm_g, w_ukv, None)
    k_all = jnp.concatenate([k_lat, k_ctx], axis=1)
    v_all = jnp.concatenate([v_lat, v_ctx], axis=1)
    a_lat = mla_attend_blocks(q_lat, k_all, v_all)
    b_lat = pool_mixer(p_lat[..., OFF_POOL:], w_pool, pool_scale)
    o_lat = jnp.concatenate([a_lat, b_lat], axis=-1) @ w_out
    if not ctx_out:
        return o_lat, None
    n_b, n_c, _ = h_ctx.shape
    q_ctx = mla_queries(p_ctx[..., :OFF_KV], q_norm_g, w_uq, None)
    a_ctx = softmax_attend(q_ctx, k_ctx, v_ctx).reshape(n_b, n_c, A_OUT)
    b_ctx = pool_mixer(p_ctx[..., OFF_POOL:], w_pool, pool_scale)
    o_ctx = jnp.concatenate([a_ctx, b_ctx], axis=-1) @ w_out
    return o_lat, o_ctx


def sink_softmax(s, sink):
    sk = jnp.broadcast_to(sink.astype(jnp.float32).reshape(1, C_KV_HEADS, C_GROUP, 1, 1), s.shape[:-1] + (1,))
    return jax.nn.softmax(jnp.concatenate([s, sk], axis=-1), axis=-1)[..., :-1]


def window_attend(q, k, v, k_ctx, v_ctx, sink):
    n_b, n_l, _, dh = q.shape
    nb = n_l // Q_BLOCK
    scale = dh ** -0.5
    pad = ((0, 0), (C_WINDOW, C_WINDOW), (0, 0), (0, 0))
    kp, vp = jnp.pad(k, pad), jnp.pad(v, pad)

    def bands(t):
        parts = [t[:, j * Q_BLOCK:j * Q_BLOCK + n_l].reshape(n_b, nb, Q_BLOCK, C_KV_HEADS, dh)
                 for j in range(BAND_PARTS)]
        return jnp.concatenate(parts, axis=2).transpose(1, 0, 2, 3, 4)

    qb = q.reshape(n_b, nb, Q_BLOCK, C_KV_HEADS, C_GROUP, dh).transpose(1, 0, 2, 3, 4, 5)
    kb, vb = bands(kp), bands(vp)
    rel = jnp.arange(Q_BLOCK)[:, None] - (jnp.arange(BAND)[None, :] - C_WINDOW)
    in_window = jnp.abs(rel) <= C_WINDOW

    def one(args):
        n, qblk, kblk, vblk = args
        kpos = n * Q_BLOCK - C_WINDOW + jnp.arange(BAND)
        valid = in_window & ((kpos >= 0) & (kpos < n_l))[None, :]
        s_win = jnp.einsum('bqhgd,bjhd->bhgqj', qblk, kblk, preferred_element_type=jnp.float32) * scale
        s_win = jnp.where(valid, s_win, NEG_INF)
        s_ctx = jnp.einsum('bqhgd,bjhd->bhgqj', qblk, k_ctx, preferred_element_type=jnp.float32) * scale
        p = sink_softmax(jnp.concatenate([s_win, s_ctx], axis=-1), sink).astype(v.dtype)
        return jnp.einsum('bhgqj,bjhd->bqhgd', p, jnp.concatenate([vblk, v_ctx], axis=1))

    o = lax.map(one, (jnp.arange(nb), qb, kb, vb))
    return o.transpose(1, 0, 2, 3, 4, 5).reshape(n_b, n_l, C_Q_W)


def sink_attend_dense(q, k, v, sink):
    n_b, n_l, _, dh = q.shape
    qg = q.reshape(n_b, n_l, C_KV_HEADS, C_GROUP, dh)
    s = jnp.einsum('bqhgd,bjhd->bhgqj', qg, k, preferred_element_type=jnp.float32) * (dh ** -0.5)
    p = sink_softmax(s, sink).astype(v.dtype)
    return jnp.einsum('bhgqj,bjhd->bqhgd', p, v).reshape(n_b, n_l, C_Q_W)


def mixer_c(h_lat, h_ctx, tabs, w_in, sink, w_out, ctx_out):
    n_b, n_l, _ = h_lat.shape
    n_c = h_ctx.shape[1]
    p_lat = h_lat @ w_in
    q_lat = apply_axial_rope(p_lat[..., :C_Q_W].reshape(n_b, n_l, C_HEADS, C_HEAD_DIM), tabs)
    k_lat = apply_axial_rope(p_lat[..., C_Q_W:C_Q_W + C_KV_W].reshape(n_b, n_l, C_KV_HEADS, C_HEAD_DIM), tabs)
    v_lat = p_lat[..., C_Q_W + C_KV_W:].reshape(n_b, n_l, C_KV_HEADS, C_HEAD_DIM)
    p_ctx = h_ctx @ (w_in if ctx_out else w_in[:, C_Q_W:])
    kv_ctx = p_ctx[..., p_ctx.shape[-1] - 2 * C_KV_W:]
    k_ctx = kv_ctx[..., :C_KV_W].reshape(n_b, n_c, C_KV_HEADS, C_HEAD_DIM)
    v_ctx = kv_ctx[..., C_KV_W:].reshape(n_b, n_c, C_KV_HEADS, C_HEAD_DIM)
    o_lat = window_attend(q_lat, k_lat, v_lat, k_ctx, v_ctx, sink) @ w_out
    if not ctx_out:
        return o_lat, None
    q_ctx = p_ctx[..., :C_Q_W].reshape(n_b, n_c, C_HEADS, C_HEAD_DIM)
    o_ctx = sink_attend_dense(q_ctx, k_ctx, v_ctx, sink) @ w_out
    return o_lat, o_ctx


def hier_moe(h, w_group, b_group, w_expert, b_expert, w_gate, w_up, w_down):
    n_t, d = h.shape
    hf = h.astype(jnp.float32)
    g_prob = jax.nn.softmax(hf @ w_group.astype(jnp.float32) + b_group.astype(jnp.float32), axis=-1)
    g_idx = jnp.argmax(g_prob, axis=-1)
    g_gate = jnp.take_along_axis(g_prob, g_idx[:, None], axis=-1)
    e_logits = (hf @ w_expert.astype(jnp.float32) + b_expert.astype(jnp.float32)).reshape(n_t, N_GROUPS, EXPERTS_PER_GROUP)
    e_in_group = jnp.take_along_axis(e_logits, g_idx[:, None, None], axis=1)[:, 0]
    top_v, top_i = lax.top_k(e_in_group, TOP_K)
    weight = (g_gate * jax.nn.softmax(top_v, axis=-1)).reshape(-1).astype(h.dtype)
    expert = (g_idx[:, None] * EXPERTS_PER_GROUP + top_i).reshape(-1)
    token = jnp.repeat(jnp.arange(n_t), TOP_K)
    n_tk = n_t * TOP_K
    n_blocks = -(-(n_tk + N_EXPERTS * (MOE_BLOCK - 1)) // MOE_BLOCK)
    order = jnp.argsort(expert)
    e_sorted, tok_sorted, w_sorted = expert[order], token[order], weight[order]
    counts = jnp.bincount(expert, length=N_EXPERTS)
    starts = jnp.cumsum(counts) - counts
    padded = (counts + MOE_BLOCK - 1) // MOE_BLOCK * MOE_BLOCK
    pends = jnp.cumsum(padded)
    dest = (pends - padded)[e_sorted] + jnp.arange(n_tk) - starts[e_sorted]
    row_tok = jnp.full((n_blocks * MOE_BLOCK,), n_t, jnp.int32).at[dest].set(tok_sorted)
    row_w = jnp.zeros((n_blocks * MOE_BLOCK,), h.dtype).at[dest].set(w_sorted)
    block_e = jnp.minimum(jnp.searchsorted(pends, jnp.arange(n_blocks) * MOE_BLOCK, side='right'), N_EXPERTS - 1)
    h_pad = jnp.concatenate([h, jnp.zeros((1, d), h.dtype)], axis=0)

    def one(args):
        rows, e = args
        xb = h_pad[rows]
        a = jax.nn.silu(xb @ w_gate[e]) * (xb @ w_up[e])
        return a @ w_down[e]

    y = lax.map(one, (row_tok.reshape(n_blocks, MOE_BLOCK), block_e)).reshape(-1, d) * row_w[:, None]
    return jax.ops.segment_sum(y, row_tok, num_segments=n_t + 1)[:n_t]


def setup_inputs(seed: int = 0) -> dict:
    key = jax.random.key(seed)
    ks = iter(jax.random.split(key, 40))

    def nrm(shape, scale):
        return jax.random.normal(next(ks), shape, jnp.float32) * scale

    def gain(shape):
        return 1.0 + nrm(shape, 0.1)

    d = D_MODEL
    return {
        'x': nrm((BATCH, SEQ, d), 1.0),
        'c': nrm((BATCH, d), 1.0),
        'ctx': nrm((BATCH, CTX_LEN, d), 1.0),
        'c_ctx': nrm((d,), 1.0),
        'mod_w': nrm((DEPTH, d, N_MOD * d), 0.5 * d ** -0.5),
        'mod_b': nrm((DEPTH, N_MOD * d), 0.02),
        'norm1_g': gain((DEPTH, d)),
        'norm2_g': gain((DEPTH, d)),
        'final_g': gain((d,)),
        'a_w_in': nrm((N_AB_LAYERS, d, AB_IN), d ** -0.5),
        'a_q_norm_g': gain((N_AB_LAYERS, A_Q_RANK)),
        'a_kv_norm_g': gain((N_AB_LAYERS, A_KV_RANK)),
        'a_w_uq': nrm((N_AB_LAYERS, A_Q_RANK, A_HEADS * (A_NOPE + A_ROPE)), A_Q_RANK ** -0.5),
        'a_w_ukv': nrm((N_AB_LAYERS, A_KV_RANK, A_HEADS * (A_NOPE + A_V)), A_KV_RANK ** -0.5),
        'a_w_pool': nrm((N_AB_LAYERS, B_GROUPS, B_GROUP_W, B_GROUP_W), B_GROUP_W ** -0.5),
        'a_pool_scale': gain((N_AB_LAYERS, B_WIDTH)),
        'a_w_out': nrm((N_AB_LAYERS, A_OUT + B_WIDTH, d), (A_OUT + B_WIDTH) ** -0.5),
        'c_w_in': nrm((N_C_LAYERS, d, C_IN), d ** -0.5),
        'c_sink': nrm((N_C_LAYERS, C_HEADS), 1.0),
        'c_w_out': nrm((N_C_LAYERS, C_Q_W, d), C_Q_W ** -0.5),
        'r_w_group': nrm((DEPTH, d, N_GROUPS), d ** -0.5),
        'r_b_group': nrm((DEPTH, N_GROUPS), 0.01),
        'r_w_expert': nrm((DEPTH, d, N_EXPERTS), d ** -0.5),
        'r_b_expert': nrm((DEPTH, N_EXPERTS), 0.01),
        'e_w_gate': nrm((DEPTH, N_EXPERTS, d, D_EXPERT), d ** -0.5),
        'e_w_up': nrm((DEPTH, N_EXPERTS, d, D_EXPERT), d ** -0.5),
        'e_w_down': nrm((DEPTH, N_EXPERTS, D_EXPERT, d), D_EXPERT ** -0.5),
    }


def reference(x, c, ctx, c_ctx, mod_w, mod_b, norm1_g, norm2_g, final_g,
              a_w_in, a_q_norm_g, a_kv_norm_g, a_w_uq, a_w_ukv, a_w_pool, a_pool_scale, a_w_out,
              c_w_in, c_sink, c_w_out,
              r_w_group, r_b_group, r_w_expert, r_b_expert, e_w_gate, e_w_up, e_w_down):
    n_b, n_lat, d = x.shape
    n_ctx = ctx.shape[1]
    rows = n_lat // GRID_W
    tabs_a = axial_rope_tables(rows, A_ROPE, x.dtype)
    tabs_c = axial_rope_tables(rows, C_HEAD_DIM, x.dtype)
    s_lat = jax.nn.silu(c)
    s_ctx = jax.nn.silu(c_ctx)
    h_lat, h_ctx = x, ctx
    for i in range(DEPTH):
        ctx_out = i < DEPTH - 1
        n_mod_ctx = N_MOD if ctx_out else 2
        m_lat = (s_lat @ mod_w[i] + mod_b[i]).reshape(n_b, 1, N_MOD, d)
        m_ctx = (s_ctx @ mod_w[i][:, :n_mod_ctx * d] + mod_b[i][:n_mod_ctx * d]).reshape(1, 1, n_mod_ctx, d)
        z_lat = modulate(rmsnorm(h_lat, norm1_g[i]), m_lat[:, :, 0], m_lat[:, :, 1])
        z_ctx = modulate(rmsnorm(h_ctx, norm1_g[i]), m_ctx[:, :, 0], m_ctx[:, :, 1])
        j = i // 2
        if i % 2 == 0:
            o_lat, o_ctx = mixer_ab(z_lat, z_ctx, tabs_a, a_w_in[j], a_q_norm_g[j], a_kv_norm_g[j],
                                    a_w_uq[j], a_w_ukv[j], a_w_pool[j], a_pool_scale[j], a_w_out[j], ctx_out)
        else:
            o_lat, o_ctx = mixer_c(z_lat, z_ctx, tabs_c, c_w_in[j], c_sink[j], c_w_out[j], ctx_out)
        h_lat = h_lat + m_lat[:, :, 2] * o_lat
        z_lat = modulate(rmsnorm(h_lat, norm2_g[i]), m_lat[:, :, 3], m_lat[:, :, 4])
        tokens = z_lat.reshape(-1, d)
        if ctx_out:
            h_ctx = h_ctx + m_ctx[:, :, 2] * o_ctx
            z_ctx = modulate(rmsnorm(h_ctx, norm2_g[i]), m_ctx[:, :, 3], m_ctx[:, :, 4])
            tokens = jnp.concatenate([tokens, z_ctx.reshape(-1, d)], axis=0)
        y = hier_moe(tokens, r_w_group[i], r_b_group[i], r_w_expert[i], r_b_expert[i],
                     e_w_gate[i], e_w_up[i], e_w_down[i])
        h_lat = h_lat + m_lat[:, :, 5] * y[:n_b * n_lat].reshape(n_b, n_lat, d)
        if ctx_out:
            h_ctx = h_ctx + m_ctx[:, :, 5] * y[n_b * n_lat:].reshape(n_b, n_ctx, d)
    return rmsnorm(h_lat, final_g)
```

```python
import functools

import numpy as np
import jax
import jax.numpy as jnp
from jax import lax
from jax.experimental import pallas as pl
from jax.experimental.pallas import tpu as pltpu

F32 = jnp.float32
BF16 = jnp.bfloat16

D_MODEL = 2048
DEPTH = 4
GRID_W = 64
EPS = 1e-6
ROPE_BASE = 10000.0
NEG_INF = -1e30
N_MOD = 6

A_NOPE = 128
A_ROPE = 64
A_V = 128
A_HEADS = 8
A_Q_RANK = 512
A_KV_RANK = 256
A_QK_PAD = 256
B_WINDOWS = (2, 4, 8, 16)
B_GROUP_W = 256
B_WIDTH = 1024
POOL_HALO = 8

C_HEAD_DIM = 64
C_HEADS = 32
C_KV_HEADS = 4
C_GROUP = 8
C_WINDOW = 128
C_Q_W = C_HEADS * C_HEAD_DIM
C_KV_W = C_KV_HEADS * C_HEAD_DIM

N_GROUPS = 4
EXPERTS_PER_GROUP = 8
N_EXPERTS = 32
TOP_K = 2
D_EXPERT = 512
ROUTER_W = 128

ROW_BLOCK = 256
MOE_ROWS = 256
LANES = 128
VMEM_LIMIT = 56 * 1024 * 1024


def _cparams(n_axes):
    return pltpu.CompilerParams(dimension_semantics=("arbitrary",) * n_axes,
                                vmem_limit_bytes=VMEM_LIMIT)


def _dot(a, b):
    return jnp.dot(a, b, preferred_element_type=F32)


def _dot_t(a, b):
    return lax.dot_general(a, b, (((1,), (1,)), ((), ())), preferred_element_type=F32)


def _rms(x):
    return x * lax.rsqrt(jnp.mean(x * x, axis=-1, keepdims=True) + EPS)


def _norm_mod(h, g, shift, scale):
    return (_rms(h) * g) * (1 + scale) + shift


def _mod_kernel(s_ref, w_ref, b_ref, o_ref):
    s = s_ref[...]
    s = s * jax.nn.sigmoid(s)
    o_ref[...] = _dot(s.astype(BF16), w_ref[...].astype(BF16)) + b_ref[...]


def _modulation(cvec, mod_w, mod_b):
    depth, d, n = mod_w.shape
    tn = 1024
    return pl.pallas_call(
        _mod_kernel,
        out_shape=jax.ShapeDtypeStruct((depth, 8, n), F32),
        grid=(depth, n // tn),
        in_specs=[pl.BlockSpec((8, d), lambda l, j: (0, 0)),
                  pl.BlockSpec((None, d, tn), lambda l, j: (l, 0, j)),
                  pl.BlockSpec((None, 1, tn), lambda l, j: (l, 0, j))],
        out_specs=pl.BlockSpec((None, 8, tn), lambda l, j: (l, 0, j)),
        compiler_params=_cparams(2),
        name="modulation",
    )(cvec, mod_w, mod_b.reshape(depth, 1, n))


class _Rows:
    def __init__(self, n_b, n_lat, n_ctx):
        self.n_b, self.n_lat, self.n_ctx = n_b, n_lat, n_ctx
        self.t_lat = n_b * n_lat
        self.t_all = self.t_lat + n_b * n_ctx
        assert n_lat % ROW_BLOCK == 0 and n_ctx == ROW_BLOCK
        self.lat_blocks = self.t_lat // ROW_BLOCK
        self.all_blocks = self.t_all // ROW_BLOCK
        self.blocks_per_seq = n_lat // ROW_BLOCK

    def mod_index(self, i):
        return jnp.minimum(i // self.blocks_per_seq, self.n_b)

    def pos_index(self, i):
        return jnp.where(i < self.lat_blocks, i % self.blocks_per_seq, self.blocks_per_seq)


def _rope_tables(n_lat, n_ctx):
    axis_dim = A_ROPE // 2
    inv_freq = ROPE_BASE ** (-jnp.arange(axis_dim // 2, dtype=F32) * 2.0 / axis_dim)
    rows = n_lat // GRID_W
    row = jnp.repeat(jnp.arange(rows, dtype=F32), GRID_W)
    col = jnp.tile(jnp.arange(GRID_W, dtype=F32), rows)
    ang_r = row[:, None] * inv_freq
    ang_c = col[:, None] * inv_freq
    cr, sr, cc, sc = jnp.cos(ang_r), jnp.sin(ang_r), jnp.cos(ang_c), jnp.sin(ang_c)
    cos = jnp.concatenate([cr, cr, cc, cc], axis=-1)
    sin = jnp.concatenate([-sr, sr, -sc, sc], axis=-1)
    cos = jnp.concatenate([cos, jnp.ones((n_ctx, 64), F32)], axis=0)
    sin = jnp.concatenate([sin, jnp.zeros((n_ctx, 64), F32)], axis=0)
    return cos, sin


_ROPE_SWAP = np.concatenate([np.arange(16, 32), np.arange(0, 16), np.arange(48, 64), np.arange(32, 48)])


def _in_a_kernel(h_ref, m_ref, g_ref, win_ref, gq_ref, wuq_ref, gkv_ref, wk_ref, wv_ref, cos_ref, sin_ref,
                 q_ref, k_ref, v_ref, pp_ref, *, q_scale):
    m = m_ref[...]
    z = _norm_mod(h_ref[...], g_ref[...], m[0:1], m[1:2])
    p = _dot(z.astype(BF16), win_ref[...])
    cos = cos_ref[...]
    sin = sin_ref[...]

    cqn = _rms(p[:, :A_Q_RANK]) * gq_ref[...]
    qraw = _dot(cqn.astype(BF16), wuq_ref[...])
    for hd in range(A_HEADS):
        c0 = hd * A_QK_PAD
        t = qraw[:, c0 + A_NOPE:c0 + A_QK_PAD]
        rot = t * cos + pltpu.roll(t, 64, 1) * sin
        q_ref[:, c0:c0 + A_NOPE] = (qraw[:, c0:c0 + A_NOPE] * q_scale).astype(BF16)
        q_ref[:, c0 + A_NOPE:c0 + A_QK_PAD] = (rot * q_scale).astype(BF16)

    ckvn = (_rms(p[:, A_Q_RANK:A_Q_RANK + A_KV_RANK]) * gkv_ref[...]).astype(BF16)
    kn = _dot(ckvn, wk_ref[...])
    kt = p[:, 768:896]
    krot = (kt * cos + pltpu.roll(kt, 64, 1) * sin).astype(BF16)
    for hd in range(A_HEADS):
        c0 = hd * A_QK_PAD
        k_ref[:, c0:c0 + A_NOPE] = kn[:, hd * A_NOPE:(hd + 1) * A_NOPE].astype(BF16)
        k_ref[:, c0 + A_NOPE:c0 + A_QK_PAD] = krot
    v_ref[...] = _dot(ckvn, wv_ref[...]).astype(BF16)
    pp_ref[...] = p[:, 896:]


def _in_a(rows, h, mod, g1, w_in, gq, w_uq, gkv, w_ukv, cos2, sin2):
    t, d = h.shape
    off_rope = A_Q_RANK + A_KV_RANK
    win = jnp.concatenate([w_in[:, :off_rope + A_ROPE], w_in[:, off_rope + _ROPE_SWAP],
                           w_in[:, off_rope + A_ROPE:]], axis=1).astype(BF16)
    wq = w_uq.reshape(A_Q_RANK, A_HEADS, A_NOPE + A_ROPE)
    wuq = jnp.concatenate([wq, wq[:, :, A_NOPE + _ROPE_SWAP]], axis=-1).reshape(A_Q_RANK, A_HEADS * A_QK_PAD)
    wkv = w_ukv.reshape(A_KV_RANK, A_HEADS, A_NOPE + A_V)
    wk = wkv[:, :, :A_NOPE].reshape(A_KV_RANK, A_HEADS * A_NOPE).astype(BF16)
    wv = wkv[:, :, A_NOPE:].reshape(A_KV_RANK, A_HEADS * A_V).astype(BF16)
    n_in = win.shape[1]
    bm = ROW_BLOCK
    const = lambda i: (0, 0)
    row = lambda i: (i, 0)
    return pl.pallas_call(
        functools.partial(_in_a_kernel, q_scale=float((A_NOPE + A_ROPE) ** -0.5)),
        out_shape=(jax.ShapeDtypeStruct((t, A_HEADS * A_QK_PAD), BF16),
                   jax.ShapeDtypeStruct((t, A_HEADS * A_QK_PAD), BF16),
                   jax.ShapeDtypeStruct((t, A_HEADS * A_V), BF16),
                   jax.ShapeDtypeStruct((t, B_WIDTH), F32)),
        grid=(rows.all_blocks,),
        in_specs=[pl.BlockSpec((bm, d), row),
                  pl.BlockSpec((None, N_MOD, d), lambda i: (rows.mod_index(i), 0, 0)),
                  pl.BlockSpec((1, d), const),
                  pl.BlockSpec((d, n_in), const),
                  pl.BlockSpec((1, A_Q_RANK), const),
                  pl.BlockSpec((A_Q_RANK, A_HEADS * A_QK_PAD), const),
                  pl.BlockSpec((1, A_KV_RANK), const),
                  pl.BlockSpec((A_KV_RANK, A_HEADS * A_NOPE), const),
                  pl.BlockSpec((A_KV_RANK, A_HEADS * A_V), const),
                  pl.BlockSpec((bm, LANES), lambda i: (rows.pos_index(i), 0)),
                  pl.BlockSpec((bm, LANES), lambda i: (rows.pos_index(i), 0))],
        out_specs=(pl.BlockSpec((bm, A_HEADS * A_QK_PAD), row),
                   pl.BlockSpec((bm, A_HEADS * A_QK_PAD), row),
                   pl.BlockSpec((bm, A_HEADS * A_V), row),
                   pl.BlockSpec((bm, B_WIDTH), row)),
        compiler_params=_cparams(1),
        name="in_proj_a",
    )(h, mod, g1.reshape(1, d), win, gq.reshape(1, -1), wuq.astype(BF16), gkv.reshape(1, -1), wk, wv, cos2, sin2)


def _attn_a_kernel(*refs, n_kv):
    q_ref = refs[0]
    k_refs = refs[1:1 + n_kv]
    v_refs = refs[1 + n_kv:1 + 2 * n_kv]
    o_ref = refs[1 + 2 * n_kv]
    q = q_ref[...]
    s = [_dot_t(q, k[...]) for k in k_refs]
    mx = s[0].max(axis=-1, keepdims=True)
    for si in s[1:]:
        mx = jnp.maximum(mx, si.max(axis=-1, keepdims=True))
    e = [jnp.exp(si - mx) for si in s]
    den = e[0].sum(axis=-1, keepdims=True)
    for ei in e[1:]:
        den = den + ei.sum(axis=-1, keepdims=True)
    o = _dot(e[0].astype(BF16), v_refs[0][...])
    for ei, v in zip(e[1:], v_refs[1:]):
        o = o + _dot(ei.astype(BF16), v[...])
    o_ref[...] = (o / den).astype(BF16)


def _attn_a(rows, q, k, v, ctx_out):
    t = q.shape[0]
    bq = ROW_BLOCK
    nq = rows.n_lat // bq
    n_lat, n_ctx = rows.n_lat, rows.n_ctx
    ctx_blk = rows.t_lat // n_ctx
    out_shape = jax.ShapeDtypeStruct((t, A_HEADS * A_V), BF16)
    lat = pl.pallas_call(
        functools.partial(_attn_a_kernel, n_kv=2),
        out_shape=out_shape,
        grid=(rows.n_b, A_HEADS, nq),
        in_specs=[pl.BlockSpec((bq, A_QK_PAD), lambda b, h, i: (b * nq + i, h)),
                  pl.BlockSpec((n_lat, A_QK_PAD), lambda b, h, i: (b, h)),
                  pl.BlockSpec((n_ctx, A_QK_PAD), lambda b, h, i: (ctx_blk + b, h)),
                  pl.BlockSpec((n_lat, A_V), lambda b, h, i: (b, h)),
                  pl.BlockSpec((n_ctx, A_V), lambda b, h, i: (ctx_blk + b, h))],
        out_specs=pl.BlockSpec((bq, A_V), lambda b, h, i: (b * nq + i, h)),
        compiler_params=_cparams(3),
        name="attn_a_lat",
    )(q, k, k, v, v)
    if not ctx_out:
        return lat
    return pl.pallas_call(
        functools.partial(_attn_a_kernel_alias, n_kv=1),
        out_shape=out_shape,
        grid=(rows.n_b, A_HEADS),
        in_specs=[pl.BlockSpec((n_ctx, A_QK_PAD), lambda b, h: (ctx_blk + b, h)),
                  pl.BlockSpec((n_ctx, A_QK_PAD), lambda b, h: (ctx_blk + b, h)),
                  pl.BlockSpec((n_ctx, A_V), lambda b, h: (ctx_blk + b, h)),
                  pl.BlockSpec(memory_space=pl.ANY)],
        out_specs=pl.BlockSpec((n_ctx, A_V), lambda b, h: (ctx_blk + b, h)),
        input_output_aliases={3: 0},
        compiler_params=_cparams(2),
        name="attn_a_ctx",
    )(q, k, v, lat)


def _attn_a_kernel_alias(q_ref, k_ref, v_ref, prev_ref, o_ref, *, n_kv):
    del prev_ref
    _attn_a_kernel(q_ref, k_ref, v_ref, o_ref, n_kv=n_kv)


def _pool_kernel(prev_ref, cur_ref, next_ref, w_ref, s_ref, o_ref, ext_ref, *, rows):
    i = pl.program_id(0)
    bm = ROW_BLOCK
    is_ctx = i >= rows.lat_blocks
    n_l = jnp.where(is_ctx, rows.n_ctx, rows.n_lat)
    pos0 = jnp.where(is_ctx, 0, (i % rows.blocks_per_seq) * bm)
    ext_ref[0:POOL_HALO, :] = prev_ref[...]
    ext_ref[POOL_HALO:POOL_HALO + bm, :] = cur_ref[...]
    ext_ref[POOL_HALO + bm:, :] = next_ref[...]
    pos = pos0 + lax.broadcasted_iota(jnp.int32, (bm, 1), 0)
    for gi, w in enumerate(B_WINDOWS):
        cols = slice(gi * B_GROUP_W, (gi + 1) * B_GROUP_W)
        acc = jnp.zeros((bm, B_GROUP_W), F32)
        for dlt in range(-(w // 2), w // 2):
            x = ext_ref[POOL_HALO + dlt:POOL_HALO + dlt + bm, cols]
            ok = (pos + dlt >= 0) & (pos + dlt < n_l)
            acc = acc + jnp.where(ok, x, 0.0)
        cnt = jnp.minimum(pos + (w // 2 - 1), n_l - 1) - jnp.maximum(pos - w // 2, 0) + 1
        dev = acc / cnt.astype(F32) - cur_ref[:, cols]
        y = _dot(dev.astype(BF16), w_ref[gi]) * s_ref[:, cols]
        o_ref[:, cols] = y.astype(BF16)


def _pool(rows, pp, w_pool, pool_scale):
    t = pp.shape[0]
    bm = ROW_BLOCK
    per = bm // POOL_HALO
    last = t // POOL_HALO - 1
    return pl.pallas_call(
        functools.partial(_pool_kernel, rows=rows),
        out_shape=jax.ShapeDtypeStruct((t, B_WIDTH), BF16),
        grid=(rows.all_blocks,),
        in_specs=[pl.BlockSpec((POOL_HALO, B_WIDTH), lambda i: (jnp.maximum(i * per - 1, 0), 0)),
                  pl.BlockSpec((bm, B_WIDTH), lambda i: (i, 0)),
                  pl.BlockSpec((POOL_HALO, B_WIDTH), lambda i: (jnp.minimum((i + 1) * per, last), 0)),
                  pl.BlockSpec((len(B_WINDOWS), B_GROUP_W, B_GROUP_W), lambda i: (0, 0, 0)),
                  pl.BlockSpec((1, B_WIDTH), lambda i: (0, 0))],
        out_specs=pl.BlockSpec((bm, B_WIDTH), lambda i: (i, 0)),
        scratch_shapes=[pltpu.VMEM((bm + 2 * POOL_HALO, B_WIDTH), F32)],
        compiler_params=_cparams(1),
        name="pool",
    )(pp, pp, pp, w_pool.astype(BF16), pool_scale.reshape(1, -1))


def _out_kernel(a1_ref, a2_ref, w1_ref, w2_ref, h_ref, m_ref, g_ref, wr_ref, br_ref,
                hn_ref, z_ref, lg_ref):
    m = m_ref[...]
    o = _dot(a1_ref[...], w1_ref[...]) + _dot(a2_ref[...], w2_ref[...])
    hn = h_ref[...] + m[2:3] * o
    hn_ref[...] = hn
    z = _norm_mod(hn, g_ref[...], m[3:4], m[4:5])
    z_ref[...] = z.astype(BF16)
    lg_ref[...] = jnp.dot(z, wr_ref[...], preferred_element_type=F32,
                          precision=lax.Precision.HIGHEST) + br_ref[...]


def _out_proj(rows, n_blocks, a1, a2, a2_col, w_out, h, mod, g2, wr, br):
    t, d = h.shape
    t_out = n_blocks * ROW_BLOCK
    bm = ROW_BLOCK
    half = w_out.shape[0] // 2
    w = w_out.astype(BF16)
    const = lambda i: (0, 0)
    row = lambda i: (i, 0)
    return pl.pallas_call(
        _out_kernel,
        out_shape=(jax.ShapeDtypeStruct((t_out, d), F32),
                   jax.ShapeDtypeStruct((t_out, d), BF16),
                   jax.ShapeDtypeStruct((t_out, ROUTER_W), F32)),
        grid=(n_blocks,),
        in_specs=[pl.BlockSpec((bm, half), row),
                  pl.BlockSpec((bm, half), lambda i: (i, a2_col)),
                  pl.BlockSpec((half, d), lambda i: (0, 0)),
                  pl.BlockSpec((half, d), lambda i: (1, 0)),
                  pl.BlockSpec((bm, d), row),
                  pl.BlockSpec((None, N_MOD, d), lambda i: (rows.mod_index(i), 0, 0)),
                  pl.BlockSpec((1, d), const),
                  pl.BlockSpec((d, ROUTER_W), const),
                  pl.BlockSpec((1, ROUTER_W), const)],
        out_specs=(pl.BlockSpec((bm, d), row),
                   pl.BlockSpec((bm, d), row),
                   pl.BlockSpec((bm, ROUTER_W), row)),
        compiler_params=_cparams(1),
        name="out_proj",
    )(a1, a2, w, w, h, mod, g2.reshape(1, d), wr, br)


def _in_c_kernel(h_ref, m_ref, g_ref, win_ref, cos_ref, sin_ref, q_ref, k_ref, v_ref, *, q_scale):
    m = m_ref[...]
    z = _norm_mod(h_ref[...], g_ref[...], m[0:1], m[1:2])
    p = _dot(z.astype(BF16), win_ref[...])
    cos = cos_ref[...]
    sin = sin_ref[...]
    lane = lax.broadcasted_iota(jnp.int32, (1, LANES), 1)
    first = (lane % 32) < 16

    def rope(x):
        partner = jnp.where(first, pltpu.roll(x, LANES - 16, 1), pltpu.roll(x, 16, 1))
        return x * cos + partner * sin

    for tile in range(C_Q_W // LANES):
        cols = slice(tile * LANES, (tile + 1) * LANES)
        q_ref[:, cols] = (rope(p[:, cols]) * q_scale).astype(BF16)
    for tile in range(C_KV_W // LANES):
        cols = slice(C_Q_W + tile * LANES, C_Q_W + (tile + 1) * LANES)
        k_ref[:, tile * LANES:(tile + 1) * LANES] = rope(p[:, cols]).astype(BF16)
    v_ref[...] = p[:, C_Q_W + C_KV_W:].astype(BF16)


def _in_c(rows, h, mod, g1, w_in, cos2, sin2):
    t, d = h.shape
    bm = ROW_BLOCK
    n_in = w_in.shape[1]
    const = lambda i: (0, 0)
    row = lambda i: (i, 0)
    return pl.pallas_call(
        functools.partial(_in_c_kernel, q_scale=float(C_HEAD_DIM ** -0.5)),
        out_shape=(jax.ShapeDtypeStruct((t, C_Q_W), BF16),
                   jax.ShapeDtypeStruct((t, C_KV_W), BF16),
                   jax.ShapeDtypeStruct((t, C_KV_W), BF16)),
        grid=(rows.all_blocks,),
        in_specs=[pl.BlockSpec((bm, d), row),
                  pl.BlockSpec((None, N_MOD, d), lambda i: (rows.mod_index(i), 0, 0)),
                  pl.BlockSpec((1, d), const),
                  pl.BlockSpec((d, n_in), const),
                  pl.BlockSpec((bm, LANES), lambda i: (rows.pos_index(i), 0)),
                  pl.BlockSpec((bm, LANES), lambda i: (rows.pos_index(i), 0))],
        out_specs=(pl.BlockSpec((bm, C_Q_W), row),
                   pl.BlockSpec((bm, C_KV_W), row),
                   pl.BlockSpec((bm, C_KV_W), row)),
        compiler_params=_cparams(1),
        name="in_proj_c",
    )(h, mod, g1.reshape(1, d), w_in.astype(BF16), cos2, sin2)


def _sink_attend(q_ref, sink_ref, keys, vals, masks, o_ref, bq):
    for kv in range(C_KV_HEADS):
        hs = slice(kv * C_HEAD_DIM, (kv + 1) * C_HEAD_DIM)
        q8 = jnp.concatenate(
            [q_ref[:, (kv * C_GROUP + g) * C_HEAD_DIM:(kv * C_GROUP + g + 1) * C_HEAD_DIM] for g in range(C_GROUP)],
            axis=0)
        sink = jnp.concatenate(
            [jnp.full((bq, 1), sink_ref[kv * C_GROUP + g], F32) for g in range(C_GROUP)], axis=0)
        s = []
        for k, msk in zip(keys, masks):
            si = _dot_t(q8, k[:, hs])
            s.append(si if msk is None else jnp.where(msk, si, NEG_INF))
        mx = sink
        for si in s:
            mx = jnp.maximum(mx, si.max(axis=-1, keepdims=True))
        den = jnp.exp(sink - mx)
        o = jnp.zeros((C_GROUP * bq, C_HEAD_DIM), F32)
        for si, v in zip(s, vals):
            e = jnp.exp(si - mx)
            den = den + e.sum(axis=-1, keepdims=True)
            o = o + _dot(e.astype(BF16), v[:, hs])
        o = o / den
        for g in range(C_GROUP):
            c0 = (kv * C_GROUP + g) * C_HEAD_DIM
            o_ref[:, c0:c0 + C_HEAD_DIM] = o[g * bq:(g + 1) * bq].astype(BF16)


def _attn_c_lat_kernel(sink_ref, q_ref, kp_ref, kc_ref, kn_ref, kx_ref, vp_ref, vc_ref, vn_ref, vx_ref, o_ref,
                       *, n_blk):
    n = pl.program_id(1)
    bq = C_WINDOW
    band = 3 * bq
    kband = jnp.concatenate([kp_ref[...], kc_ref[...], kn_ref[...]], axis=0)
    vband = jnp.concatenate([vp_ref[...], vc_ref[...], vn_ref[...]], axis=0)
    qi = lax.broadcasted_iota(jnp.int32, (C_GROUP * bq, band), 0) % bq
    kj = lax.broadcasted_iota(jnp.int32, (C_GROUP * bq, band), 1)
    rel = qi - (kj - bq)
    ok = (jnp.abs(rel) <= C_WINDOW) & ((kj >= bq) | (n > 0)) & ((kj < 2 * bq) | (n < n_blk - 1))
    _sink_attend(q_ref, sink_ref, [kband, kx_ref[...]], [vband, vx_ref[...]], [ok, None], o_ref, bq)


def _attn_c_ctx_kernel(sink_ref, q_ref, kx_ref, vx_ref, prev_ref, o_ref):
    del prev_ref
    _sink_attend(q_ref, sink_ref, [kx_ref[...]], [vx_ref[...]], [None], o_ref, q_ref.shape[0])


def _attn_c(rows, q, k, v, sink, ctx_out):
    t = q.shape[0]
    bq = C_WINDOW
    n_blk = rows.n_lat // bq
    n_ctx = rows.n_ctx
    ctx_blk = rows.t_lat // n_ctx
    out_shape = jax.ShapeDtypeStruct((t, C_Q_W), BF16)
    smem = pl.BlockSpec(memory_space=pltpu.SMEM)
    prev = lambda b, n: (b * n_blk + jnp.maximum(n - 1, 0), 0)
    cur = lambda b, n: (b * n_blk + n, 0)
    nxt = lambda b, n: (b * n_blk + jnp.minimum(n + 1, n_blk - 1), 0)
    cx = lambda b, n: (ctx_blk + b, 0)
    kv_specs = [pl.BlockSpec((bq, C_KV_W), prev), pl.BlockSpec((bq, C_KV_W), cur),
                pl.BlockSpec((bq, C_KV_W), nxt), pl.BlockSpec((n_ctx, C_KV_W), cx)]
    lat = pl.pallas_call(
        functools.partial(_attn_c_lat_kernel, n_blk=n_blk),
        out_shape=out_shape,
        grid=(rows.n_b, n_blk),
        in_specs=[smem, pl.BlockSpec((bq, C_Q_W), cur)] + kv_specs + kv_specs,
        out_specs=pl.BlockSpec((bq, C_Q_W), cur),
        compiler_params=_cparams(2),
        name="attn_c_lat",
    )(sink, q, k, k, k, k, v, v, v, v)
    if not ctx_out:
        return lat
    cxb = lambda b: (ctx_blk + b, 0)
    return pl.pallas_call(
        _attn_c_ctx_kernel,
        out_shape=out_shape,
        grid=(rows.n_b,),
        in_specs=[smem, pl.BlockSpec((n_ctx, C_Q_W), cxb), pl.BlockSpec((n_ctx, C_KV_W), cxb),
                  pl.BlockSpec((n_ctx, C_KV_W), cxb), pl.BlockSpec(memory_space=pl.ANY)],
        out_specs=pl.BlockSpec((n_ctx, C_Q_W), cxb),
        input_output_aliases={4: 0},
        compiler_params=_cparams(1),
        name="attn_c_ctx",
    )(sink, q, k, v, lat)


def _moe_kernel(be_ref, nu_ref, x_ref, wgu_ref, wd_ref, rw_ref, y_ref):
    del be_ref
    i = pl.program_id(0)

    @pl.when(i < nu_ref[0])
    def _():
        gu = _dot(x_ref[...], wgu_ref[...])
        g = gu[:, :D_EXPERT]
        a = (g * jax.nn.sigmoid(g)) * gu[:, D_EXPERT:]
        y_ref[...] = _dot(a.astype(BF16), wd_ref[...]) * rw_ref[...]

    @pl.when(i >= nu_ref[0])
    def _():
        y_ref[...] = jnp.zeros_like(y_ref)


def _moe_blocks(xs, row_w, block_e, n_used, wgu, wd):
    r, d = xs.shape
    bm = MOE_ROWS
    nb = r // bm
    return pl.pallas_call(
        _moe_kernel,
        out_shape=jax.ShapeDtypeStruct((r, d), F32),
        grid_spec=pltpu.PrefetchScalarGridSpec(
            num_scalar_prefetch=2,
            grid=(nb,),
            in_specs=[pl.BlockSpec((bm, d), lambda i, be, nu: (i, 0)),
                      pl.BlockSpec((None, d, 2 * D_EXPERT), lambda i, be, nu: (be[i], 0, 0)),
                      pl.BlockSpec((None, D_EXPERT, d), lambda i, be, nu: (be[i], 0, 0)),
                      pl.BlockSpec((bm, 1), lambda i, be, nu: (i, 0))],
            out_specs=pl.BlockSpec((bm, d), lambda i, be, nu: (i, 0))),
        compiler_params=_cparams(1),
        name="moe_experts",
    )(block_e, n_used, xs, wgu, wd, row_w)


def _route(logits, bm):
    n_t = logits.shape[0]
    lg = logits[:, :N_GROUPS]
    g_prob = jax.nn.softmax(lg, axis=-1)
    g_idx = jnp.argmax(g_prob, axis=-1)
    g_gate = jnp.take_along_axis(g_prob, g_idx[:, None], axis=-1)
    le = logits[:, N_GROUPS:N_GROUPS + N_EXPERTS].reshape(n_t, N_GROUPS, EXPERTS_PER_GROUP)
    e_in = jnp.take_along_axis(le, g_idx[:, None, None], axis=1)[:, 0]
    top_v, top_i = lax.top_k(e_in, TOP_K)
    weight = (g_gate * jax.nn.softmax(top_v, axis=-1)).reshape(-1)
    expert = (g_idx[:, None] * EXPERTS_PER_GROUP + top_i).reshape(-1).astype(jnp.int32)
    n_tk = n_t * TOP_K
    nb = -(-(n_tk + N_EXPERTS * (bm - 1)) // bm)
    onehot = (expert[:, None] == jnp.arange(N_EXPERTS, dtype=jnp.int32)[None, :]).astype(jnp.int32)
    csum = jnp.cumsum(onehot, axis=0)
    rank = jnp.take_along_axis(csum - onehot, expert[:, None], axis=1)[:, 0]
    counts = csum[-1]
    padded = (counts + bm - 1) // bm * bm
    pends = jnp.cumsum(padded)
    dest = (pends - padded)[expert] + rank
    row_tok = jnp.zeros((nb * bm,), jnp.int32).at[dest].set(jnp.arange(n_tk, dtype=jnp.int32) // TOP_K)
    row_w = jnp.zeros((nb * bm,), F32).at[dest].set(weight)
    n_used = (pends[-1] // bm).astype(jnp.int32)
    block_e = jnp.minimum(jnp.searchsorted(pends, jnp.arange(nb, dtype=jnp.int32) * bm, side='right'),
                          N_EXPERTS - 1).astype(jnp.int32)
    return row_tok, row_w, block_e, n_used.reshape(1), dest.reshape(n_t, TOP_K)


def _combine_kernel(h_ref, y0_ref, y1_ref, m_ref, g_ref, o_ref, *, final):
    hn = h_ref[...] + m_ref[...][5:6] * (y0_ref[...] + y1_ref[...])
    if final:
        hn = _rms(hn) * g_ref[...]
    o_ref[...] = hn


def _combine(rows, n_blocks, h, y0, y1, mod, final_g, final):
    d = h.shape[1]
    bm = ROW_BLOCK
    row = lambda i: (i, 0)
    return pl.pallas_call(
        functools.partial(_combine_kernel, final=final),
        out_shape=jax.ShapeDtypeStruct((n_blocks * bm, d), F32),
        grid=(n_blocks,),
        in_specs=[pl.BlockSpec((bm, d), row), pl.BlockSpec((bm, d), row), pl.BlockSpec((bm, d), row),
                  pl.BlockSpec((None, N_MOD, d), lambda i: (rows.mod_index(i), 0, 0)),
                  pl.BlockSpec((1, d), lambda i: (0, 0))],
        out_specs=pl.BlockSpec((bm, d), row),
        compiler_params=_cparams(1),
        name="combine",
    )(h, y0, y1, mod, final_g.reshape(1, d))


def kernel(x, c, ctx, c_ctx, mod_w, mod_b, norm1_g, norm2_g, final_g, a_w_in, a_q_norm_g, a_kv_norm_g, a_w_uq,
           a_w_ukv, a_w_pool, a_pool_scale, a_w_out, c_w_in, c_sink, c_w_out, r_w_group, r_b_group, r_w_expert,
           r_b_expert, e_w_gate, e_w_up, e_w_down):
    n_b, n_lat, d = x.shape
    n_ctx = ctx.shape[1]
    rows = _Rows(n_b, n_lat, n_ctx)
    depth = mod_w.shape[0]

    cvec = jnp.zeros((8, d), F32).at[:n_b].set(c).at[n_b].set(c_ctx)
    mods = _modulation(cvec, mod_w, mod_b).reshape(depth, 8, N_MOD, d)

    cos, sin = _rope_tables(n_lat, n_ctx)
    zeros = jnp.zeros_like(cos)
    cos_a, sin_a = jnp.concatenate([cos, zeros], axis=1), jnp.concatenate([sin, zeros], axis=1)
    cos_c, sin_c = jnp.concatenate([cos, cos], axis=1), jnp.concatenate([sin, sin], axis=1)

    h = jnp.concatenate([x.reshape(-1, d), ctx.reshape(-1, d)], axis=0)
    for i in range(depth):
        ctx_out = i < depth - 1
        j = i // 2
        mod = mods[i]
        if i % 2 == 0:
            q, k, v, pp = _in_a(rows, h, mod, norm1_g[i], a_w_in[j], a_q_norm_g[j], a_w_uq[j], a_kv_norm_g[j],
                                a_w_ukv[j], cos_a, sin_a)
            a1 = _attn_a(rows, q, k, v, ctx_out)
            a2 = _pool(rows, pp, a_w_pool[j], a_pool_scale[j])
            a2_col, w_out = 0, a_w_out[j]
        else:
            q, k, v = _in_c(rows, h, mod, norm1_g[i], c_w_in[j], cos_c, sin_c)
            a1 = a2 = _attn_c(rows, q, k, v, c_sink[j], ctx_out)
            a2_col, w_out = 1, c_w_out[j]
        n_blocks = rows.all_blocks if ctx_out else rows.lat_blocks
        wr = jnp.zeros((d, ROUTER_W), F32).at[:, :N_GROUPS].set(r_w_group[i])
        wr = wr.at[:, N_GROUPS:N_GROUPS + N_EXPERTS].set(r_w_expert[i])
        br = jnp.zeros((1, ROUTER_W), F32).at[0, :N_GROUPS].set(r_b_group[i])
        br = br.at[0, N_GROUPS:N_GROUPS + N_EXPERTS].set(r_b_expert[i])
        hn, z, logits = _out_proj(rows, n_blocks, a1, a2, a2_col, w_out, h, mod, norm2_g[i], wr, br)

        row_tok, row_w, block_e, n_used, dest = _route(logits, MOE_ROWS)
        xs = jnp.take(z, row_tok, axis=0)
        wgu = jnp.concatenate([e_w_gate[i], e_w_up[i]], axis=-1).astype(BF16)
        y = _moe_blocks(xs, row_w.reshape(-1, 1), block_e, n_used, wgu, e_w_down[i].astype(BF16))
        y0 = jnp.take(y, dest[:, 0], axis=0)
        y1 = jnp.take(y, dest[:, 1], axis=0)
        h = _combine(rows, n_blocks, hn, y0, y1, mod, final_g, final=not ctx_out)
    return h.reshape(n_b, n_lat, d)
```

```python
import functools

import numpy as np
import jax
import jax.numpy as jnp
from jax import lax
from jax.experimental import pallas as pl
from jax.experimental.pallas import tpu as pltpu

F32 = jnp.float32
BF16 = jnp.bfloat16

D_MODEL = 2048
DEPTH = 4
GRID_W = 64
EPS = 1e-6
ROPE_BASE = 10000.0
NEG_INF = -1e30
N_MOD = 6

A_NOPE = 128
A_ROPE = 64
A_V = 128
A_HEADS = 8
A_Q_RANK = 512
A_KV_RANK = 256
A_QK_PAD = 256
B_WINDOWS = (2, 4, 8, 16)
B_GROUP_W = 256
B_WIDTH = 1024
POOL_HALO = 8

C_HEAD_DIM = 64
C_HEADS = 32
C_KV_HEADS = 4
C_GROUP = 8
C_WINDOW = 128
C_Q_W = C_HEADS * C_HEAD_DIM
C_KV_W = C_KV_HEADS * C_HEAD_DIM

N_GROUPS = 4
EXPERTS_PER_GROUP = 8
N_EXPERTS = 32
TOP_K = 2
D_EXPERT = 512
ROUTER_W = 128

ROW_BLOCK = 256
MOE_ROWS = 256
LANES = 128
VMEM_LIMIT = 56 * 1024 * 1024
LOG2E = 1.4426950408889634
ATTN_A_ROWS = 256
ATTN_A_KEYS = 512
C_KV_PAD = 128


def _cparams(n_axes):
    return pltpu.CompilerParams(dimension_semantics=("arbitrary",) * n_axes,
                                vmem_limit_bytes=VMEM_LIMIT)


def _dot(a, b):
    return jnp.dot(a, b, preferred_element_type=F32)


def _dot_t(a, b):
    return lax.dot_general(a, b, (((1,), (1,)), ((), ())), preferred_element_type=F32)


def _rms(x):
    return x * lax.rsqrt(jnp.mean(x * x, axis=-1, keepdims=True) + EPS)


def _norm_mod(h, g, shift, scale):
    return (_rms(h) * g) * (1 + scale) + shift


def _mod_kernel(s_ref, w_ref, b_ref, o_ref):
    s = s_ref[...]
    s = s * jax.nn.sigmoid(s)
    o_ref[...] = _dot(s.astype(BF16), w_ref[...].astype(BF16)) + b_ref[...]


def _modulation(cvec, mod_w, mod_b):
    depth, d, n = mod_w.shape
    tn = 1024
    return pl.pallas_call(
        _mod_kernel,
        out_shape=jax.ShapeDtypeStruct((depth, 8, n), F32),
        grid=(depth, n // tn),
        in_specs=[pl.BlockSpec((8, d), lambda l, j: (0, 0)),
                  pl.BlockSpec((None, d, tn), lambda l, j: (l, 0, j)),
                  pl.BlockSpec((None, 1, tn), lambda l, j: (l, 0, j))],
        out_specs=pl.BlockSpec((None, 8, tn), lambda l, j: (l, 0, j)),
        compiler_params=_cparams(2),
        name="modulation",
    )(cvec, mod_w, mod_b.reshape(depth, 1, n))


class _Rows:
    def __init__(self, n_b, n_lat, n_ctx):
        self.n_b, self.n_lat, self.n_ctx = n_b, n_lat, n_ctx
        self.t_lat = n_b * n_lat
        self.t_all = self.t_lat + n_b * n_ctx
        assert n_lat % ROW_BLOCK == 0 and n_ctx == ROW_BLOCK
        self.lat_blocks = self.t_lat // ROW_BLOCK
        self.all_blocks = self.t_all // ROW_BLOCK
        self.blocks_per_seq = n_lat // ROW_BLOCK

    def mod_index(self, i):
        return jnp.minimum(i // self.blocks_per_seq, self.n_b)

    def pos_index(self, i):
        return jnp.where(i < self.lat_blocks, i % self.blocks_per_seq, self.blocks_per_seq)


def _rope_tables(n_lat, n_ctx):
    axis_dim = A_ROPE // 2
    inv_freq = ROPE_BASE ** (-jnp.arange(axis_dim // 2, dtype=F32) * 2.0 / axis_dim)
    rows = n_lat // GRID_W
    row = jnp.repeat(jnp.arange(rows, dtype=F32), GRID_W)
    col = jnp.tile(jnp.arange(GRID_W, dtype=F32), rows)
    ang_r = row[:, None] * inv_freq
    ang_c = col[:, None] * inv_freq
    cr, sr, cc, sc = jnp.cos(ang_r), jnp.sin(ang_r), jnp.cos(ang_c), jnp.sin(ang_c)
    cos = jnp.concatenate([cr, cr, cc, cc], axis=-1)
    sin = jnp.concatenate([-sr, sr, -sc, sc], axis=-1)
    cos = jnp.concatenate([cos, jnp.ones((n_ctx, 64), F32)], axis=0)
    sin = jnp.concatenate([sin, jnp.zeros((n_ctx, 64), F32)], axis=0)
    return cos, sin


_ROPE_SWAP = np.concatenate([np.arange(16, 32), np.arange(0, 16), np.arange(48, 64), np.arange(32, 48)])


def _in_a_kernel(h_ref, m_ref, g_ref, win_ref, gq_ref, wuq_ref, gkv_ref, wk_ref, wv_ref, cos_ref, sin_ref,
                 q_ref, k_ref, v_ref, pp_ref, *, q_scale):
    m = m_ref[...]
    z = _norm_mod(h_ref[...], g_ref[...], m[0:1], m[1:2])
    p = _dot(z.astype(BF16), win_ref[...])
    cos = cos_ref[...]
    sin = sin_ref[...]

    cqn = _rms(p[:, :A_Q_RANK]) * gq_ref[...]
    qraw = _dot(cqn.astype(BF16), wuq_ref[...])
    for hd in range(A_HEADS):
        c0 = hd * A_QK_PAD
        t = qraw[:, c0 + A_NOPE:c0 + A_QK_PAD]
        rot = t * cos + pltpu.roll(t, 64, 1) * sin
        q_ref[:, c0:c0 + A_NOPE] = (qraw[:, c0:c0 + A_NOPE] * q_scale).astype(BF16)
        q_ref[:, c0 + A_NOPE:c0 + A_QK_PAD] = (rot * q_scale).astype(BF16)

    ckvn = (_rms(p[:, A_Q_RANK:A_Q_RANK + A_KV_RANK]) * gkv_ref[...]).astype(BF16)
    kn = _dot(ckvn, wk_ref[...])
    kt = p[:, 768:896]
    krot = (kt * cos + pltpu.roll(kt, 64, 1) * sin).astype(BF16)
    for hd in range(A_HEADS):
        c0 = hd * A_QK_PAD
        k_ref[:, c0:c0 + A_NOPE] = kn[:, hd * A_NOPE:(hd + 1) * A_NOPE].astype(BF16)
        k_ref[:, c0 + A_NOPE:c0 + A_QK_PAD] = krot
    v_ref[...] = _dot(ckvn, wv_ref[...]).astype(BF16)
    pp_ref[...] = p[:, 896:]


def _in_a(rows, h, mod, g1, w_in, gq, w_uq, gkv, w_ukv, cos2, sin2):
    t, d = h.shape
    off_rope = A_Q_RANK + A_KV_RANK
    win = jnp.concatenate([w_in[:, :off_rope + A_ROPE], w_in[:, off_rope + _ROPE_SWAP],
                           w_in[:, off_rope + A_ROPE:]], axis=1).astype(BF16)
    wq = w_uq.reshape(A_Q_RANK, A_HEADS, A_NOPE + A_ROPE)
    wuq = jnp.concatenate([wq, wq[:, :, A_NOPE + _ROPE_SWAP]], axis=-1).reshape(A_Q_RANK, A_HEADS * A_QK_PAD)
    wkv = w_ukv.reshape(A_KV_RANK, A_HEADS, A_NOPE + A_V)
    wk = wkv[:, :, :A_NOPE].reshape(A_KV_RANK, A_HEADS * A_NOPE).astype(BF16)
    wv = wkv[:, :, A_NOPE:].reshape(A_KV_RANK, A_HEADS * A_V).astype(BF16)
    n_in = win.shape[1]
    bm = ROW_BLOCK
    const = lambda i: (0, 0)
    row = lambda i: (i, 0)
    return pl.pallas_call(
        functools.partial(_in_a_kernel, q_scale=float((A_NOPE + A_ROPE) ** -0.5 * LOG2E)),
        out_shape=(jax.ShapeDtypeStruct((t, A_HEADS * A_QK_PAD), BF16),
                   jax.ShapeDtypeStruct((t, A_HEADS * A_QK_PAD), BF16),
                   jax.ShapeDtypeStruct((t, A_HEADS * A_V), BF16),
                   jax.ShapeDtypeStruct((t, B_WIDTH), F32)),
        grid=(rows.all_blocks,),
        in_specs=[pl.BlockSpec((bm, d), row),
                  pl.BlockSpec((None, N_MOD, d), lambda i: (rows.mod_index(i), 0, 0)),
                  pl.BlockSpec((1, d), const),
                  pl.BlockSpec((d, n_in), const),
                  pl.BlockSpec((1, A_Q_RANK), const),
                  pl.BlockSpec((A_Q_RANK, A_HEADS * A_QK_PAD), const),
                  pl.BlockSpec((1, A_KV_RANK), const),
                  pl.BlockSpec((A_KV_RANK, A_HEADS * A_NOPE), const),
                  pl.BlockSpec((A_KV_RANK, A_HEADS * A_V), const),
                  pl.BlockSpec((bm, LANES), lambda i: (rows.pos_index(i), 0)),
                  pl.BlockSpec((bm, LANES), lambda i: (rows.pos_index(i), 0))],
        out_specs=(pl.BlockSpec((bm, A_HEADS * A_QK_PAD), row),
                   pl.BlockSpec((bm, A_HEADS * A_QK_PAD), row),
                   pl.BlockSpec((bm, A_HEADS * A_V), row),
                   pl.BlockSpec((bm, B_WIDTH), row)),
        compiler_params=_cparams(1),
        name="in_proj_a",
    )(h, mod, g1.reshape(1, d), win, gq.reshape(1, -1), wuq.astype(BF16), gkv.reshape(1, -1), wk, wv, cos2, sin2)


def _attn_a_kernel(*refs, n_kv):
    q_ref = refs[0]
    k_refs = refs[1:1 + n_kv]
    v_refs = refs[1 + n_kv:1 + 2 * n_kv]
    o_ref = refs[1 + 2 * n_kv]
    chunks = []
    for k, v in zip(k_refs, v_refs):
        for r0 in range(0, k.shape[0], ATTN_A_KEYS):
            chunks.append((k, v, r0, min(ATTN_A_KEYS, k.shape[0] - r0)))
    bq = min(ATTN_A_ROWS, q_ref.shape[0])

    def block(i, carry):
        rows = pl.ds(pl.multiple_of(i * bq, bq), bq)
        q = q_ref[rows, :]
        s = []
        top = None
        for k, _, r0, n in chunks:
            si = _dot_t(q, k[r0:r0 + n, :])
            s.append(si)
            for j in range(n // LANES):
                tile = si[:, j * LANES:(j + 1) * LANES]
                top = tile if top is None else jnp.maximum(top, tile)
        mx = top.max(axis=-1, keepdims=True)
        o = None
        den = None
        for si, (_, v, r0, n) in zip(s, chunks):
            e = jnp.exp2(si - mx)
            for j in range(n // LANES):
                tile = e[:, j * LANES:(j + 1) * LANES]
                den = tile if den is None else den + tile
            part = _dot(e.astype(BF16), v[r0:r0 + n, :])
            o = part if o is None else o + part
        o_ref[rows, :] = (o / den.sum(axis=-1, keepdims=True)).astype(BF16)
        return carry

    lax.fori_loop(0, q_ref.shape[0] // bq, block, 0)


def _attn_a(rows, q, k, v, ctx_out):
    t = q.shape[0]
    n_lat, n_ctx = rows.n_lat, rows.n_ctx
    ctx_blk = rows.t_lat // n_ctx
    out_shape = jax.ShapeDtypeStruct((t, A_HEADS * A_V), BF16)
    lat = pl.pallas_call(
        functools.partial(_attn_a_kernel, n_kv=2),
        out_shape=out_shape,
        grid=(rows.n_b, A_HEADS),
        in_specs=[pl.BlockSpec((n_lat, A_QK_PAD), lambda b, h: (b, h)),
                  pl.BlockSpec((n_lat, A_QK_PAD), lambda b, h: (b, h)),
                  pl.BlockSpec((n_ctx, A_QK_PAD), lambda b, h: (ctx_blk + b, h)),
                  pl.BlockSpec((n_lat, A_V), lambda b, h: (b, h)),
                  pl.BlockSpec((n_ctx, A_V), lambda b, h: (ctx_blk + b, h))],
        out_specs=pl.BlockSpec((n_lat, A_V), lambda b, h: (b, h)),
        compiler_params=_cparams(2),
        name="attn_a_lat",
    )(q, k, k, v, v)
    if not ctx_out:
        return lat
    return pl.pallas_call(
        functools.partial(_attn_a_kernel_alias, n_kv=1),
        out_shape=out_shape,
        grid=(rows.n_b, A_HEADS),
        in_specs=[pl.BlockSpec((n_ctx, A_QK_PAD), lambda b, h: (ctx_blk + b, h)),
                  pl.BlockSpec((n_ctx, A_QK_PAD), lambda b, h: (ctx_blk + b, h)),
                  pl.BlockSpec((n_ctx, A_V), lambda b, h: (ctx_blk + b, h)),
                  pl.BlockSpec(memory_space=pl.ANY)],
        out_specs=pl.BlockSpec((n_ctx, A_V), lambda b, h: (ctx_blk + b, h)),
        input_output_aliases={3: 0},
        compiler_params=_cparams(2),
        name="attn_a_ctx",
    )(q, k, v, lat)


def _attn_a_kernel_alias(q_ref, k_ref, v_ref, prev_ref, o_ref, *, n_kv):
    del prev_ref
    _attn_a_kernel(q_ref, k_ref, v_ref, o_ref, n_kv=n_kv)


def _pool_kernel(prev_ref, cur_ref, next_ref, w_ref, s_ref, o_ref, ext_ref, *, rows):
    i = pl.program_id(0)
    bm = ROW_BLOCK
    is_ctx = i >= rows.lat_blocks
    n_l = jnp.where(is_ctx, rows.n_ctx, rows.n_lat)
    pos0 = jnp.where(is_ctx, 0, (i % rows.blocks_per_seq) * bm)
    ext_ref[0:POOL_HALO, :] = prev_ref[...]
    ext_ref[POOL_HALO:POOL_HALO + bm, :] = cur_ref[...]
    ext_ref[POOL_HALO + bm:, :] = next_ref[...]
    pos = pos0 + lax.broadcasted_iota(jnp.int32, (bm, 1), 0)
    for gi, w in enumerate(B_WINDOWS):
        cols = slice(gi * B_GROUP_W, (gi + 1) * B_GROUP_W)
        acc = jnp.zeros((bm, B_GROUP_W), F32)
        for dlt in range(-(w // 2), w // 2):
            x = ext_ref[POOL_HALO + dlt:POOL_HALO + dlt + bm, cols]
            ok = (pos + dlt >= 0) & (pos + dlt < n_l)
            acc = acc + jnp.where(ok, x, 0.0)
        cnt = jnp.minimum(pos + (w // 2 - 1), n_l - 1) - jnp.maximum(pos - w // 2, 0) + 1
        dev = acc / cnt.astype(F32) - cur_ref[:, cols]
        y = _dot(dev.astype(BF16), w_ref[gi]) * s_ref[:, cols]
        o_ref[:, cols] = y.astype(BF16)


def _pool(rows, pp, w_pool, pool_scale):
    t = pp.shape[0]
    bm = ROW_BLOCK
    per = bm // POOL_HALO
    last = t // POOL_HALO - 1
    return pl.pallas_call(
        functools.partial(_pool_kernel, rows=rows),
        out_shape=jax.ShapeDtypeStruct((t, B_WIDTH), BF16),
        grid=(rows.all_blocks,),
        in_specs=[pl.BlockSpec((POOL_HALO, B_WIDTH), lambda i: (jnp.maximum(i * per - 1, 0), 0)),
                  pl.BlockSpec((bm, B_WIDTH), lambda i: (i, 0)),
                  pl.BlockSpec((POOL_HALO, B_WIDTH), lambda i: (jnp.minimum((i + 1) * per, last), 0)),
                  pl.BlockSpec((len(B_WINDOWS), B_GROUP_W, B_GROUP_W), lambda i: (0, 0, 0)),
                  pl.BlockSpec((1, B_WIDTH), lambda i: (0, 0))],
        out_specs=pl.BlockSpec((bm, B_WIDTH), lambda i: (i, 0)),
        scratch_shapes=[pltpu.VMEM((bm + 2 * POOL_HALO, B_WIDTH), F32)],
        compiler_params=_cparams(1),
        name="pool",
    )(pp, pp, pp, w_pool.astype(BF16), pool_scale.reshape(1, -1))


def _out_kernel(a1_ref, a2_ref, w1_ref, w2_ref, h_ref, m_ref, g_ref, wr1_ref, wr2_ref, br_ref,
                hn_ref, z_ref, ri_ref, rw_ref, cnt_ref, carry_ref):
    i = pl.program_id(0)
    bm = h_ref.shape[0]

    @pl.when(i == 0)
    def _():
        carry_ref[...] = jnp.zeros_like(carry_ref)

    m = m_ref[...]
    o = _dot(a1_ref[...], w1_ref[...]) + _dot(a2_ref[...], w2_ref[...])
    hn = h_ref[...] + m[2:3] * o
    hn_ref[...] = hn
    z = _norm_mod(hn, g_ref[...], m[3:4], m[4:5])
    z_hi = z.astype(BF16)
    z_ref[...] = z_hi
    z_lo = (z - z_hi.astype(F32)).astype(BF16)
    l2 = _dot(z_hi, wr1_ref[...])
    lg = l2[:, :ROUTER_W] + l2[:, ROUTER_W:] + _dot(z_lo, wr2_ref[...]) + br_ref[...]

    lane = lax.broadcasted_iota(jnp.int32, (bm, ROUTER_W), 1)
    low = jnp.float32(-3e38)
    is_g = lane < N_GROUPS
    glog = jnp.where(is_g, lg, low)
    gmax = glog.max(axis=-1, keepdims=True)
    g_idx = jnp.where(glog == gmax, lane, ROUTER_W).min(axis=-1, keepdims=True)
    g_gate = 1.0 / jnp.where(is_g, jnp.exp(lg - gmax), 0.0).sum(axis=-1, keepdims=True)
    lo = N_GROUPS + EXPERTS_PER_GROUP * g_idx
    el = jnp.where((lane >= lo) & (lane < lo + EXPERTS_PER_GROUP), lg, low)
    v1 = el.max(axis=-1, keepdims=True)
    i1 = jnp.where(el == v1, lane, ROUTER_W).min(axis=-1, keepdims=True)
    el2 = jnp.where(lane == i1, low, el)
    v2 = el2.max(axis=-1, keepdims=True)
    i2 = jnp.where(el2 == v2, lane, ROUTER_W).min(axis=-1, keepdims=True)
    e21 = jnp.exp(v2 - v1)
    w1 = g_gate * (1.0 / (1.0 + e21))
    w2 = g_gate * (e21 / (1.0 + e21))

    hit1 = lane == i1
    hit2 = lane == i2
    onehot = jnp.where(hit1 | hit2, 1.0, 0.0)
    r_i = lax.broadcasted_iota(jnp.int32, (bm, bm), 0)
    c_i = lax.broadcasted_iota(jnp.int32, (bm, bm), 1)
    before = _dot(jnp.where(r_i > c_i, 1.0, 0.0).astype(BF16), onehot.astype(BF16)) + carry_ref[0:1, :]
    rank1 = jnp.where(hit1, before, 0.0).sum(axis=-1, keepdims=True).astype(jnp.int32)
    rank2 = jnp.where(hit2, before, 0.0).sum(axis=-1, keepdims=True).astype(jnp.int32)
    total = carry_ref[0:1, :] + onehot.sum(axis=0, keepdims=True)
    carry_ref[...] = jnp.broadcast_to(total, carry_ref.shape)
    cnt_ref[...] = jnp.broadcast_to(total, cnt_ref.shape)

    ri_ref[...] = jnp.where(lane == 0, i1 - N_GROUPS, jnp.where(lane == 1, i2 - N_GROUPS,
                            jnp.where(lane == 2, rank1, jnp.where(lane == 3, rank2, 0))))
    rw_ref[...] = jnp.where(lane == 0, w1, jnp.where(lane == 1, w2, 0.0))


def _out_proj(rows, n_blocks, a1, a2, a2_col, w_out, h, mod, g2, wr, br):
    t, d = h.shape
    t_out = n_blocks * ROW_BLOCK
    bm = ROW_BLOCK
    half = w_out.shape[0] // 2
    w = w_out.astype(BF16)
    wr_hi = wr.astype(BF16)
    wr_lo = (wr - wr_hi.astype(F32)).astype(BF16)
    const = lambda i: (0, 0)
    row = lambda i: (i, 0)
    return pl.pallas_call(
        _out_kernel,
        out_shape=(jax.ShapeDtypeStruct((t_out, d), F32),
                   jax.ShapeDtypeStruct((t_out, d), BF16),
                   jax.ShapeDtypeStruct((t_out, ROUTER_W), jnp.int32),
                   jax.ShapeDtypeStruct((t_out, ROUTER_W), F32),
                   jax.ShapeDtypeStruct((8, ROUTER_W), F32)),
        grid=(n_blocks,),
        in_specs=[pl.BlockSpec((bm, half), row),
                  pl.BlockSpec((bm, half), lambda i: (i, a2_col)),
                  pl.BlockSpec((half, d), lambda i: (0, 0)),
                  pl.BlockSpec((half, d), lambda i: (1, 0)),
                  pl.BlockSpec((bm, d), row),
                  pl.BlockSpec((None, N_MOD, d), lambda i: (rows.mod_index(i), 0, 0)),
                  pl.BlockSpec((1, d), const),
                  pl.BlockSpec((d, 2 * ROUTER_W), const),
                  pl.BlockSpec((d, ROUTER_W), const),
                  pl.BlockSpec((1, ROUTER_W), const)],
        out_specs=(pl.BlockSpec((bm, d), row),
                   pl.BlockSpec((bm, d), row),
                   pl.BlockSpec((bm, ROUTER_W), row),
                   pl.BlockSpec((bm, ROUTER_W), row),
                   pl.BlockSpec((8, ROUTER_W), const)),
        scratch_shapes=[pltpu.VMEM((8, ROUTER_W), F32)],
        compiler_params=_cparams(1),
        name="out_proj",
    )(a1, a2, w, w, h, mod, g2.reshape(1, d), jnp.concatenate([wr_hi, wr_lo], axis=1), wr_hi, br)


def _in_c_kernel(h_ref, m_ref, g_ref, win_ref, cos_ref, sin_ref, q_ref, k_ref, v_ref, *, q_scale):
    m = m_ref[...]
    z = _norm_mod(h_ref[...], g_ref[...], m[0:1], m[1:2])
    p = _dot(z.astype(BF16), win_ref[...])
    cos = cos_ref[...]
    sin = sin_ref[...]
    lane = lax.broadcasted_iota(jnp.int32, (1, LANES), 1)
    first = (lane % 32) < 16

    def rope(x):
        partner = jnp.where(first, pltpu.roll(x, LANES - 16, 1), pltpu.roll(x, 16, 1))
        return x * cos + partner * sin

    for tile in range(C_Q_W // LANES):
        cols = slice(tile * LANES, (tile + 1) * LANES)
        q_ref[:, cols] = (rope(p[:, cols]) * q_scale).astype(BF16)
    low_half = lane < C_HEAD_DIM
    for tile in range(C_KV_W // LANES):
        kk = rope(p[:, C_Q_W + tile * LANES:C_Q_W + (tile + 1) * LANES])
        vv = p[:, C_Q_W + C_KV_W + tile * LANES:C_Q_W + C_KV_W + (tile + 1) * LANES]
        ones = jnp.where(lane == C_HEAD_DIM, 1.0, 0.0)
        for half, (kh, vh) in enumerate(((kk, vv), (pltpu.roll(kk, C_HEAD_DIM, 1), pltpu.roll(vv, C_HEAD_DIM, 1)))):
            c0 = (2 * tile + half) * C_KV_PAD
            k_ref[:, c0:c0 + C_KV_PAD] = jnp.where(low_half, kh, 0.0).astype(BF16)
            v_ref[:, c0:c0 + C_KV_PAD] = jnp.where(low_half, vh, ones).astype(BF16)


def _in_c(rows, h, mod, g1, w_in, cos2, sin2):
    t, d = h.shape
    bm = ROW_BLOCK
    n_in = w_in.shape[1]
    const = lambda i: (0, 0)
    row = lambda i: (i, 0)
    return pl.pallas_call(
        functools.partial(_in_c_kernel, q_scale=float(C_HEAD_DIM ** -0.5 * LOG2E)),
        out_shape=(jax.ShapeDtypeStruct((t, C_Q_W), BF16),
                   jax.ShapeDtypeStruct((t, C_KV_HEADS * C_KV_PAD), BF16),
                   jax.ShapeDtypeStruct((t, C_KV_HEADS * C_KV_PAD), BF16)),
        grid=(rows.all_blocks,),
        in_specs=[pl.BlockSpec((bm, d), row),
                  pl.BlockSpec((None, N_MOD, d), lambda i: (rows.mod_index(i), 0, 0)),
                  pl.BlockSpec((1, d), const),
                  pl.BlockSpec((d, n_in), const),
                  pl.BlockSpec((bm, LANES), lambda i: (rows.pos_index(i), 0)),
                  pl.BlockSpec((bm, LANES), lambda i: (rows.pos_index(i), 0))],
        out_specs=(pl.BlockSpec((bm, C_Q_W), row),
                   pl.BlockSpec((bm, C_KV_HEADS * C_KV_PAD), row),
                   pl.BlockSpec((bm, C_KV_HEADS * C_KV_PAD), row)),
        compiler_params=_cparams(1),
        name="in_proj_c",
    )(h, mod, g1.reshape(1, d), w_in.astype(BF16), cos2, sin2)


def _sink_attend(q_ref, sink_ref, keys, vals, masks, o_ref, bq):
    def scores(kv):
        q8 = jnp.concatenate([q_ref[:, (kv * C_GROUP + g) * C_HEAD_DIM:(kv * C_GROUP + g + 1) * C_HEAD_DIM]
                              for g in range(C_GROUP)], axis=0)
        return [_dot_t(q8, k[:, kv * C_KV_PAD:kv * C_KV_PAD + C_HEAD_DIM]) for k in keys]

    s_next = scores(0)
    for kv in range(C_KV_HEADS):
        s = s_next
        if kv + 1 < C_KV_HEADS:
            s_next = scores(kv + 1)
        e, mxs = [], []
        for g in range(C_GROUP):
            sg = [si[g * bq:(g + 1) * bq] for si in s]
            sg = [si if msk is None else jnp.where(msk, si, NEG_INF) for si, msk in zip(sg, masks)]
            sink = sink_ref[kv * C_GROUP + g] * LOG2E
            tiles = [si[:, j * LANES:(j + 1) * LANES] for si in sg for j in range(si.shape[1] // LANES)]
            mx = jnp.maximum(sink, functools.reduce(jnp.maximum, tiles).max(axis=-1, keepdims=True))
            e.append([jnp.exp2(si - mx).astype(BF16) for si in sg])
            mxs.append((sink, mx))
        oe = None
        for piece, v in enumerate(vals):
            part = _dot(jnp.concatenate([eg[piece] for eg in e], axis=0), v[:, kv * C_KV_PAD:(kv + 1) * C_KV_PAD])
            oe = part if oe is None else oe + part
        for g, (sink, mx) in enumerate(mxs):
            og = oe[g * bq:(g + 1) * bq]
            den = og[:, C_HEAD_DIM:C_HEAD_DIM + 1] + jnp.exp2(sink - mx)
            c0 = (kv * C_GROUP + g) * C_HEAD_DIM
            o_ref[:, c0:c0 + C_HEAD_DIM] = (og[:, :C_HEAD_DIM] / den).astype(BF16)


def _attn_c_lat_kernel(sink_ref, q_ref, kp_ref, kc_ref, kn_ref, kx_ref, vp_ref, vc_ref, vn_ref, vx_ref, o_ref,
                       *, n_blk):
    n = pl.program_id(1)
    bq = C_WINDOW
    band = 3 * bq
    kband = jnp.concatenate([kp_ref[...], kc_ref[...], kn_ref[...]], axis=0)
    vband = jnp.concatenate([vp_ref[...], vc_ref[...], vn_ref[...]], axis=0)
    qi = lax.broadcasted_iota(jnp.int32, (bq, band), 0)
    kj = lax.broadcasted_iota(jnp.int32, (bq, band), 1)
    rel = qi - (kj - bq)
    ok = (jnp.abs(rel) <= C_WINDOW) & ((kj >= bq) | (n > 0)) & ((kj < 2 * bq) | (n < n_blk - 1))
    _sink_attend(q_ref, sink_ref, [kband, kx_ref[...]], [vband, vx_ref[...]], [ok, None], o_ref, bq)


def _attn_c_ctx_kernel(sink_ref, q_ref, kx_ref, vx_ref, prev_ref, o_ref):
    del prev_ref
    _sink_attend(q_ref, sink_ref, [kx_ref[...]], [vx_ref[...]], [None], o_ref, q_ref.shape[0])


def _attn_c(rows, q, k, v, sink, ctx_out):
    t = q.shape[0]
    bq = C_WINDOW
    n_blk = rows.n_lat // bq
    n_ctx = rows.n_ctx
    ctx_blk = rows.t_lat // n_ctx
    kvw = C_KV_HEADS * C_KV_PAD
    out_shape = jax.ShapeDtypeStruct((t, C_Q_W), BF16)
    smem = pl.BlockSpec(memory_space=pltpu.SMEM)
    prev = lambda b, n: (b * n_blk + jnp.maximum(n - 1, 0), 0)
    cur = lambda b, n: (b * n_blk + n, 0)
    nxt = lambda b, n: (b * n_blk + jnp.minimum(n + 1, n_blk - 1), 0)
    cx = lambda b, n: (ctx_blk + b, 0)
    kv_specs = [pl.BlockSpec((bq, kvw), prev), pl.BlockSpec((bq, kvw), cur),
                pl.BlockSpec((bq, kvw), nxt), pl.BlockSpec((n_ctx, kvw), cx)]
    lat = pl.pallas_call(
        functools.partial(_attn_c_lat_kernel, n_blk=n_blk),
        out_shape=out_shape,
        grid=(rows.n_b, n_blk),
        in_specs=[smem, pl.BlockSpec((bq, C_Q_W), cur)] + kv_specs + kv_specs,
        out_specs=pl.BlockSpec((bq, C_Q_W), cur),
        compiler_params=_cparams(2),
        name="attn_c_lat",
    )(sink, q, k, k, k, k, v, v, v, v)
    if not ctx_out:
        return lat
    cxb = lambda b: (ctx_blk + b, 0)
    return pl.pallas_call(
        _attn_c_ctx_kernel,
        out_shape=out_shape,
        grid=(rows.n_b,),
        in_specs=[smem, pl.BlockSpec((n_ctx, C_Q_W), cxb), pl.BlockSpec((n_ctx, kvw), cxb),
                  pl.BlockSpec((n_ctx, kvw), cxb), pl.BlockSpec(memory_space=pl.ANY)],
        out_specs=pl.BlockSpec((n_ctx, C_Q_W), cxb),
        input_output_aliases={4: 0},
        compiler_params=_cparams(1),
        name="attn_c_ctx",
    )(sink, q, k, v, lat)


def _moe_kernel(be_ref, nu_ref, x_ref, wg_ref, wu_ref, wd_ref, y_ref, wgu_s, wd_s):
    i = pl.program_id(0)
    used = i < nu_ref[0]
    fresh = (i == 0) | (be_ref[i] != be_ref[jnp.maximum(i - 1, 0)])

    @pl.when(used & fresh)
    def _():
        wgu_s[:, :D_EXPERT] = wg_ref[...].astype(BF16)
        wgu_s[:, D_EXPERT:] = wu_ref[...].astype(BF16)
        wd_s[...] = wd_ref[...].astype(BF16)

    @pl.when(used)
    def _():
        gu = _dot(x_ref[...], wgu_s[...])
        g = gu[:, :D_EXPERT]
        a = (g * jax.nn.sigmoid(g)) * gu[:, D_EXPERT:]
        y_ref[...] = _dot(a.astype(BF16), wd_s[...]).astype(BF16)

    @pl.when(jnp.logical_not(used))
    def _():
        y_ref[...] = jnp.zeros_like(y_ref)


def _moe_blocks(layer, xs, block_e, n_used, w_gate, w_up, w_down):
    r, d = xs.shape
    bm = MOE_ROWS
    nb = r // bm
    return pl.pallas_call(
        _moe_kernel,
        out_shape=jax.ShapeDtypeStruct((r, d), BF16),
        grid_spec=pltpu.PrefetchScalarGridSpec(
            num_scalar_prefetch=2,
            grid=(nb,),
            in_specs=[pl.BlockSpec((bm, d), lambda i, be, nu: (i, 0)),
                      pl.BlockSpec((None, None, d, D_EXPERT), lambda i, be, nu: (layer, be[i], 0, 0)),
                      pl.BlockSpec((None, None, d, D_EXPERT), lambda i, be, nu: (layer, be[i], 0, 0)),
                      pl.BlockSpec((None, None, D_EXPERT, d), lambda i, be, nu: (layer, be[i], 0, 0))],
            out_specs=pl.BlockSpec((bm, d), lambda i, be, nu: (i, 0)),
            scratch_shapes=[pltpu.VMEM((d, 2 * D_EXPERT), BF16), pltpu.VMEM((D_EXPERT, d), BF16)]),
        compiler_params=_cparams(1),
        name="moe_experts",
    )(block_e, n_used, xs, w_gate, w_up, w_down)


def _plan(ri, cnt, bm):
    n_t = ri.shape[0]
    n_tk = n_t * TOP_K
    nb = -(-(n_tk + N_EXPERTS * (bm - 1)) // bm)
    counts = cnt[0, N_GROUPS:N_GROUPS + N_EXPERTS].astype(jnp.int32)
    padded = (counts + bm - 1) // bm * bm
    pends = jnp.cumsum(padded)
    dest = (pends - padded)[ri[:, :TOP_K]] + ri[:, TOP_K:2 * TOP_K]
    row_tok = jnp.zeros((nb * bm,), jnp.int32).at[dest.reshape(-1)].set(
        jnp.arange(n_tk, dtype=jnp.int32) // TOP_K)
    n_used = (pends[-1] // bm).astype(jnp.int32)
    blk = jnp.arange(nb, dtype=jnp.int32)
    block_e = jnp.minimum(jnp.searchsorted(pends, blk * bm, side='right'), N_EXPERTS - 1).astype(jnp.int32)
    block_e = jnp.where(blk < n_used, block_e, block_e[n_used - 1])
    return row_tok, block_e, n_used.reshape(1), dest


def _combine_kernel(h_ref, y0_ref, y1_ref, rw_ref, m_ref, g_ref, o_ref, *, final):
    rw = rw_ref[...]
    y = rw[:, 0:1] * y0_ref[...].astype(F32) + rw[:, 1:2] * y1_ref[...].astype(F32)
    hn = h_ref[...] + m_ref[...][5:6] * y
    if final:
        hn = _rms(hn) * g_ref[...]
    o_ref[...] = hn


def _combine(rows, n_blocks, h, y0, y1, rw, mod, final_g, final):
    d = h.shape[1]
    bm = ROW_BLOCK
    row = lambda i: (i, 0)
    return pl.pallas_call(
        functools.partial(_combine_kernel, final=final),
        out_shape=jax.ShapeDtypeStruct((n_blocks * bm, d), F32),
        grid=(n_blocks,),
        in_specs=[pl.BlockSpec((bm, d), row), pl.BlockSpec((bm, d), row), pl.BlockSpec((bm, d), row),
                  pl.BlockSpec((bm, ROUTER_W), row),
                  pl.BlockSpec((None, N_MOD, d), lambda i: (rows.mod_index(i), 0, 0)),
                  pl.BlockSpec((1, d), lambda i: (0, 0))],
        out_specs=pl.BlockSpec((bm, d), row),
        compiler_params=_cparams(1),
        name="combine",
    )(h, y0, y1, rw, mod, final_g.reshape(1, d))


def kernel(x, c, ctx, c_ctx, mod_w, mod_b, norm1_g, norm2_g, final_g, a_w_in, a_q_norm_g, a_kv_norm_g, a_w_uq,
           a_w_ukv, a_w_pool, a_pool_scale, a_w_out, c_w_in, c_sink, c_w_out, r_w_group, r_b_group, r_w_expert,
           r_b_expert, e_w_gate, e_w_up, e_w_down):
    n_b, n_lat, d = x.shape
    n_ctx = ctx.shape[1]
    rows = _Rows(n_b, n_lat, n_ctx)
    depth = mod_w.shape[0]

    cvec = jnp.zeros((8, d), F32).at[:n_b].set(c).at[n_b].set(c_ctx)
    mods = _modulation(cvec, mod_w, mod_b).reshape(depth, 8, N_MOD, d)

    cos, sin = _rope_tables(n_lat, n_ctx)
    zeros = jnp.zeros_like(cos)
    cos_a, sin_a = jnp.concatenate([cos, zeros], axis=1), jnp.concatenate([sin, zeros], axis=1)
    cos_c, sin_c = jnp.concatenate([cos, cos], axis=1), jnp.concatenate([sin, sin], axis=1)

    h = jnp.concatenate([x.reshape(-1, d), ctx.reshape(-1, d)], axis=0)
    for i in range(depth):
        ctx_out = i < depth - 1
        j = i // 2
        mod = mods[i]
        if i % 2 == 0:
            q, k, v, pp = _in_a(rows, h, mod, norm1_g[i], a_w_in[j], a_q_norm_g[j], a_w_uq[j], a_kv_norm_g[j],
                                a_w_ukv[j], cos_a, sin_a)
            a1 = _attn_a(rows, q, k, v, ctx_out)
            a2 = _pool(rows, pp, a_w_pool[j], a_pool_scale[j])
            a2_col, w_out = 0, a_w_out[j]
        else:
            q, k, v = _in_c(rows, h, mod, norm1_g[i], c_w_in[j], cos_c, sin_c)
            a1 = a2 = _attn_c(rows, q, k, v, c_sink[j], ctx_out)
            a2_col, w_out = 1, c_w_out[j]
        n_blocks = rows.all_blocks if ctx_out else rows.lat_blocks
        wr = jnp.zeros((d, ROUTER_W), F32).at[:, :N_GROUPS].set(r_w_group[i])
        wr = wr.at[:, N_GROUPS:N_GROUPS + N_EXPERTS].set(r_w_expert[i])
        br = jnp.zeros((1, ROUTER_W), F32).at[0, :N_GROUPS].set(r_b_group[i])
        br = br.at[0, N_GROUPS:N_GROUPS + N_EXPERTS].set(r_b_expert[i])
        hn, z, ri, rw, cnt = _out_proj(rows, n_blocks, a1, a2, a2_col, w_out, h, mod, norm2_g[i], wr, br)

        row_tok, block_e, n_used, dest = _plan(ri, cnt, MOE_ROWS)
        xs = jnp.take(z, row_tok, axis=0)
        y = _moe_blocks(i, xs, block_e, n_used, e_w_gate, e_w_up, e_w_down)
        y0 = jnp.take(y, dest[:, 0], axis=0)
        y1 = jnp.take(y, dest[:, 1], axis=0)
        h = _combine(rows, n_blocks, hn, y0, y1, rw, mod, final_g, final=not ctx_out)
    return h.reshape(n_b, n_lat, d)
```

```python
import functools

import numpy as np
import jax
import jax.numpy as jnp
from jax import lax
from jax.experimental import pallas as pl
from jax.experimental.pallas import tpu as pltpu

F32 = jnp.float32
BF16 = jnp.bfloat16

D_MODEL = 2048
DEPTH = 4
GRID_W = 64
EPS = 1e-6
ROPE_BASE = 10000.0
NEG_INF = -1e30
N_MOD = 6

A_NOPE = 128
A_ROPE = 64
A_V = 128
A_HEADS = 8
A_Q_RANK = 512
A_KV_RANK = 256
A_QK_PAD = 256
B_WINDOWS = (2, 4, 8, 16)
B_GROUP_W = 256
B_WIDTH = 1024
POOL_HALO = 8

C_HEAD_DIM = 64
C_HEADS = 32
C_KV_HEADS = 4
C_GROUP = 8
C_WINDOW = 128
C_Q_W = C_HEADS * C_HEAD_DIM
C_KV_W = C_KV_HEADS * C_HEAD_DIM

N_GROUPS = 4
EXPERTS_PER_GROUP = 8
N_EXPERTS = 32
TOP_K = 2
D_EXPERT = 512
ROUTER_W = 128

ROW_BLOCK = 256
MOE_ROWS = 512
LANES = 128
VMEM_LIMIT = 56 * 1024 * 1024
LOG2E = 1.4426950408889634
ATTN_A_ROWS = 256
ATTN_A_KEYS = 512
C_KV_PAD = 128


def _cparams(n_axes):
    return pltpu.CompilerParams(dimension_semantics=("arbitrary",) * n_axes,
                                vmem_limit_bytes=VMEM_LIMIT)


def _dot(a, b):
    return jnp.dot(a, b, preferred_element_type=F32)


def _dot_t(a, b):
    return lax.dot_general(a, b, (((1,), (1,)), ((), ())), preferred_element_type=F32)


def _rms(x):
    return x * lax.rsqrt(jnp.mean(x * x, axis=-1, keepdims=True) + EPS)


def _norm_mod(h, g, shift, scale):
    return (_rms(h) * g) * (1 + scale) + shift


def _mod_kernel(s_ref, w_ref, b_ref, o_ref):
    s = s_ref[...]
    s = s * jax.nn.sigmoid(s)
    o_ref[...] = _dot(s.astype(BF16), w_ref[...].astype(BF16)) + b_ref[...]


def _modulation(cvec, mod_w, mod_b):
    depth, d, n = mod_w.shape
    tn = 1024
    return pl.pallas_call(
        _mod_kernel,
        out_shape=jax.ShapeDtypeStruct((depth, 8, n), F32),
        grid=(depth, n // tn),
        in_specs=[pl.BlockSpec((8, d), lambda l, j: (0, 0)),
                  pl.BlockSpec((None, d, tn), lambda l, j: (l, 0, j)),
                  pl.BlockSpec((None, 1, tn), lambda l, j: (l, 0, j))],
        out_specs=pl.BlockSpec((None, 8, tn), lambda l, j: (l, 0, j)),
        compiler_params=_cparams(2),
        name="modulation",
    )(cvec, mod_w, mod_b.reshape(depth, 1, n))


class _Rows:
    def __init__(self, n_b, n_lat, n_ctx):
        self.n_b, self.n_lat, self.n_ctx = n_b, n_lat, n_ctx
        self.t_lat = n_b * n_lat
        self.t_all = self.t_lat + n_b * n_ctx
        assert n_lat % ROW_BLOCK == 0 and n_ctx == ROW_BLOCK
        self.lat_blocks = self.t_lat // ROW_BLOCK
        self.all_blocks = self.t_all // ROW_BLOCK
        self.blocks_per_seq = n_lat // ROW_BLOCK

    def mod_index(self, i):
        return jnp.minimum(i // self.blocks_per_seq, self.n_b)

    def pos_index(self, i):
        return jnp.where(i < self.lat_blocks, i % self.blocks_per_seq, self.blocks_per_seq)


def _rope_tables(n_lat, n_ctx):
    axis_dim = A_ROPE // 2
    inv_freq = ROPE_BASE ** (-jnp.arange(axis_dim // 2, dtype=F32) * 2.0 / axis_dim)
    rows = n_lat // GRID_W
    row = jnp.repeat(jnp.arange(rows, dtype=F32), GRID_W)
    col = jnp.tile(jnp.arange(GRID_W, dtype=F32), rows)
    ang_r = row[:, None] * inv_freq
    ang_c = col[:, None] * inv_freq
    cr, sr, cc, sc = jnp.cos(ang_r), jnp.sin(ang_r), jnp.cos(ang_c), jnp.sin(ang_c)
    cos = jnp.concatenate([cr, cr, cc, cc], axis=-1)
    sin = jnp.concatenate([-sr, sr, -sc, sc], axis=-1)
    cos = jnp.concatenate([cos, jnp.ones((n_ctx, 64), F32)], axis=0)
    sin = jnp.concatenate([sin, jnp.zeros((n_ctx, 64), F32)], axis=0)
    return cos, sin


_ROPE_SWAP = np.concatenate([np.arange(16, 32), np.arange(0, 16), np.arange(48, 64), np.arange(32, 48)])


def _in_a_kernel(h_ref, m_ref, g_ref, win_ref, gq_ref, wuq_ref, gkv_ref, wk_ref, wv_ref, cos_ref, sin_ref,
                 q_ref, k_ref, v_ref, pp_ref, *, q_scale):
    m = m_ref[...]
    z = _norm_mod(h_ref[...], g_ref[...], m[0:1], m[1:2])
    p = _dot(z.astype(BF16), win_ref[...])
    cos = cos_ref[...]
    sin = sin_ref[...]

    cqn = _rms(p[:, :A_Q_RANK]) * gq_ref[...]
    qraw = _dot(cqn.astype(BF16), wuq_ref[...])
    for hd in range(A_HEADS):
        c0 = hd * A_QK_PAD
        t = qraw[:, c0 + A_NOPE:c0 + A_QK_PAD]
        rot = t * cos + pltpu.roll(t, 64, 1) * sin
        q_ref[:, c0:c0 + A_NOPE] = (qraw[:, c0:c0 + A_NOPE] * q_scale).astype(BF16)
        q_ref[:, c0 + A_NOPE:c0 + A_QK_PAD] = (rot * q_scale).astype(BF16)

    ckvn = (_rms(p[:, A_Q_RANK:A_Q_RANK + A_KV_RANK]) * gkv_ref[...]).astype(BF16)
    kn = _dot(ckvn, wk_ref[...])
    kt = p[:, 768:896]
    krot = (kt * cos + pltpu.roll(kt, 64, 1) * sin).astype(BF16)
    for hd in range(A_HEADS):
        c0 = hd * A_QK_PAD
        k_ref[:, c0:c0 + A_NOPE] = kn[:, hd * A_NOPE:(hd + 1) * A_NOPE].astype(BF16)
        k_ref[:, c0 + A_NOPE:c0 + A_QK_PAD] = krot
    v_ref[...] = _dot(ckvn, wv_ref[...]).astype(BF16)
    pp_ref[...] = p[:, 896:]


def _in_a(rows, h, mod, g1, w_in, gq, w_uq, gkv, w_ukv, cos2, sin2):
    t, d = h.shape
    off_rope = A_Q_RANK + A_KV_RANK
    win = jnp.concatenate([w_in[:, :off_rope + A_ROPE], w_in[:, off_rope + _ROPE_SWAP],
                           w_in[:, off_rope + A_ROPE:]], axis=1).astype(BF16)
    wq = w_uq.reshape(A_Q_RANK, A_HEADS, A_NOPE + A_ROPE)
    wuq = jnp.concatenate([wq, wq[:, :, A_NOPE + _ROPE_SWAP]], axis=-1).reshape(A_Q_RANK, A_HEADS * A_QK_PAD)
    wkv = w_ukv.reshape(A_KV_RANK, A_HEADS, A_NOPE + A_V)
    wk = wkv[:, :, :A_NOPE].reshape(A_KV_RANK, A_HEADS * A_NOPE).astype(BF16)
    wv = wkv[:, :, A_NOPE:].reshape(A_KV_RANK, A_HEADS * A_V).astype(BF16)
    n_in = win.shape[1]
    bm = ROW_BLOCK
    const = lambda i: (0, 0)
    row = lambda i: (i, 0)
    return pl.pallas_call(
        functools.partial(_in_a_kernel, q_scale=float((A_NOPE + A_ROPE) ** -0.5 * LOG2E)),
        out_shape=(jax.ShapeDtypeStruct((t, A_HEADS * A_QK_PAD), BF16),
                   jax.ShapeDtypeStruct((t, A_HEADS * A_QK_PAD), BF16),
                   jax.ShapeDtypeStruct((t, A_HEADS * A_V), BF16),
                   jax.ShapeDtypeStruct((t, B_WIDTH), F32)),
        grid=(rows.all_blocks,),
        in_specs=[pl.BlockSpec((bm, d), row),
                  pl.BlockSpec((None, N_MOD, d), lambda i: (rows.mod_index(i), 0, 0)),
                  pl.BlockSpec((1, d), const),
                  pl.BlockSpec((d, n_in), const),
                  pl.BlockSpec((1, A_Q_RANK), const),
                  pl.BlockSpec((A_Q_RANK, A_HEADS * A_QK_PAD), const),
                  pl.BlockSpec((1, A_KV_RANK), const),
                  pl.BlockSpec((A_KV_RANK, A_HEADS * A_NOPE), const),
                  pl.BlockSpec((A_KV_RANK, A_HEADS * A_V), const),
                  pl.BlockSpec((bm, LANES), lambda i: (rows.pos_index(i), 0)),
                  pl.BlockSpec((bm, LANES), lambda i: (rows.pos_index(i), 0))],
        out_specs=(pl.BlockSpec((bm, A_HEADS * A_QK_PAD), row),
                   pl.BlockSpec((bm, A_HEADS * A_QK_PAD), row),
                   pl.BlockSpec((bm, A_HEADS * A_V), row),
                   pl.BlockSpec((bm, B_WIDTH), row)),
        compiler_params=_cparams(1),
        name="in_proj_a",
    )(h, mod, g1.reshape(1, d), win, gq.reshape(1, -1), wuq.astype(BF16), gkv.reshape(1, -1), wk, wv, cos2, sin2)


def _attn_a_kernel(*refs, n_kv):
    q_ref = refs[0]
    k_refs = refs[1:1 + n_kv]
    v_refs = refs[1 + n_kv:1 + 2 * n_kv]
    o_ref = refs[1 + 2 * n_kv]
    chunks = []
    for k, v in zip(k_refs, v_refs):
        for r0 in range(0, k.shape[0], ATTN_A_KEYS):
            chunks.append((k, v, r0, min(ATTN_A_KEYS, k.shape[0] - r0)))
    bq = min(ATTN_A_ROWS, q_ref.shape[0])

    def block(i, carry):
        rows = pl.ds(pl.multiple_of(i * bq, bq), bq)
        q = q_ref[rows, :]
        s = []
        top = None
        for k, _, r0, n in chunks:
            si = _dot_t(q, k[r0:r0 + n, :])
            s.append(si)
            for j in range(n // LANES):
                tile = si[:, j * LANES:(j + 1) * LANES]
                top = tile if top is None else jnp.maximum(top, tile)
        mx = top.max(axis=-1, keepdims=True)
        o = None
        den = None
        for si, (_, v, r0, n) in zip(s, chunks):
            e = jnp.exp2(si - mx)
            for j in range(n // LANES):
                tile = e[:, j * LANES:(j + 1) * LANES]
                den = tile if den is None else den + tile
            part = _dot(e.astype(BF16), v[r0:r0 + n, :])
            o = part if o is None else o + part
        o_ref[rows, :] = (o / den.sum(axis=-1, keepdims=True)).astype(BF16)
        return carry

    lax.fori_loop(0, q_ref.shape[0] // bq, block, 0)


def _attn_a(rows, q, k, v, ctx_out):
    n_lat, n_ctx = rows.n_lat, rows.n_ctx
    ctx_blk = rows.t_lat // n_ctx
    lat = pl.pallas_call(
        functools.partial(_attn_a_kernel, n_kv=2),
        out_shape=jax.ShapeDtypeStruct((rows.t_lat, A_HEADS * A_V), BF16),
        grid=(rows.n_b, A_HEADS),
        in_specs=[pl.BlockSpec((n_lat, A_QK_PAD), lambda b, h: (b, h)),
                  pl.BlockSpec((n_lat, A_QK_PAD), lambda b, h: (b, h)),
                  pl.BlockSpec((n_ctx, A_QK_PAD), lambda b, h: (ctx_blk + b, h)),
                  pl.BlockSpec((n_lat, A_V), lambda b, h: (b, h)),
                  pl.BlockSpec((n_ctx, A_V), lambda b, h: (ctx_blk + b, h))],
        out_specs=pl.BlockSpec((n_lat, A_V), lambda b, h: (b, h)),
        compiler_params=_cparams(2),
        name="attn_a_lat",
    )(q, k, k, v, v)
    if not ctx_out:
        return lat
    ctx = pl.pallas_call(
        functools.partial(_attn_a_kernel, n_kv=1),
        out_shape=jax.ShapeDtypeStruct((rows.n_b * n_ctx, A_HEADS * A_V), BF16),
        grid=(rows.n_b, A_HEADS),
        in_specs=[pl.BlockSpec((n_ctx, A_QK_PAD), lambda b, h: (ctx_blk + b, h)),
                  pl.BlockSpec((n_ctx, A_QK_PAD), lambda b, h: (ctx_blk + b, h)),
                  pl.BlockSpec((n_ctx, A_V), lambda b, h: (ctx_blk + b, h))],
        out_specs=pl.BlockSpec((n_ctx, A_V), lambda b, h: (b, h)),
        compiler_params=_cparams(2),
        name="attn_a_ctx",
    )(q, k, v)
    return jnp.concatenate([lat, ctx], axis=0)


def _pool_kernel(prev_ref, cur_ref, next_ref, w_ref, s_ref, o_ref, ext_ref, *, rows):
    i = pl.program_id(0)
    bm = ROW_BLOCK
    is_ctx = i >= rows.lat_blocks
    n_l = jnp.where(is_ctx, rows.n_ctx, rows.n_lat)
    pos0 = jnp.where(is_ctx, 0, (i % rows.blocks_per_seq) * bm)
    ext_ref[0:POOL_HALO, :] = prev_ref[...]
    ext_ref[POOL_HALO:POOL_HALO + bm, :] = cur_ref[...]
    ext_ref[POOL_HALO + bm:, :] = next_ref[...]
    pos = pos0 + lax.broadcasted_iota(jnp.int32, (bm, 1), 0)
    for gi, w in enumerate(B_WINDOWS):
        cols = slice(gi * B_GROUP_W, (gi + 1) * B_GROUP_W)
        acc = jnp.zeros((bm, B_GROUP_W), F32)
        for dlt in range(-(w // 2), w // 2):
            x = ext_ref[POOL_HALO + dlt:POOL_HALO + dlt + bm, cols]
            ok = (pos + dlt >= 0) & (pos + dlt < n_l)
            acc = acc + jnp.where(ok, x, 0.0)
        cnt = jnp.minimum(pos + (w // 2 - 1), n_l - 1) - jnp.maximum(pos - w // 2, 0) + 1
        dev = acc / cnt.astype(F32) - cur_ref[:, cols]
        y = _dot(dev.astype(BF16), w_ref[gi]) * s_ref[:, cols]
        o_ref[:, cols] = y.astype(BF16)


def _pool(rows, pp, w_pool, pool_scale):
    t = pp.shape[0]
    bm = ROW_BLOCK
    per = bm // POOL_HALO
    last = t // POOL_HALO - 1
    return pl.pallas_call(
        functools.partial(_pool_kernel, rows=rows),
        out_shape=jax.ShapeDtypeStruct((t, B_WIDTH), BF16),
        grid=(rows.all_blocks,),
        in_specs=[pl.BlockSpec((POOL_HALO, B_WIDTH), lambda i: (jnp.maximum(i * per - 1, 0), 0)),
                  pl.BlockSpec((bm, B_WIDTH), lambda i: (i, 0)),
                  pl.BlockSpec((POOL_HALO, B_WIDTH), lambda i: (jnp.minimum((i + 1) * per, last), 0)),
                  pl.BlockSpec((len(B_WINDOWS), B_GROUP_W, B_GROUP_W), lambda i: (0, 0, 0)),
                  pl.BlockSpec((1, B_WIDTH), lambda i: (0, 0))],
        out_specs=pl.BlockSpec((bm, B_WIDTH), lambda i: (i, 0)),
        scratch_shapes=[pltpu.VMEM((bm + 2 * POOL_HALO, B_WIDTH), F32)],
        compiler_params=_cparams(1),
        name="pool",
    )(pp, pp, pp, w_pool.astype(BF16), pool_scale.reshape(1, -1))


def _out_kernel(a1_ref, a2_ref, w1_ref, w2_ref, h_ref, m_ref, g_ref, wr1_ref, wr2_ref, br_ref,
                hn_ref, z_ref, ri_ref, rw_ref, cnt_ref, carry_ref):
    i = pl.program_id(0)
    bm = h_ref.shape[0]

    @pl.when(i == 0)
    def _():
        carry_ref[...] = jnp.zeros_like(carry_ref)

    m = m_ref[...]
    o = _dot(a1_ref[...], w1_ref[...]) + _dot(a2_ref[...], w2_ref[...])
    hn = h_ref[...] + m[2:3] * o
    hn_ref[...] = hn
    z = _norm_mod(hn, g_ref[...], m[3:4], m[4:5])
    z_hi = z.astype(BF16)
    z_ref[...] = z_hi
    z_lo = (z - z_hi.astype(F32)).astype(BF16)
    l2 = _dot(z_hi, wr1_ref[...])
    lg = l2[:, :ROUTER_W] + l2[:, ROUTER_W:] + _dot(z_lo, wr2_ref[...]) + br_ref[...]

    lane = lax.broadcasted_iota(jnp.int32, (bm, ROUTER_W), 1)
    low = jnp.float32(-3e38)
    is_g = lane < N_GROUPS
    glog = jnp.where(is_g, lg, low)
    gmax = glog.max(axis=-1, keepdims=True)
    g_idx = jnp.where(glog == gmax, lane, ROUTER_W).min(axis=-1, keepdims=True)
    g_gate = 1.0 / jnp.where(is_g, jnp.exp(lg - gmax), 0.0).sum(axis=-1, keepdims=True)
    lo = N_GROUPS + EXPERTS_PER_GROUP * g_idx
    el = jnp.where((lane >= lo) & (lane < lo + EXPERTS_PER_GROUP), lg, low)
    v1 = el.max(axis=-1, keepdims=True)
    i1 = jnp.where(el == v1, lane, ROUTER_W).min(axis=-1, keepdims=True)
    el2 = jnp.where(lane == i1, low, el)
    v2 = el2.max(axis=-1, keepdims=True)
    i2 = jnp.where(el2 == v2, lane, ROUTER_W).min(axis=-1, keepdims=True)
    e21 = jnp.exp(v2 - v1)
    w1 = g_gate * (1.0 / (1.0 + e21))
    w2 = g_gate * (e21 / (1.0 + e21))

    hit1 = lane == i1
    hit2 = lane == i2
    onehot = jnp.where(hit1 | hit2, 1.0, 0.0)
    r_i = lax.broadcasted_iota(jnp.int32, (bm, bm), 0)
    c_i = lax.broadcasted_iota(jnp.int32, (bm, bm), 1)
    before = _dot(jnp.where(r_i > c_i, 1.0, 0.0).astype(BF16), onehot.astype(BF16)) + carry_ref[0:1, :]
    rank1 = jnp.where(hit1, before, 0.0).sum(axis=-1, keepdims=True).astype(jnp.int32)
    rank2 = jnp.where(hit2, before, 0.0).sum(axis=-1, keepdims=True).astype(jnp.int32)
    total = carry_ref[0:1, :] + onehot.sum(axis=0, keepdims=True)
    carry_ref[...] = jnp.broadcast_to(total, carry_ref.shape)
    cnt_ref[...] = jnp.broadcast_to(total, cnt_ref.shape)

    ri_ref[...] = jnp.where(lane == 0, i1 - N_GROUPS, jnp.where(lane == 1, i2 - N_GROUPS,
                            jnp.where(lane == 2, rank1, jnp.where(lane == 3, rank2, 0))))
    rw_ref[...] = jnp.where(lane == 0, w1, jnp.where(lane == 1, w2, 0.0))


def _out_proj(rows, n_blocks, a1, a2, a2_col, w_out, h, mod, g2, wr, br):
    t, d = h.shape
    t_out = n_blocks * ROW_BLOCK
    bm = ROW_BLOCK
    half = w_out.shape[0] // 2
    w = w_out.astype(BF16)
    wr_hi = wr.astype(BF16)
    wr_lo = (wr - wr_hi.astype(F32)).astype(BF16)
    const = lambda i: (0, 0)
    row = lambda i: (i, 0)
    return pl.pallas_call(
        _out_kernel,
        out_shape=(jax.ShapeDtypeStruct((t_out, d), F32),
                   jax.ShapeDtypeStruct((t_out, d), BF16),
                   jax.ShapeDtypeStruct((t_out, ROUTER_W), jnp.int32),
                   jax.ShapeDtypeStruct((t_out, ROUTER_W), F32),
                   jax.ShapeDtypeStruct((8, ROUTER_W), F32)),
        grid=(n_blocks,),
        in_specs=[pl.BlockSpec((bm, half), row),
                  pl.BlockSpec((bm, half), lambda i: (i, a2_col)),
                  pl.BlockSpec((half, d), lambda i: (0, 0)),
                  pl.BlockSpec((half, d), lambda i: (1, 0)),
                  pl.BlockSpec((bm, d), row),
                  pl.BlockSpec((None, N_MOD, d), lambda i: (rows.mod_index(i), 0, 0)),
                  pl.BlockSpec((1, d), const),
                  pl.BlockSpec((d, 2 * ROUTER_W), const),
                  pl.BlockSpec((d, ROUTER_W), const),
                  pl.BlockSpec((1, ROUTER_W), const)],
        out_specs=(pl.BlockSpec((bm, d), row),
                   pl.BlockSpec((bm, d), row),
                   pl.BlockSpec((bm, ROUTER_W), row),
                   pl.BlockSpec((bm, ROUTER_W), row),
                   pl.BlockSpec((8, ROUTER_W), const)),
        scratch_shapes=[pltpu.VMEM((8, ROUTER_W), F32)],
        compiler_params=_cparams(1),
        name="out_proj",
    )(a1, a2, w, w, h, mod, g2.reshape(1, d), jnp.concatenate([wr_hi, wr_lo], axis=1), wr_hi, br)


def _in_c_kernel(h_ref, m_ref, g_ref, win_ref, cos_ref, sin_ref, q_ref, k_ref, v_ref, *, q_scale):
    m = m_ref[...]
    z = _norm_mod(h_ref[...], g_ref[...], m[0:1], m[1:2])
    p = _dot(z.astype(BF16), win_ref[...])
    cos = cos_ref[...]
    sin = sin_ref[...]
    lane = lax.broadcasted_iota(jnp.int32, (1, LANES), 1)
    first = (lane % 32) < 16

    def rope(x):
        partner = jnp.where(first, pltpu.roll(x, LANES - 16, 1), pltpu.roll(x, 16, 1))
        return x * cos + partner * sin

    for tile in range(C_Q_W // LANES):
        cols = slice(tile * LANES, (tile + 1) * LANES)
        q_ref[:, cols] = (rope(p[:, cols]) * q_scale).astype(BF16)
    low_half = lane < C_HEAD_DIM
    for tile in range(C_KV_W // LANES):
        kk = rope(p[:, C_Q_W + tile * LANES:C_Q_W + (tile + 1) * LANES])
        vv = p[:, C_Q_W + C_KV_W + tile * LANES:C_Q_W + C_KV_W + (tile + 1) * LANES]
        ones = jnp.where(lane == C_HEAD_DIM, 1.0, 0.0)
        for half, (kh, vh) in enumerate(((kk, vv), (pltpu.roll(kk, C_HEAD_DIM, 1), pltpu.roll(vv, C_HEAD_DIM, 1)))):
            c0 = (2 * tile + half) * C_KV_PAD
            k_ref[:, c0:c0 + C_KV_PAD] = jnp.where(low_half, kh, 0.0).astype(BF16)
            v_ref[:, c0:c0 + C_KV_PAD] = jnp.where(low_half, vh, ones).astype(BF16)


def _in_c(rows, h, mod, g1, w_in, cos2, sin2):
    t, d = h.shape
    bm = ROW_BLOCK
    n_in = w_in.shape[1]
    const = lambda i: (0, 0)
    row = lambda i: (i, 0)
    return pl.pallas_call(
        functools.partial(_in_c_kernel, q_scale=float(C_HEAD_DIM ** -0.5 * LOG2E)),
        out_shape=(jax.ShapeDtypeStruct((t, C_Q_W), BF16),
                   jax.ShapeDtypeStruct((t, C_KV_HEADS * C_KV_PAD), BF16),
                   jax.ShapeDtypeStruct((t, C_KV_HEADS * C_KV_PAD), BF16)),
        grid=(rows.all_blocks,),
        in_specs=[pl.BlockSpec((bm, d), row),
                  pl.BlockSpec((None, N_MOD, d), lambda i: (rows.mod_index(i), 0, 0)),
                  pl.BlockSpec((1, d), const),
                  pl.BlockSpec((d, n_in), const),
                  pl.BlockSpec((bm, LANES), lambda i: (rows.pos_index(i), 0)),
                  pl.BlockSpec((bm, LANES), lambda i: (rows.pos_index(i), 0))],
        out_specs=(pl.BlockSpec((bm, C_Q_W), row),
                   pl.BlockSpec((bm, C_KV_HEADS * C_KV_PAD), row),
                   pl.BlockSpec((bm, C_KV_HEADS * C_KV_PAD), row)),
        compiler_params=_cparams(1),
        name="in_proj_c",
    )(h, mod, g1.reshape(1, d), w_in.astype(BF16), cos2, sin2)


def _sink_attend(q_ref, sink_ref, keys, vals, masks, o_ref, bq):
    def scores(kv):
        q8 = jnp.concatenate([q_ref[:, (kv * C_GROUP + g) * C_HEAD_DIM:(kv * C_GROUP + g + 1) * C_HEAD_DIM]
                              for g in range(C_GROUP)], axis=0)
        return [_dot_t(q8, k[:, kv * C_KV_PAD:kv * C_KV_PAD + C_HEAD_DIM]) for k in keys]

    s_next = scores(0)
    for kv in range(C_KV_HEADS):
        s = s_next
        if kv + 1 < C_KV_HEADS:
            s_next = scores(kv + 1)
        e, mxs = [], []
        for g in range(C_GROUP):
            sg = [si[g * bq:(g + 1) * bq] for si in s]
            sg = [si if msk is None else jnp.where(msk, si, NEG_INF) for si, msk in zip(sg, masks)]
            sink = sink_ref[kv * C_GROUP + g] * LOG2E
            tiles = [si[:, j * LANES:(j + 1) * LANES] for si in sg for j in range(si.shape[1] // LANES)]
            mx = jnp.maximum(sink, functools.reduce(jnp.maximum, tiles).max(axis=-1, keepdims=True))
            e.append([jnp.exp2(si - mx).astype(BF16) for si in sg])
            mxs.append((sink, mx))
        oe = None
        for piece, v in enumerate(vals):
            part = _dot(jnp.concatenate([eg[piece] for eg in e], axis=0), v[:, kv * C_KV_PAD:(kv + 1) * C_KV_PAD])
            oe = part if oe is None else oe + part
        for g, (sink, mx) in enumerate(mxs):
            og = oe[g * bq:(g + 1) * bq]
            den = og[:, C_HEAD_DIM:C_HEAD_DIM + 1] + jnp.exp2(sink - mx)
            c0 = (kv * C_GROUP + g) * C_HEAD_DIM
            o_ref[:, c0:c0 + C_HEAD_DIM] = (og[:, :C_HEAD_DIM] / den).astype(BF16)


def _attn_c_lat_kernel(sink_ref, q_ref, kp_ref, kc_ref, kn_ref, kx_ref, vp_ref, vc_ref, vn_ref, vx_ref, o_ref,
                       *, n_blk):
    n = pl.program_id(1)
    bq = C_WINDOW
    band = 3 * bq
    kband = jnp.concatenate([kp_ref[...], kc_ref[...], kn_ref[...]], axis=0)
    vband = jnp.concatenate([vp_ref[...], vc_ref[...], vn_ref[...]], axis=0)
    qi = lax.broadcasted_iota(jnp.int32, (bq, band), 0)
    kj = lax.broadcasted_iota(jnp.int32, (bq, band), 1)
    rel = qi - (kj - bq)
    ok = (jnp.abs(rel) <= C_WINDOW) & ((kj >= bq) | (n > 0)) & ((kj < 2 * bq) | (n < n_blk - 1))
    _sink_attend(q_ref, sink_ref, [kband, kx_ref[...]], [vband, vx_ref[...]], [ok, None], o_ref, bq)


def _attn_c_ctx_kernel(sink_ref, q_ref, kx_ref, vx_ref, o_ref):
    _sink_attend(q_ref, sink_ref, [kx_ref[...]], [vx_ref[...]], [None], o_ref, q_ref.shape[0])


def _attn_c(rows, q, k, v, sink, ctx_out):
    t = q.shape[0]
    bq = C_WINDOW
    n_blk = rows.n_lat // bq
    n_ctx = rows.n_ctx
    ctx_blk = rows.t_lat // n_ctx
    kvw = C_KV_HEADS * C_KV_PAD
    del t
    smem = pl.BlockSpec(memory_space=pltpu.SMEM)
    prev = lambda b, n: (b * n_blk + jnp.maximum(n - 1, 0), 0)
    cur = lambda b, n: (b * n_blk + n, 0)
    nxt = lambda b, n: (b * n_blk + jnp.minimum(n + 1, n_blk - 1), 0)
    cx = lambda b, n: (ctx_blk + b, 0)
    kv_specs = [pl.BlockSpec((bq, kvw), prev), pl.BlockSpec((bq, kvw), cur),
                pl.BlockSpec((bq, kvw), nxt), pl.BlockSpec((n_ctx, kvw), cx)]
    lat = pl.pallas_call(
        functools.partial(_attn_c_lat_kernel, n_blk=n_blk),
        out_shape=jax.ShapeDtypeStruct((rows.t_lat, C_Q_W), BF16),
        grid=(rows.n_b, n_blk),
        in_specs=[smem, pl.BlockSpec((bq, C_Q_W), cur)] + kv_specs + kv_specs,
        out_specs=pl.BlockSpec((bq, C_Q_W), cur),
        compiler_params=_cparams(2),
        name="attn_c_lat",
    )(sink, q, k, k, k, k, v, v, v, v)
    if not ctx_out:
        return lat
    cxb = lambda b: (ctx_blk + b, 0)
    ctx = pl.pallas_call(
        _attn_c_ctx_kernel,
        out_shape=jax.ShapeDtypeStruct((rows.n_b * n_ctx, C_Q_W), BF16),
        grid=(rows.n_b,),
        in_specs=[smem, pl.BlockSpec((n_ctx, C_Q_W), cxb), pl.BlockSpec((n_ctx, kvw), cxb),
                  pl.BlockSpec((n_ctx, kvw), cxb)],
        out_specs=pl.BlockSpec((n_ctx, C_Q_W), lambda b: (b, 0)),
        compiler_params=_cparams(1),
        name="attn_c_ctx",
    )(sink, q, k, v)
    return jnp.concatenate([lat, ctx], axis=0)


def _moe_kernel(be_ref, nu_ref, x_ref, wg_ref, wu_ref, wd_ref, y_ref, wgu_s, wd_s):
    i = pl.program_id(0)
    used = i < nu_ref[0]
    fresh = (i == 0) | (be_ref[i] != be_ref[jnp.maximum(i - 1, 0)])

    @pl.when(used & fresh)
    def _():
        wgu_s[:, :D_EXPERT] = wg_ref[...].astype(BF16)
        wgu_s[:, D_EXPERT:] = wu_ref[...].astype(BF16)
        wd_s[...] = wd_ref[...].astype(BF16)

    @pl.when(used)
    def _():
        gu = _dot(x_ref[...], wgu_s[...])
        g = gu[:, :D_EXPERT]
        a = (g * jax.nn.sigmoid(g)) * gu[:, D_EXPERT:]
        y_ref[...] = _dot(a.astype(BF16), wd_s[...]).astype(BF16)

    @pl.when(jnp.logical_not(used))
    def _():
        y_ref[...] = jnp.zeros_like(y_ref)


def _moe_blocks(layer, xs, block_e, n_used, w_gate, w_up, w_down):
    r, d = xs.shape
    bm = MOE_ROWS
    nb = r // bm
    return pl.pallas_call(
        _moe_kernel,
        out_shape=jax.ShapeDtypeStruct((r, d), BF16),
        grid_spec=pltpu.PrefetchScalarGridSpec(
            num_scalar_prefetch=2,
            grid=(nb,),
            in_specs=[pl.BlockSpec((bm, d), lambda i, be, nu: (i, 0)),
                      pl.BlockSpec((None, None, d, D_EXPERT), lambda i, be, nu: (layer, be[i], 0, 0)),
                      pl.BlockSpec((None, None, d, D_EXPERT), lambda i, be, nu: (layer, be[i], 0, 0)),
                      pl.BlockSpec((None, None, D_EXPERT, d), lambda i, be, nu: (layer, be[i], 0, 0))],
            out_specs=pl.BlockSpec((bm, d), lambda i, be, nu: (i, 0)),
            scratch_shapes=[pltpu.VMEM((d, 2 * D_EXPERT), BF16), pltpu.VMEM((D_EXPERT, d), BF16)]),
        compiler_params=_cparams(1),
        name="moe_experts",
    )(block_e, n_used, xs, w_gate, w_up, w_down)


def _plan(ri, cnt, bm):
    n_t = ri.shape[0]
    n_tk = n_t * TOP_K
    nb = -(-(n_tk + N_EXPERTS * (bm - 1)) // bm)
    counts = cnt[0, N_GROUPS:N_GROUPS + N_EXPERTS].astype(jnp.int32)
    padded = (counts + bm - 1) // bm * bm
    pends = jnp.cumsum(padded)
    dest = (pends - padded)[ri[:, :TOP_K]] + ri[:, TOP_K:2 * TOP_K]
    row_tok = (jnp.arange(nb * bm, dtype=jnp.int32) % n_t).at[dest.reshape(-1)].set(
        jnp.arange(n_tk, dtype=jnp.int32) // TOP_K, unique_indices=True, mode='promise_in_bounds')
    n_used = (pends[-1] // bm).astype(jnp.int32)
    blk = jnp.arange(nb, dtype=jnp.int32)
    block_e = jnp.sum((blk[:, None] * bm >= pends[None, :]).astype(jnp.int32), axis=1)
    block_e = jnp.minimum(block_e, N_EXPERTS - 1)
    block_e = jnp.where(blk < n_used, block_e, block_e[n_used - 1])
    return row_tok, block_e, n_used.reshape(1), dest


def _combine_kernel(h_ref, y0_ref, y1_ref, rw_ref, m_ref, g_ref, o_ref, *, final):
    rw = rw_ref[...]
    y = rw[:, 0:1] * y0_ref[...].astype(F32) + rw[:, 1:2] * y1_ref[...].astype(F32)
    hn = h_ref[...] + m_ref[...][5:6] * y
    if final:
        hn = _rms(hn) * g_ref[...]
    o_ref[...] = hn


def _combine(rows, n_blocks, h, y0, y1, rw, mod, final_g, final):
    d = h.shape[1]
    bm = ROW_BLOCK
    row = lambda i: (i, 0)
    return pl.pallas_call(
        functools.partial(_combine_kernel, final=final),
        out_shape=jax.ShapeDtypeStruct((n_blocks * bm, d), F32),
        grid=(n_blocks,),
        in_specs=[pl.BlockSpec((bm, d), row), pl.BlockSpec((bm, d), row), pl.BlockSpec((bm, d), row),
                  pl.BlockSpec((bm, ROUTER_W), row),
                  pl.BlockSpec((None, N_MOD, d), lambda i: (rows.mod_index(i), 0, 0)),
                  pl.BlockSpec((1, d), lambda i: (0, 0))],
        out_specs=pl.BlockSpec((bm, d), row),
        compiler_params=_cparams(1),
        name="combine",
    )(h, y0, y1, rw, mod, final_g.reshape(1, d))


def kernel(x, c, ctx, c_ctx, mod_w, mod_b, norm1_g, norm2_g, final_g, a_w_in, a_q_norm_g, a_kv_norm_g, a_w_uq,
           a_w_ukv, a_w_pool, a_pool_scale, a_w_out, c_w_in, c_sink, c_w_out, r_w_group, r_b_group, r_w_expert,
           r_b_expert, e_w_gate, e_w_up, e_w_down):
    n_b, n_lat, d = x.shape
    n_ctx = ctx.shape[1]
    rows = _Rows(n_b, n_lat, n_ctx)
    depth = mod_w.shape[0]

    cvec = jnp.zeros((8, d), F32).at[:n_b].set(c).at[n_b].set(c_ctx)
    mods = _modulation(cvec, mod_w, mod_b).reshape(depth, 8, N_MOD, d)

    cos, sin = _rope_tables(n_lat, n_ctx)
    zeros = jnp.zeros_like(cos)
    cos_a, sin_a = jnp.concatenate([cos, zeros], axis=1), jnp.concatenate([sin, zeros], axis=1)
    cos_c, sin_c = jnp.concatenate([cos, cos], axis=1), jnp.concatenate([sin, sin], axis=1)

    h = jnp.concatenate([x.reshape(-1, d), ctx.reshape(-1, d)], axis=0)
    for i in range(depth):
        ctx_out = i < depth - 1
        j = i // 2
        mod = mods[i]
        if i % 2 == 0:
            q, k, v, pp = _in_a(rows, h, mod, norm1_g[i], a_w_in[j], a_q_norm_g[j], a_w_uq[j], a_kv_norm_g[j],
                                a_w_ukv[j], cos_a, sin_a)
            a1 = _attn_a(rows, q, k, v, ctx_out)
            a2 = _pool(rows, pp, a_w_pool[j], a_pool_scale[j])
            a2_col, w_out = 0, a_w_out[j]
        else:
            q, k, v = _in_c(rows, h, mod, norm1_g[i], c_w_in[j], cos_c, sin_c)
            a1 = a2 = _attn_c(rows, q, k, v, c_sink[j], ctx_out)
            a2_col, w_out = 1, c_w_out[j]
        n_blocks = rows.all_blocks if ctx_out else rows.lat_blocks
        wr = jnp.zeros((d, ROUTER_W), F32).at[:, :N_GROUPS].set(r_w_group[i])
        wr = wr.at[:, N_GROUPS:N_GROUPS + N_EXPERTS].set(r_w_expert[i])
        br = jnp.zeros((1, ROUTER_W), F32).at[0, :N_GROUPS].set(r_b_group[i])
        br = br.at[0, N_GROUPS:N_GROUPS + N_EXPERTS].set(r_b_expert[i])
        hn, z, ri, rw, cnt = _out_proj(rows, n_blocks, a1, a2, a2_col, w_out, h, mod, norm2_g[i], wr, br)

        row_tok, block_e, n_used, dest = _plan(ri, cnt, MOE_ROWS)
        xs = z.at[row_tok].get(mode='promise_in_bounds')
        y = _moe_blocks(i, xs, block_e, n_used, e_w_gate, e_w_up, e_w_down)
        y0 = y.at[dest[:, 0]].get(mode='promise_in_bounds')
        y1 = y.at[dest[:, 1]].get(mode='promise_in_bounds')
        h = _combine(rows, n_blocks, hn, y0, y1, rw, mod, final_g, final=not ctx_out)
    return h.reshape(n_b, n_lat, d)
```

```python
import functools

import numpy as np
import jax
import jax.numpy as jnp
from jax import lax
from jax.experimental import pallas as pl
from jax.experimental.pallas import tpu as pltpu

F32 = jnp.float32
BF16 = jnp.bfloat16

D_MODEL = 2048
DEPTH = 4
GRID_W = 64
EPS = 1e-6
ROPE_BASE = 10000.0
NEG_INF = -1e30
N_MOD = 6

A_NOPE = 128
A_ROPE = 64
A_V = 128
A_HEADS = 8
A_Q_RANK = 512
A_KV_RANK = 256
A_QK_PAD = 256
B_WINDOWS = (2, 4, 8, 16)
B_GROUP_W = 256
B_WIDTH = 1024
POOL_HALO = 8

C_HEAD_DIM = 64
C_HEADS = 32
C_KV_HEADS = 4
C_GROUP = 8
C_WINDOW = 128
C_Q_W = C_HEADS * C_HEAD_DIM
C_KV_W = C_KV_HEADS * C_HEAD_DIM

N_GROUPS = 4
EXPERTS_PER_GROUP = 8
N_EXPERTS = 32
TOP_K = 2
D_EXPERT = 512
ROUTER_W = 128

ROW_BLOCK = 512
POOL_ROWS = 256
MOE_ROWS = 512
LANES = 128
VMEM_LIMIT = 56 * 1024 * 1024
LOG2E = 1.4426950408889634
ATTN_A_ROWS = 256
ATTN_A_KEYS = 512
C_KV_PAD = 128


def _cparams(n_axes):
    return pltpu.CompilerParams(dimension_semantics=("arbitrary",) * n_axes,
                                vmem_limit_bytes=VMEM_LIMIT)


def _dot(a, b):
    return jnp.dot(a, b, preferred_element_type=F32)


def _dot_t(a, b):
    return lax.dot_general(a, b, (((1,), (1,)), ((), ())), preferred_element_type=F32)


def _rms(x):
    return x * lax.rsqrt(jnp.mean(x * x, axis=-1, keepdims=True) + EPS)


def _norm_mod(h, g, shift, scale):
    return (_rms(h) * g) * (1 + scale) + shift


def _mod_kernel(s_ref, w_ref, b_ref, o_ref):
    s = s_ref[...]
    s = s * jax.nn.sigmoid(s)
    o_ref[...] = _dot(s.astype(BF16), w_ref[...].astype(BF16)) + b_ref[...]


def _modulation(cvec, mod_w, mod_b):
    depth, d, n = mod_w.shape
    tn = 1024
    return pl.pallas_call(
        _mod_kernel,
        out_shape=jax.ShapeDtypeStruct((depth, 8, n), F32),
        grid=(depth, n // tn),
        in_specs=[pl.BlockSpec((8, d), lambda l, j: (0, 0)),
                  pl.BlockSpec((None, d, tn), lambda l, j: (l, 0, j)),
                  pl.BlockSpec((None, 1, tn), lambda l, j: (l, 0, j))],
        out_specs=pl.BlockSpec((None, 8, tn), lambda l, j: (l, 0, j)),
        compiler_params=_cparams(2),
        name="modulation",
    )(cvec, mod_w, mod_b.reshape(depth, 1, n))


class _Rows:
    def __init__(self, n_b, n_lat, n_ctx, bm):
        self.n_b, self.n_lat, self.n_ctx, self.bm = n_b, n_lat, n_ctx, bm
        self.t_lat = n_b * n_lat
        self.t_all = self.t_lat + n_b * n_ctx
        assert n_lat % bm == 0 and (n_b * n_ctx) % bm == 0
        self.lat_blocks = self.t_lat // bm
        self.all_blocks = self.t_all // bm
        self.blocks_per_seq = n_lat // bm

    def mod_index(self, i):
        return jnp.minimum(i // self.blocks_per_seq, self.n_b)

    def pos_index(self, i):
        return jnp.where(i < self.lat_blocks, i % self.blocks_per_seq, self.blocks_per_seq)


def _rope_tables(n_lat, n_ctx):
    axis_dim = A_ROPE // 2
    inv_freq = ROPE_BASE ** (-jnp.arange(axis_dim // 2, dtype=F32) * 2.0 / axis_dim)
    rows = n_lat // GRID_W
    row = jnp.repeat(jnp.arange(rows, dtype=F32), GRID_W)
    col = jnp.tile(jnp.arange(GRID_W, dtype=F32), rows)
    ang_r = row[:, None] * inv_freq
    ang_c = col[:, None] * inv_freq
    cr, sr, cc, sc = jnp.cos(ang_r), jnp.sin(ang_r), jnp.cos(ang_c), jnp.sin(ang_c)
    cos = jnp.concatenate([cr, cr, cc, cc], axis=-1)
    sin = jnp.concatenate([-sr, sr, -sc, sc], axis=-1)
    cos = jnp.concatenate([cos, jnp.ones((n_ctx, 64), F32)], axis=0)
    sin = jnp.concatenate([sin, jnp.zeros((n_ctx, 64), F32)], axis=0)
    return cos, sin


_ROPE_SWAP = np.concatenate([np.arange(16, 32), np.arange(0, 16), np.arange(48, 64), np.arange(32, 48)])


def _in_a_kernel(h_ref, m_ref, g_ref, win_ref, gq_ref, wuq_ref, gkv_ref, wk_ref, wv_ref, cos_ref, sin_ref,
                 q_ref, k_ref, v_ref, pp_ref, *, q_scale):
    m = m_ref[...]
    z = _norm_mod(h_ref[...], g_ref[...], m[0:1], m[1:2])
    p = _dot(z.astype(BF16), win_ref[...])
    cos = cos_ref[...]
    sin = sin_ref[...]

    cqn = _rms(p[:, :A_Q_RANK]) * gq_ref[...]
    qraw = _dot(cqn.astype(BF16), wuq_ref[...])
    for hd in range(A_HEADS):
        c0 = hd * A_QK_PAD
        t = qraw[:, c0 + A_NOPE:c0 + A_QK_PAD]
        rot = t * cos + pltpu.roll(t, 64, 1) * sin
        q_ref[:, c0:c0 + A_NOPE] = (qraw[:, c0:c0 + A_NOPE] * q_scale).astype(BF16)
        q_ref[:, c0 + A_NOPE:c0 + A_QK_PAD] = (rot * q_scale).astype(BF16)

    ckvn = (_rms(p[:, A_Q_RANK:A_Q_RANK + A_KV_RANK]) * gkv_ref[...]).astype(BF16)
    kn = _dot(ckvn, wk_ref[...])
    kt = p[:, 768:896]
    krot = (kt * cos + pltpu.roll(kt, 64, 1) * sin).astype(BF16)
    for hd in range(A_HEADS):
        c0 = hd * A_QK_PAD
        k_ref[:, c0:c0 + A_NOPE] = kn[:, hd * A_NOPE:(hd + 1) * A_NOPE].astype(BF16)
        k_ref[:, c0 + A_NOPE:c0 + A_QK_PAD] = krot
    v_ref[...] = _dot(ckvn, wv_ref[...]).astype(BF16)
    pp_ref[...] = p[:, 896:]


def _in_a(rows, h, mod, g1, w_in, gq, w_uq, gkv, w_ukv, cos2, sin2):
    t, d = h.shape
    off_rope = A_Q_RANK + A_KV_RANK
    win = jnp.concatenate([w_in[:, :off_rope + A_ROPE], w_in[:, off_rope + _ROPE_SWAP],
                           w_in[:, off_rope + A_ROPE:]], axis=1).astype(BF16)
    wq = w_uq.reshape(A_Q_RANK, A_HEADS, A_NOPE + A_ROPE)
    wuq = jnp.concatenate([wq, wq[:, :, A_NOPE + _ROPE_SWAP]], axis=-1).reshape(A_Q_RANK, A_HEADS * A_QK_PAD)
    wkv = w_ukv.reshape(A_KV_RANK, A_HEADS, A_NOPE + A_V)
    wk = wkv[:, :, :A_NOPE].reshape(A_KV_RANK, A_HEADS * A_NOPE).astype(BF16)
    wv = wkv[:, :, A_NOPE:].reshape(A_KV_RANK, A_HEADS * A_V).astype(BF16)
    n_in = win.shape[1]
    bm = rows.bm
    const = lambda i: (0, 0)
    row = lambda i: (i, 0)
    return pl.pallas_call(
        functools.partial(_in_a_kernel, q_scale=float((A_NOPE + A_ROPE) ** -0.5 * LOG2E)),
        out_shape=(jax.ShapeDtypeStruct((t, A_HEADS * A_QK_PAD), BF16),
                   jax.ShapeDtypeStruct((t, A_HEADS * A_QK_PAD), BF16),
                   jax.ShapeDtypeStruct((t, A_HEADS * A_V), BF16),
                   jax.ShapeDtypeStruct((t, B_WIDTH), F32)),
        grid=(rows.all_blocks,),
        in_specs=[pl.BlockSpec((bm, d), row),
                  pl.BlockSpec((None, N_MOD, d), lambda i: (rows.mod_index(i), 0, 0)),
                  pl.BlockSpec((1, d), const),
                  pl.BlockSpec((d, n_in), const),
                  pl.BlockSpec((1, A_Q_RANK), const),
                  pl.BlockSpec((A_Q_RANK, A_HEADS * A_QK_PAD), const),
                  pl.BlockSpec((1, A_KV_RANK), const),
                  pl.BlockSpec((A_KV_RANK, A_HEADS * A_NOPE), const),
                  pl.BlockSpec((A_KV_RANK, A_HEADS * A_V), const),
                  pl.BlockSpec((bm, LANES), lambda i: (rows.pos_index(i), 0)),
                  pl.BlockSpec((bm, LANES), lambda i: (rows.pos_index(i), 0))],
        out_specs=(pl.BlockSpec((bm, A_HEADS * A_QK_PAD), row),
                   pl.BlockSpec((bm, A_HEADS * A_QK_PAD), row),
                   pl.BlockSpec((bm, A_HEADS * A_V), row),
                   pl.BlockSpec((bm, B_WIDTH), row)),
        compiler_params=_cparams(1),
        name="in_proj_a",
    )(h, mod, g1.reshape(1, d), win, gq.reshape(1, -1), wuq.astype(BF16), gkv.reshape(1, -1), wk, wv, cos2, sin2)


def _attn_a_kernel(*refs, n_kv):
    q_ref = refs[0]
    k_refs = refs[1:1 + n_kv]
    v_refs = refs[1 + n_kv:1 + 2 * n_kv]
    o_ref = refs[1 + 2 * n_kv]
    chunks = []
    for k, v in zip(k_refs, v_refs):
        for r0 in range(0, k.shape[0], ATTN_A_KEYS):
            chunks.append((k, v, r0, min(ATTN_A_KEYS, k.shape[0] - r0)))
    bq = min(ATTN_A_ROWS, q_ref.shape[0])

    def scores(rows):
        q = q_ref[rows, :]
        s = []
        top = None
        for k, _, r0, n in chunks:
            si = _dot_t(q, k[r0:r0 + n, :])
            s.append(si)
            for j in range(n // LANES):
                tile = si[:, j * LANES:(j + 1) * LANES]
                top = tile if top is None else jnp.maximum(top, tile)
        return s, top.max(axis=-1, keepdims=True)

    def attend(rows, s, mx):
        o = None
        den = None
        for si, (_, v, r0, n) in zip(s, chunks):
            e = jnp.exp2(si - mx)
            for j in range(n // LANES):
                tile = e[:, j * LANES:(j + 1) * LANES]
                den = tile if den is None else den + tile
            part = _dot(e.astype(BF16), v[r0:r0 + n, :])
            o = part if o is None else o + part
        o_ref[rows, :] = (o / den.sum(axis=-1, keepdims=True)).astype(BF16)

    pair = 2 if q_ref.shape[0] % (2 * bq) == 0 else 1

    def block(i, carry):
        rows = [pl.ds(pl.multiple_of((i * pair + u) * bq, bq), bq) for u in range(pair)]
        staged = [scores(r) for r in rows]
        for r, (s, mx) in zip(rows, staged):
            attend(r, s, mx)
        return carry

    lax.fori_loop(0, q_ref.shape[0] // (pair * bq), block, 0)


def _attn_a(rows, q, k, v, ctx_out):
    n_lat, n_ctx = rows.n_lat, rows.n_ctx
    ctx_blk = rows.t_lat // n_ctx
    lat = pl.pallas_call(
        functools.partial(_attn_a_kernel, n_kv=2),
        out_shape=jax.ShapeDtypeStruct((rows.t_lat, A_HEADS * A_V), BF16),
        grid=(rows.n_b, A_HEADS),
        in_specs=[pl.BlockSpec((n_lat, A_QK_PAD), lambda b, h: (b, h)),
                  pl.BlockSpec((n_lat, A_QK_PAD), lambda b, h: (b, h)),
                  pl.BlockSpec((n_ctx, A_QK_PAD), lambda b, h: (ctx_blk + b, h)),
                  pl.BlockSpec((n_lat, A_V), lambda b, h: (b, h)),
                  pl.BlockSpec((n_ctx, A_V), lambda b, h: (ctx_blk + b, h))],
        out_specs=pl.BlockSpec((n_lat, A_V), lambda b, h: (b, h)),
        compiler_params=_cparams(2),
        name="attn_a_lat",
    )(q, k, k, v, v)
    if not ctx_out:
        return lat
    ctx = pl.pallas_call(
        functools.partial(_attn_a_kernel, n_kv=1),
        out_shape=jax.ShapeDtypeStruct((rows.n_b * n_ctx, A_HEADS * A_V), BF16),
        grid=(rows.n_b, A_HEADS),
        in_specs=[pl.BlockSpec((n_ctx, A_QK_PAD), lambda b, h: (ctx_blk + b, h)),
                  pl.BlockSpec((n_ctx, A_QK_PAD), lambda b, h: (ctx_blk + b, h)),
                  pl.BlockSpec((n_ctx, A_V), lambda b, h: (ctx_blk + b, h))],
        out_specs=pl.BlockSpec((n_ctx, A_V), lambda b, h: (b, h)),
        compiler_params=_cparams(2),
        name="attn_a_ctx",
    )(q, k, v)
    return jnp.concatenate([lat, ctx], axis=0)


def _pool_kernel(prev_ref, cur_ref, next_ref, w_ref, s_ref, o_ref, ext_ref, *, rows):
    i = pl.program_id(0)
    bm = rows.bm
    is_ctx = i >= rows.lat_blocks
    n_l = jnp.where(is_ctx, rows.n_ctx, rows.n_lat)
    pos0 = jnp.where(is_ctx, 0, (i % rows.blocks_per_seq) * bm)
    ext_ref[0:POOL_HALO, :] = prev_ref[...]
    ext_ref[POOL_HALO:POOL_HALO + bm, :] = cur_ref[...]
    ext_ref[POOL_HALO + bm:, :] = next_ref[...]
    pos = pos0 + lax.broadcasted_iota(jnp.int32, (bm, 1), 0)
    for gi, w in enumerate(B_WINDOWS):
        cols = slice(gi * B_GROUP_W, (gi + 1) * B_GROUP_W)
        acc = jnp.zeros((bm, B_GROUP_W), F32)
        for dlt in range(-(w // 2), w // 2):
            x = ext_ref[POOL_HALO + dlt:POOL_HALO + dlt + bm, cols]
            ok = (pos + dlt >= 0) & (pos + dlt < n_l)
            acc = acc + jnp.where(ok, x, 0.0)
        cnt = jnp.minimum(pos + (w // 2 - 1), n_l - 1) - jnp.maximum(pos - w // 2, 0) + 1
        dev = acc / cnt.astype(F32) - cur_ref[:, cols]
        y = _dot(dev.astype(BF16), w_ref[gi]) * s_ref[:, cols]
        o_ref[:, cols] = y.astype(BF16)


def _pool(rows, pp, w_pool, pool_scale):
    t = pp.shape[0]
    bm = rows.bm
    assert rows.n_ctx == bm
    per = bm // POOL_HALO
    last = t // POOL_HALO - 1
    return pl.pallas_call(
        functools.partial(_pool_kernel, rows=rows),
        out_shape=jax.ShapeDtypeStruct((t, B_WIDTH), BF16),
        grid=(rows.all_blocks,),
        in_specs=[pl.BlockSpec((POOL_HALO, B_WIDTH), lambda i: (jnp.maximum(i * per - 1, 0), 0)),
                  pl.BlockSpec((bm, B_WIDTH), lambda i: (i, 0)),
                  pl.BlockSpec((POOL_HALO, B_WIDTH), lambda i: (jnp.minimum((i + 1) * per, last), 0)),
                  pl.BlockSpec((len(B_WINDOWS), B_GROUP_W, B_GROUP_W), lambda i: (0, 0, 0)),
                  pl.BlockSpec((1, B_WIDTH), lambda i: (0, 0))],
        out_specs=pl.BlockSpec((bm, B_WIDTH), lambda i: (i, 0)),
        scratch_shapes=[pltpu.VMEM((bm + 2 * POOL_HALO, B_WIDTH), F32)],
        compiler_params=_cparams(1),
        name="pool",
    )(pp, pp, pp, w_pool.astype(BF16), pool_scale.reshape(1, -1))


def _out_kernel(a1_ref, a2_ref, w1_ref, w2_ref, h_ref, m_ref, g_ref, wr1_ref, wr2_ref, br_ref,
                hn_ref, z_ref, ri_ref, rw_ref, cnt_ref, carry_ref):
    i = pl.program_id(0)
    bm = h_ref.shape[0]

    @pl.when(i == 0)
    def _():
        carry_ref[...] = jnp.zeros_like(carry_ref)

    m = m_ref[...]
    o = _dot(a1_ref[...], w1_ref[...]) + _dot(a2_ref[...], w2_ref[...])
    hn = h_ref[...] + m[2:3] * o
    hn_ref[...] = hn
    z = _norm_mod(hn, g_ref[...], m[3:4], m[4:5])
    z_hi = z.astype(BF16)
    z_ref[...] = z_hi
    z_lo = (z - z_hi.astype(F32)).astype(BF16)
    l2 = _dot(z_hi, wr1_ref[...])
    lg = l2[:, :ROUTER_W] + l2[:, ROUTER_W:] + _dot(z_lo, wr2_ref[...]) + br_ref[...]

    lane = lax.broadcasted_iota(jnp.int32, (bm, ROUTER_W), 1)
    low = jnp.float32(-3e38)
    is_g = lane < N_GROUPS
    glog = jnp.where(is_g, lg, low)
    gmax = glog.max(axis=-1, keepdims=True)
    g_idx = jnp.where(glog == gmax, lane, ROUTER_W).min(axis=-1, keepdims=True)
    g_gate = 1.0 / jnp.where(is_g, jnp.exp(lg - gmax), 0.0).sum(axis=-1, keepdims=True)
    lo = N_GROUPS + EXPERTS_PER_GROUP * g_idx
    el = jnp.where((lane >= lo) & (lane < lo + EXPERTS_PER_GROUP), lg, low)
    v1 = el.max(axis=-1, keepdims=True)
    i1 = jnp.where(el == v1, lane, ROUTER_W).min(axis=-1, keepdims=True)
    el2 = jnp.where(lane == i1, low, el)
    v2 = el2.max(axis=-1, keepdims=True)
    i2 = jnp.where(el2 == v2, lane, ROUTER_W).min(axis=-1, keepdims=True)
    e21 = jnp.exp(v2 - v1)
    w1 = g_gate * (1.0 / (1.0 + e21))
    w2 = g_gate * (e21 / (1.0 + e21))

    hit1 = lane == i1
    hit2 = lane == i2
    onehot = jnp.where(hit1 | hit2, 1.0, 0.0)
    r_i = lax.broadcasted_iota(jnp.int32, (bm, bm), 0)
    c_i = lax.broadcasted_iota(jnp.int32, (bm, bm), 1)
    before = _dot(jnp.where(r_i > c_i, 1.0, 0.0).astype(BF16), onehot.astype(BF16)) + carry_ref[0:1, :]
    rank1 = jnp.where(hit1, before, 0.0).sum(axis=-1, keepdims=True).astype(jnp.int32)
    rank2 = jnp.where(hit2, before, 0.0).sum(axis=-1, keepdims=True).astype(jnp.int32)
    total = carry_ref[0:1, :] + onehot.sum(axis=0, keepdims=True)
    carry_ref[...] = jnp.broadcast_to(total, carry_ref.shape)
    cnt_ref[...] = jnp.broadcast_to(total, cnt_ref.shape)

    ri_ref[...] = jnp.where(lane == 0, i1 - N_GROUPS, jnp.where(lane == 1, i2 - N_GROUPS,
                            jnp.where(lane == 2, rank1, jnp.where(lane == 3, rank2, 0))))
    rw_ref[...] = jnp.where(lane == 0, w1, jnp.where(lane == 1, w2, 0.0))


def _out_proj(rows, n_blocks, a1, a2, a2_col, w_out, h, mod, g2, wr, br):
    t, d = h.shape
    bm = rows.bm
    t_out = n_blocks * bm
    half = w_out.shape[0] // 2
    w = w_out.astype(BF16)
    wr_hi = wr.astype(BF16)
    wr_lo = (wr - wr_hi.astype(F32)).astype(BF16)
    const = lambda i: (0, 0)
    row = lambda i: (i, 0)
    return pl.pallas_call(
        _out_kernel,
        out_shape=(jax.ShapeDtypeStruct((t_out, d), F32),
                   jax.ShapeDtypeStruct((t_out, d), BF16),
                   jax.ShapeDtypeStruct((t_out, ROUTER_W), jnp.int32),
                   jax.ShapeDtypeStruct((t_out, ROUTER_W), F32),
                   jax.ShapeDtypeStruct((8, ROUTER_W), F32)),
        grid=(n_blocks,),
        in_specs=[pl.BlockSpec((bm, half), row),
                  pl.BlockSpec((bm, half), lambda i: (i, a2_col)),
                  pl.BlockSpec((half, d), lambda i: (0, 0)),
                  pl.BlockSpec((half, d), lambda i: (1, 0)),
                  pl.BlockSpec((bm, d), row),
                  pl.BlockSpec((None, N_MOD, d), lambda i: (rows.mod_index(i), 0, 0)),
                  pl.BlockSpec((1, d), const),
                  pl.BlockSpec((d, 2 * ROUTER_W), const),
                  pl.BlockSpec((d, ROUTER_W), const),
                  pl.BlockSpec((1, ROUTER_W), const)],
        out_specs=(pl.BlockSpec((bm, d), row),
                   pl.BlockSpec((bm, d), row),
                   pl.BlockSpec((bm, ROUTER_W), row),
                   pl.BlockSpec((bm, ROUTER_W), row),
                   pl.BlockSpec((8, ROUTER_W), const)),
        scratch_shapes=[pltpu.VMEM((8, ROUTER_W), F32)],
        compiler_params=_cparams(1),
        name="out_proj",
    )(a1, a2, w, w, h, mod, g2.reshape(1, d), jnp.concatenate([wr_hi, wr_lo], axis=1), wr_hi, br)


def _in_c_kernel(h_ref, m_ref, g_ref, win_ref, cos_ref, sin_ref, q_ref, k_ref, v_ref, *, q_scale):
    m = m_ref[...]
    z = _norm_mod(h_ref[...], g_ref[...], m[0:1], m[1:2])
    p = _dot(z.astype(BF16), win_ref[...])
    cos = cos_ref[...]
    sin = sin_ref[...]
    lane = lax.broadcasted_iota(jnp.int32, (1, LANES), 1)
    first = (lane % 32) < 16

    def rope(x):
        partner = jnp.where(first, pltpu.roll(x, LANES - 16, 1), pltpu.roll(x, 16, 1))
        return x * cos + partner * sin

    for tile in range(C_Q_W // LANES):
        cols = slice(tile * LANES, (tile + 1) * LANES)
        q_ref[:, cols] = (rope(p[:, cols]) * q_scale).astype(BF16)
    low_half = lane < C_HEAD_DIM
    for tile in range(C_KV_W // LANES):
        kk = rope(p[:, C_Q_W + tile * LANES:C_Q_W + (tile + 1) * LANES])
        vv = p[:, C_Q_W + C_KV_W + tile * LANES:C_Q_W + C_KV_W + (tile + 1) * LANES]
        ones = jnp.where(lane == C_HEAD_DIM, 1.0, 0.0)
        for half, (kh, vh) in enumerate(((kk, vv), (pltpu.roll(kk, C_HEAD_DIM, 1), pltpu.roll(vv, C_HEAD_DIM, 1)))):
            c0 = (2 * tile + half) * C_KV_PAD
            k_ref[:, c0:c0 + C_KV_PAD] = jnp.where(low_half, kh, 0.0).astype(BF16)
            v_ref[:, c0:c0 + C_KV_PAD] = jnp.where(low_half, vh, ones).astype(BF16)


def _in_c(rows, h, mod, g1, w_in, cos2, sin2):
    t, d = h.shape
    bm = rows.bm
    n_in = w_in.shape[1]
    const = lambda i: (0, 0)
    row = lambda i: (i, 0)
    return pl.pallas_call(
        functools.partial(_in_c_kernel, q_scale=float(C_HEAD_DIM ** -0.5 * LOG2E)),
        out_shape=(jax.ShapeDtypeStruct((t, C_Q_W), BF16),
                   jax.ShapeDtypeStruct((t, C_KV_HEADS * C_KV_PAD), BF16),
                   jax.ShapeDtypeStruct((t, C_KV_HEADS * C_KV_PAD), BF16)),
        grid=(rows.all_blocks,),
        in_specs=[pl.BlockSpec((bm, d), row),
                  pl.BlockSpec((None, N_MOD, d), lambda i: (rows.mod_index(i), 0, 0)),
                  pl.BlockSpec((1, d), const),
                  pl.BlockSpec((d, n_in), const),
                  pl.BlockSpec((bm, LANES), lambda i: (rows.pos_index(i), 0)),
                  pl.BlockSpec((bm, LANES), lambda i: (rows.pos_index(i), 0))],
        out_specs=(pl.BlockSpec((bm, C_Q_W), row),
                   pl.BlockSpec((bm, C_KV_HEADS * C_KV_PAD), row),
                   pl.BlockSpec((bm, C_KV_HEADS * C_KV_PAD), row)),
        compiler_params=_cparams(1),
        name="in_proj_c",
    )(h, mod, g1.reshape(1, d), w_in.astype(BF16), cos2, sin2)


def _sink_attend(q_ref, sink_ref, keys, vals, masks, o_ref, bq):
    def scores(kv):
        q8 = jnp.concatenate([q_ref[:, (kv * C_GROUP + g) * C_HEAD_DIM:(kv * C_GROUP + g + 1) * C_HEAD_DIM]
                              for g in range(C_GROUP)], axis=0)
        return [_dot_t(q8, k[:, kv * C_KV_PAD:kv * C_KV_PAD + C_HEAD_DIM]) for k in keys]

    s_next = scores(0)
    for kv in range(C_KV_HEADS):
        s = s_next
        if kv + 1 < C_KV_HEADS:
            s_next = scores(kv + 1)
        e, mxs = [], []
        for g in range(C_GROUP):
            sg = [si[g * bq:(g + 1) * bq] for si in s]
            sg = [si if msk is None else jnp.where(msk, si, NEG_INF) for si, msk in zip(sg, masks)]
            sink = sink_ref[kv * C_GROUP + g] * LOG2E
            tiles = [si[:, j * LANES:(j + 1) * LANES] for si in sg for j in range(si.shape[1] // LANES)]
            mx = jnp.maximum(sink, functools.reduce(jnp.maximum, tiles).max(axis=-1, keepdims=True))
            e.append([jnp.exp2(si - mx).astype(BF16) for si in sg])
            mxs.append((sink, mx))
        oe = None
        for piece, v in enumerate(vals):
            part = _dot(jnp.concatenate([eg[piece] for eg in e], axis=0), v[:, kv * C_KV_PAD:(kv + 1) * C_KV_PAD])
            oe = part if oe is None else oe + part
        for g, (sink, mx) in enumerate(mxs):
            og = oe[g * bq:(g + 1) * bq]
            den = og[:, C_HEAD_DIM:C_HEAD_DIM + 1] + jnp.exp2(sink - mx)
            c0 = (kv * C_GROUP + g) * C_HEAD_DIM
            o_ref[:, c0:c0 + C_HEAD_DIM] = (og[:, :C_HEAD_DIM] / den).astype(BF16)


def _attn_c_lat_kernel(sink_ref, q_ref, kp_ref, kc_ref, kn_ref, kx_ref, vp_ref, vc_ref, vn_ref, vx_ref, o_ref,
                       *, n_blk):
    n = pl.program_id(1)
    bq = C_WINDOW
    band = 3 * bq
    kband = jnp.concatenate([kp_ref[...], kc_ref[...], kn_ref[...]], axis=0)
    vband = jnp.concatenate([vp_ref[...], vc_ref[...], vn_ref[...]], axis=0)
    qi = lax.broadcasted_iota(jnp.int32, (bq, band), 0)
    kj = lax.broadcasted_iota(jnp.int32, (bq, band), 1)
    rel = qi - (kj - bq)
    ok = (jnp.abs(rel) <= C_WINDOW) & ((kj >= bq) | (n > 0)) & ((kj < 2 * bq) | (n < n_blk - 1))
    _sink_attend(q_ref, sink_ref, [kband, kx_ref[...]], [vband, vx_ref[...]], [ok, None], o_ref, bq)


def _attn_c_ctx_kernel(sink_ref, q_ref, kx_ref, vx_ref, o_ref):
    _sink_attend(q_ref, sink_ref, [kx_ref[...]], [vx_ref[...]], [None], o_ref, q_ref.shape[0])


def _attn_c(rows, q, k, v, sink, ctx_out):
    t = q.shape[0]
    bq = C_WINDOW
    n_blk = rows.n_lat // bq
    n_ctx = rows.n_ctx
    ctx_blk = rows.t_lat // n_ctx
    kvw = C_KV_HEADS * C_KV_PAD
    del t
    smem = pl.BlockSpec(memory_space=pltpu.SMEM)
    prev = lambda b, n: (b * n_blk + jnp.maximum(n - 1, 0), 0)
    cur = lambda b, n: (b * n_blk + n, 0)
    nxt = lambda b, n: (b * n_blk + jnp.minimum(n + 1, n_blk - 1), 0)
    cx = lambda b, n: (ctx_blk + b, 0)
    kv_specs = [pl.BlockSpec((bq, kvw), prev), pl.BlockSpec((bq, kvw), cur),
                pl.BlockSpec((bq, kvw), nxt), pl.BlockSpec((n_ctx, kvw), cx)]
    lat = pl.pallas_call(
        functools.partial(_attn_c_lat_kernel, n_blk=n_blk),
        out_shape=jax.ShapeDtypeStruct((rows.t_lat, C_Q_W), BF16),
        grid=(rows.n_b, n_blk),
        in_specs=[smem, pl.BlockSpec((bq, C_Q_W), cur)] + kv_specs + kv_specs,
        out_specs=pl.BlockSpec((bq, C_Q_W), cur),
        compiler_params=_cparams(2),
        name="attn_c_lat",
    )(sink, q, k, k, k, k, v, v, v, v)
    if not ctx_out:
        return lat
    cxb = lambda b: (ctx_blk + b, 0)
    ctx = pl.pallas_call(
        _attn_c_ctx_kernel,
        out_shape=jax.ShapeDtypeStruct((rows.n_b * n_ctx, C_Q_W), BF16),
        grid=(rows.n_b,),
        in_specs=[smem, pl.BlockSpec((n_ctx, C_Q_W), cxb), pl.BlockSpec((n_ctx, kvw), cxb),
                  pl.BlockSpec((n_ctx, kvw), cxb)],
        out_specs=pl.BlockSpec((n_ctx, C_Q_W), lambda b: (b, 0)),
        compiler_params=_cparams(1),
        name="attn_c_ctx",
    )(sink, q, k, v)
    return jnp.concatenate([lat, ctx], axis=0)


def _moe_kernel(be_ref, nu_ref, x_ref, wg_ref, wu_ref, wd_ref, y_ref, wgu_s, wd_s):
    i = pl.program_id(0)
    used = i < nu_ref[0]
    fresh = (i == 0) | (be_ref[i] != be_ref[jnp.maximum(i - 1, 0)])

    @pl.when(used & fresh)
    def _():
        wgu_s[:, :D_EXPERT] = wg_ref[...].astype(BF16)
        wgu_s[:, D_EXPERT:] = wu_ref[...].astype(BF16)
        wd_s[...] = wd_ref[...].astype(BF16)

    @pl.when(used)
    def _():
        gu = _dot(x_ref[...], wgu_s[...])
        g = gu[:, :D_EXPERT]
        a = (g * jax.nn.sigmoid(g)) * gu[:, D_EXPERT:]
        y_ref[...] = _dot(a.astype(BF16), wd_s[...]).astype(BF16)

    @pl.when(jnp.logical_not(used))
    def _():
        y_ref[...] = jnp.zeros_like(y_ref)


def _moe_blocks(layer, xs, block_e, n_used, w_gate, w_up, w_down):
    r, d = xs.shape
    bm = MOE_ROWS
    nb = r // bm
    return pl.pallas_call(
        _moe_kernel,
        out_shape=jax.ShapeDtypeStruct((r, d), BF16),
        grid_spec=pltpu.PrefetchScalarGridSpec(
            num_scalar_prefetch=2,
            grid=(nb,),
            in_specs=[pl.BlockSpec((bm, d), lambda i, be, nu: (i, 0)),
                      pl.BlockSpec((None, None, d, D_EXPERT), lambda i, be, nu: (layer, be[i], 0, 0)),
                      pl.BlockSpec((None, None, d, D_EXPERT), lambda i, be, nu: (layer, be[i], 0, 0)),
                      pl.BlockSpec((None, None, D_EXPERT, d), lambda i, be, nu: (layer, be[i], 0, 0))],
            out_specs=pl.BlockSpec((bm, d), lambda i, be, nu: (i, 0)),
            scratch_shapes=[pltpu.VMEM((d, 2 * D_EXPERT), BF16), pltpu.VMEM((D_EXPERT, d), BF16)]),
        compiler_params=_cparams(1),
        name="moe_experts",
    )(block_e, n_used, xs, w_gate, w_up, w_down)


def _plan(ri, cnt, bm):
    n_t = ri.shape[0]
    n_tk = n_t * TOP_K
    nb = -(-(n_tk + N_EXPERTS * (bm - 1)) // bm)
    counts = cnt[0, N_GROUPS:N_GROUPS + N_EXPERTS].astype(jnp.int32)
    padded = (counts + bm - 1) // bm * bm
    pends = jnp.cumsum(padded)
    dest = (pends - padded)[ri[:, :TOP_K]] + ri[:, TOP_K:2 * TOP_K]
    row_tok = (jnp.arange(nb * bm, dtype=jnp.int32) % n_t).at[dest.reshape(-1)].set(
        jnp.arange(n_tk, dtype=jnp.int32) // TOP_K, unique_indices=True, mode='promise_in_bounds')
    n_used = (pends[-1] // bm).astype(jnp.int32)
    blk = jnp.arange(nb, dtype=jnp.int32)
    block_e = jnp.sum((blk[:, None] * bm >= pends[None, :]).astype(jnp.int32), axis=1)
    block_e = jnp.minimum(block_e, N_EXPERTS - 1)
    block_e = jnp.where(blk < n_used, block_e, block_e[n_used - 1])
    return row_tok, block_e, n_used.reshape(1), dest


def _combine_kernel(h_ref, y0_ref, y1_ref, rw_ref, m_ref, g_ref, o_ref, *, final):
    rw = rw_ref[...]
    y = rw[:, 0:1] * y0_ref[...].astype(F32) + rw[:, 1:2] * y1_ref[...].astype(F32)
    hn = h_ref[...] + m_ref[...][5:6] * y
    if final:
        hn = _rms(hn) * g_ref[...]
    o_ref[...] = hn


def _combine(rows, n_blocks, h, y0, y1, rw, mod, final_g, final):
    d = h.shape[1]
    bm = rows.bm
    row = lambda i: (i, 0)
    return pl.pallas_call(
        functools.partial(_combine_kernel, final=final),
        out_shape=jax.ShapeDtypeStruct((n_blocks * bm, d), F32),
        grid=(n_blocks,),
        in_specs=[pl.BlockSpec((bm, d), row), pl.BlockSpec((bm, d), row), pl.BlockSpec((bm, d), row),
                  pl.BlockSpec((bm, ROUTER_W), row),
                  pl.BlockSpec((None, N_MOD, d), lambda i: (rows.mod_index(i), 0, 0)),
                  pl.BlockSpec((1, d), lambda i: (0, 0))],
        out_specs=pl.BlockSpec((bm, d), row),
        compiler_params=_cparams(1),
        name="combine",
    )(h, y0, y1, rw, mod, final_g.reshape(1, d))


def kernel(x, c, ctx, c_ctx, mod_w, mod_b, norm1_g, norm2_g, final_g, a_w_in, a_q_norm_g, a_kv_norm_g, a_w_uq,
           a_w_ukv, a_w_pool, a_pool_scale, a_w_out, c_w_in, c_sink, c_w_out, r_w_group, r_b_group, r_w_expert,
           r_b_expert, e_w_gate, e_w_up, e_w_down):
    n_b, n_lat, d = x.shape
    n_ctx = ctx.shape[1]
    rows = _Rows(n_b, n_lat, n_ctx, ROW_BLOCK)
    pool_rows = _Rows(n_b, n_lat, n_ctx, POOL_ROWS)
    depth = mod_w.shape[0]

    cvec = jnp.zeros((8, d), F32).at[:n_b].set(c).at[n_b].set(c_ctx)
    mods = _modulation(cvec, mod_w, mod_b).reshape(depth, 8, N_MOD, d)

    cos, sin = _rope_tables(n_lat, ROW_BLOCK)
    zeros = jnp.zeros_like(cos)
    cos_a, sin_a = jnp.concatenate([cos, zeros], axis=1), jnp.concatenate([sin, zeros], axis=1)
    cos_c, sin_c = jnp.concatenate([cos, cos], axis=1), jnp.concatenate([sin, sin], axis=1)

    h = jnp.concatenate([x.reshape(-1, d), ctx.reshape(-1, d)], axis=0)
    for i in range(depth):
        ctx_out = i < depth - 1
        j = i // 2
        mod = mods[i]
        if i % 2 == 0:
            q, k, v, pp = _in_a(rows, h, mod, norm1_g[i], a_w_in[j], a_q_norm_g[j], a_w_uq[j], a_kv_norm_g[j],
                                a_w_ukv[j], cos_a, sin_a)
            a1 = _attn_a(rows, q, k, v, ctx_out)
            a2 = _pool(pool_rows, pp, a_w_pool[j], a_pool_scale[j])
            a2_col, w_out = 0, a_w_out[j]
        else:
            q, k, v = _in_c(rows, h, mod, norm1_g[i], c_w_in[j], cos_c, sin_c)
            a1 = a2 = _attn_c(rows, q, k, v, c_sink[j], ctx_out)
            a2_col, w_out = 1, c_w_out[j]
        n_blocks = rows.all_blocks if ctx_out else rows.lat_blocks
        wr = jnp.zeros((d, ROUTER_W), F32).at[:, :N_GROUPS].set(r_w_group[i])
        wr = wr.at[:, N_GROUPS:N_GROUPS + N_EXPERTS].set(r_w_expert[i])
        br = jnp.zeros((1, ROUTER_W), F32).at[0, :N_GROUPS].set(r_b_group[i])
        br = br.at[0, N_GROUPS:N_GROUPS + N_EXPERTS].set(r_b_expert[i])
        hn, z, ri, rw, cnt = _out_proj(rows, n_blocks, a1, a2, a2_col, w_out, h, mod, norm2_g[i], wr, br)

        row_tok, block_e, n_used, dest = _plan(ri, cnt, MOE_ROWS)
        xs = z.at[row_tok].get(mode='promise_in_bounds')
        y = _moe_blocks(i, xs, block_e, n_used, e_w_gate, e_w_up, e_w_down)
        y0 = y.at[dest[:, 0]].get(mode='promise_in_bounds')
        y1 = y.at[dest[:, 1]].get(mode='promise_in_bounds')
        h = _combine(rows, n_blocks, hn, y0, y1, rw, mod, final_g, final=not ctx_out)
    return h.reshape(n_b, n_lat, d)
```

```python
import functools

import numpy as np
import jax
import jax.numpy as jnp
from jax import lax
from jax.experimental import pallas as pl
from jax.experimental.pallas import tpu as pltpu

F32 = jnp.float32
BF16 = jnp.bfloat16

D_MODEL = 2048
DEPTH = 4
GRID_W = 64
EPS = 1e-6
ROPE_BASE = 10000.0
NEG_INF = -1e30
N_MOD = 6

A_NOPE = 128
A_ROPE = 64
A_V = 128
A_HEADS = 8
A_Q_RANK = 512
A_KV_RANK = 256
A_QK_PAD = 256
B_WINDOWS = (2, 4, 8, 16)
B_GROUP_W = 256
B_WIDTH = 1024
POOL_HALO = 8

C_HEAD_DIM = 64
C_HEADS = 32
C_KV_HEADS = 4
C_GROUP = 8
C_WINDOW = 128
C_Q_W = C_HEADS * C_HEAD_DIM
C_KV_W = C_KV_HEADS * C_HEAD_DIM

N_GROUPS = 4
EXPERTS_PER_GROUP = 8
N_EXPERTS = 32
TOP_K = 2
D_EXPERT = 512
ROUTER_W = 128

ROW_BLOCK = 512
POOL_ROWS = 256
MOE_ROWS = 512
LANES = 128
VMEM_LIMIT = 56 * 1024 * 1024
LOG2E = 1.4426950408889634
ATTN_A_ROWS = 256
ATTN_A_KEYS = 512
C_KV_PAD = 128


def _cparams(n_axes):
    return pltpu.CompilerParams(dimension_semantics=("arbitrary",) * n_axes,
                                vmem_limit_bytes=VMEM_LIMIT)


def _dot(a, b):
    return jnp.dot(a, b, preferred_element_type=F32)


def _dot_t(a, b):
    return lax.dot_general(a, b, (((1,), (1,)), ((), ())), preferred_element_type=F32)


def _rms(x):
    return x * lax.rsqrt(jnp.mean(x * x, axis=-1, keepdims=True) + EPS)


def _norm_mod(h, g, shift, scale):
    return (_rms(h) * g) * (1 + scale) + shift


def _mod_kernel(s_ref, w_ref, b_ref, o_ref):
    s = s_ref[...]
    s = s * jax.nn.sigmoid(s)
    o_ref[...] = _dot(s.astype(BF16), w_ref[...].astype(BF16)) + b_ref[...]


def _modulation(cvec, mod_w, mod_b):
    depth, d, n = mod_w.shape
    tn = 1024
    return pl.pallas_call(
        _mod_kernel,
        out_shape=jax.ShapeDtypeStruct((depth, 8, n), F32),
        grid=(depth, n // tn),
        in_specs=[pl.BlockSpec((8, d), lambda l, j: (0, 0)),
                  pl.BlockSpec((None, d, tn), lambda l, j: (l, 0, j)),
                  pl.BlockSpec((None, 1, tn), lambda l, j: (l, 0, j))],
        out_specs=pl.BlockSpec((None, 8, tn), lambda l, j: (l, 0, j)),
        compiler_params=_cparams(2),
        name="modulation",
    )(cvec, mod_w, mod_b.reshape(depth, 1, n))


class _Rows:
    def __init__(self, n_b, n_lat, n_ctx, bm):
        self.n_b, self.n_lat, self.n_ctx, self.bm = n_b, n_lat, n_ctx, bm
        self.t_lat = n_b * n_lat
        self.t_all = self.t_lat + n_b * n_ctx
        assert n_lat % bm == 0 and (n_b * n_ctx) % bm == 0
        self.lat_blocks = self.t_lat // bm
        self.all_blocks = self.t_all // bm
        self.blocks_per_seq = n_lat // bm

    def mod_index(self, i):
        return jnp.minimum(i // self.blocks_per_seq, self.n_b)

    def pos_index(self, i):
        return jnp.where(i < self.lat_blocks, i % self.blocks_per_seq, self.blocks_per_seq)


def _rope_tables(n_lat, n_ctx):
    axis_dim = A_ROPE // 2
    inv_freq = ROPE_BASE ** (-jnp.arange(axis_dim // 2, dtype=F32) * 2.0 / axis_dim)
    rows = n_lat // GRID_W
    row = jnp.repeat(jnp.arange(rows, dtype=F32), GRID_W)
    col = jnp.tile(jnp.arange(GRID_W, dtype=F32), rows)
    ang_r = row[:, None] * inv_freq
    ang_c = col[:, None] * inv_freq
    cr, sr, cc, sc = jnp.cos(ang_r), jnp.sin(ang_r), jnp.cos(ang_c), jnp.sin(ang_c)
    cos = jnp.concatenate([cr, cr, cc, cc], axis=-1)
    sin = jnp.concatenate([-sr, sr, -sc, sc], axis=-1)
    cos = jnp.concatenate([cos, jnp.ones((n_ctx, 64), F32)], axis=0)
    sin = jnp.concatenate([sin, jnp.zeros((n_ctx, 64), F32)], axis=0)
    return cos, sin


_ROPE_SWAP = np.concatenate([np.arange(16, 32), np.arange(0, 16), np.arange(48, 64), np.arange(32, 48)])


def _in_a_kernel(h_ref, m_ref, g_ref, win_ref, gq_ref, wuq_ref, gkv_ref, wk_ref, wv_ref, cos_ref, sin_ref,
                 q_ref, k_ref, v_ref, pp_ref, *, q_scale):
    m = m_ref[...]
    z = _norm_mod(h_ref[...], g_ref[...], m[0:1], m[1:2])
    p = _dot(z.astype(BF16), win_ref[...])
    cos = cos_ref[...]
    sin = sin_ref[...]

    cqn = _rms(p[:, :A_Q_RANK]) * gq_ref[...]
    qraw = _dot(cqn.astype(BF16), wuq_ref[...])
    for hd in range(A_HEADS):
        c0 = hd * A_QK_PAD
        t = qraw[:, c0 + A_NOPE:c0 + A_QK_PAD]
        rot = t * cos + pltpu.roll(t, 64, 1) * sin
        q_ref[:, c0:c0 + A_NOPE] = (qraw[:, c0:c0 + A_NOPE] * q_scale).astype(BF16)
        q_ref[:, c0 + A_NOPE:c0 + A_QK_PAD] = (rot * q_scale).astype(BF16)

    ckvn = (_rms(p[:, A_Q_RANK:A_Q_RANK + A_KV_RANK]) * gkv_ref[...]).astype(BF16)
    kn = _dot(ckvn, wk_ref[...])
    kt = p[:, 768:896]
    krot = (kt * cos + pltpu.roll(kt, 64, 1) * sin).astype(BF16)
    for hd in range(A_HEADS):
        c0 = hd * A_QK_PAD
        k_ref[:, c0:c0 + A_NOPE] = kn[:, hd * A_NOPE:(hd + 1) * A_NOPE].astype(BF16)
        k_ref[:, c0 + A_NOPE:c0 + A_QK_PAD] = krot
    v_ref[...] = _dot(ckvn, wv_ref[...]).astype(BF16)
    pp_ref[...] = p[:, 896:]


def _in_a(rows, h, mod, g1, w_in, gq, w_uq, gkv, w_ukv, cos2, sin2):
    t, d = h.shape
    off_rope = A_Q_RANK + A_KV_RANK
    win = jnp.concatenate([w_in[:, :off_rope + A_ROPE], w_in[:, off_rope + _ROPE_SWAP],
                           w_in[:, off_rope + A_ROPE:]], axis=1).astype(BF16)
    wq = w_uq.reshape(A_Q_RANK, A_HEADS, A_NOPE + A_ROPE)
    wuq = jnp.concatenate([wq, wq[:, :, A_NOPE + _ROPE_SWAP]], axis=-1).reshape(A_Q_RANK, A_HEADS * A_QK_PAD)
    wkv = w_ukv.reshape(A_KV_RANK, A_HEADS, A_NOPE + A_V)
    wk = wkv[:, :, :A_NOPE].reshape(A_KV_RANK, A_HEADS * A_NOPE).astype(BF16)
    wv = wkv[:, :, A_NOPE:].reshape(A_KV_RANK, A_HEADS * A_V).astype(BF16)
    n_in = win.shape[1]
    bm = rows.bm
    const = lambda i: (0, 0)
    row = lambda i: (i, 0)
    return pl.pallas_call(
        functools.partial(_in_a_kernel, q_scale=float((A_NOPE + A_ROPE) ** -0.5 * LOG2E)),
        out_shape=(jax.ShapeDtypeStruct((t, A_HEADS * A_QK_PAD), BF16),
                   jax.ShapeDtypeStruct((t, A_HEADS * A_QK_PAD), BF16),
                   jax.ShapeDtypeStruct((t, A_HEADS * A_V), BF16),
                   jax.ShapeDtypeStruct((t, B_WIDTH), F32)),
        grid=(rows.all_blocks,),
        in_specs=[pl.BlockSpec((bm, d), row),
                  pl.BlockSpec((None, N_MOD, d), lambda i: (rows.mod_index(i), 0, 0)),
                  pl.BlockSpec((1, d), const),
                  pl.BlockSpec((d, n_in), const),
                  pl.BlockSpec((1, A_Q_RANK), const),
                  pl.BlockSpec((A_Q_RANK, A_HEADS * A_QK_PAD), const),
                  pl.BlockSpec((1, A_KV_RANK), const),
                  pl.BlockSpec((A_KV_RANK, A_HEADS * A_NOPE), const),
                  pl.BlockSpec((A_KV_RANK, A_HEADS * A_V), const),
                  pl.BlockSpec((bm, LANES), lambda i: (rows.pos_index(i), 0)),
                  pl.BlockSpec((bm, LANES), lambda i: (rows.pos_index(i), 0))],
        out_specs=(pl.BlockSpec((bm, A_HEADS * A_QK_PAD), row),
                   pl.BlockSpec((bm, A_HEADS * A_QK_PAD), row),
                   pl.BlockSpec((bm, A_HEADS * A_V), row),
                   pl.BlockSpec((bm, B_WIDTH), row)),
        compiler_params=_cparams(1),
        name="in_proj_a",
    )(h, mod, g1.reshape(1, d), win, gq.reshape(1, -1), wuq.astype(BF16), gkv.reshape(1, -1), wk, wv, cos2, sin2)


def _attn_a_kernel(*refs, n_kv):
    q_ref = refs[0]
    k_refs = refs[1:1 + n_kv]
    v_refs = refs[1 + n_kv:1 + 2 * n_kv]
    o_ref = refs[1 + 2 * n_kv]
    chunks = []
    for k, v in zip(k_refs, v_refs):
        for r0 in range(0, k.shape[0], ATTN_A_KEYS):
            chunks.append((k, v, r0, min(ATTN_A_KEYS, k.shape[0] - r0)))
    bq = min(ATTN_A_ROWS, q_ref.shape[0])

    def scores(rows):
        q = q_ref[rows, :]
        s = []
        top = None
        for k, _, r0, n in chunks:
            si = _dot_t(q, k[r0:r0 + n, :])
            s.append(si)
            for j in range(n // LANES):
                tile = si[:, j * LANES:(j + 1) * LANES]
                top = tile if top is None else jnp.maximum(top, tile)
        return s, top.max(axis=-1, keepdims=True)

    def attend(rows, s, mx):
        o = None
        den = None
        for si, (_, v, r0, n) in zip(s, chunks):
            e = jnp.exp2(si - mx)
            for j in range(n // LANES):
                tile = e[:, j * LANES:(j + 1) * LANES]
                den = tile if den is None else den + tile
            part = _dot(e.astype(BF16), v[r0:r0 + n, :])
            o = part if o is None else o + part
        o_ref[rows, :] = (o / den.sum(axis=-1, keepdims=True)).astype(BF16)

    pair = 2 if q_ref.shape[0] % (2 * bq) == 0 else 1

    def block(i, carry):
        rows = [pl.ds(pl.multiple_of((i * pair + u) * bq, bq), bq) for u in range(pair)]
        staged = [scores(r) for r in rows]
        for r, (s, mx) in zip(rows, staged):
            attend(r, s, mx)
        return carry

    lax.fori_loop(0, q_ref.shape[0] // (pair * bq), block, 0)


def _attn_a(rows, q, k, v, ctx_out):
    n_lat, n_ctx = rows.n_lat, rows.n_ctx
    ctx_blk = rows.t_lat // n_ctx
    lat = pl.pallas_call(
        functools.partial(_attn_a_kernel, n_kv=2),
        out_shape=jax.ShapeDtypeStruct((rows.t_lat, A_HEADS * A_V), BF16),
        grid=(rows.n_b, A_HEADS),
        in_specs=[pl.BlockSpec((n_lat, A_QK_PAD), lambda b, h: (b, h)),
                  pl.BlockSpec((n_lat, A_QK_PAD), lambda b, h: (b, h)),
                  pl.BlockSpec((n_ctx, A_QK_PAD), lambda b, h: (ctx_blk + b, h)),
                  pl.BlockSpec((n_lat, A_V), lambda b, h: (b, h)),
                  pl.BlockSpec((n_ctx, A_V), lambda b, h: (ctx_blk + b, h))],
        out_specs=pl.BlockSpec((n_lat, A_V), lambda b, h: (b, h)),
        compiler_params=_cparams(2),
        name="attn_a_lat",
    )(q, k, k, v, v)
    if not ctx_out:
        return lat, None
    ctx = pl.pallas_call(
        functools.partial(_attn_a_kernel, n_kv=1),
        out_shape=jax.ShapeDtypeStruct((rows.n_b * n_ctx, A_HEADS * A_V), BF16),
        grid=(rows.n_b, A_HEADS),
        in_specs=[pl.BlockSpec((n_ctx, A_QK_PAD), lambda b, h: (ctx_blk + b, h)),
                  pl.BlockSpec((n_ctx, A_QK_PAD), lambda b, h: (ctx_blk + b, h)),
                  pl.BlockSpec((n_ctx, A_V), lambda b, h: (ctx_blk + b, h))],
        out_specs=pl.BlockSpec((n_ctx, A_V), lambda b, h: (b, h)),
        compiler_params=_cparams(2),
        name="attn_a_ctx",
    )(q, k, v)
    return lat, ctx


def _pool_kernel(prev_ref, cur_ref, next_ref, w_ref, s_ref, o_ref, ext_ref, *, rows):
    i = pl.program_id(0)
    bm = rows.bm
    is_ctx = i >= rows.lat_blocks
    n_l = jnp.where(is_ctx, rows.n_ctx, rows.n_lat)
    pos0 = jnp.where(is_ctx, 0, (i % rows.blocks_per_seq) * bm)
    ext_ref[0:POOL_HALO, :] = prev_ref[...]
    ext_ref[POOL_HALO:POOL_HALO + bm, :] = cur_ref[...]
    ext_ref[POOL_HALO + bm:, :] = next_ref[...]
    pos = pos0 + lax.broadcasted_iota(jnp.int32, (bm, 1), 0)
    for gi, w in enumerate(B_WINDOWS):
        cols = slice(gi * B_GROUP_W, (gi + 1) * B_GROUP_W)
        acc = jnp.zeros((bm, B_GROUP_W), F32)
        for dlt in range(-(w // 2), w // 2):
            x = ext_ref[POOL_HALO + dlt:POOL_HALO + dlt + bm, cols]
            ok = (pos + dlt >= 0) & (pos + dlt < n_l)
            acc = acc + jnp.where(ok, x, 0.0)
        cnt = jnp.minimum(pos + (w // 2 - 1), n_l - 1) - jnp.maximum(pos - w // 2, 0) + 1
        dev = acc / cnt.astype(F32) - cur_ref[:, cols]
        y = _dot(dev.astype(BF16), w_ref[gi]) * s_ref[:, cols]
        o_ref[:, cols] = y.astype(BF16)


def _pool(rows, pp, w_pool, pool_scale):
    t = pp.shape[0]
    bm = rows.bm
    assert rows.n_ctx == bm
    per = bm // POOL_HALO
    last = t // POOL_HALO - 1
    return pl.pallas_call(
        functools.partial(_pool_kernel, rows=rows),
        out_shape=jax.ShapeDtypeStruct((t, B_WIDTH), BF16),
        grid=(rows.all_blocks,),
        in_specs=[pl.BlockSpec((POOL_HALO, B_WIDTH), lambda i: (jnp.maximum(i * per - 1, 0), 0)),
                  pl.BlockSpec((bm, B_WIDTH), lambda i: (i, 0)),
                  pl.BlockSpec((POOL_HALO, B_WIDTH), lambda i: (jnp.minimum((i + 1) * per, last), 0)),
                  pl.BlockSpec((len(B_WINDOWS), B_GROUP_W, B_GROUP_W), lambda i: (0, 0, 0)),
                  pl.BlockSpec((1, B_WIDTH), lambda i: (0, 0))],
        out_specs=pl.BlockSpec((bm, B_WIDTH), lambda i: (i, 0)),
        scratch_shapes=[pltpu.VMEM((bm + 2 * POOL_HALO, B_WIDTH), F32)],
        compiler_params=_cparams(1),
        name="pool",
    )(pp, pp, pp, w_pool.astype(BF16), pool_scale.reshape(1, -1))


def _out_kernel(*refs, split, lat_blocks):
    i = pl.program_id(0)
    halves = []
    n = 0
    for is_pair in split:
        if is_pair:
            halves.append(jnp.where(i < lat_blocks, refs[n][...], refs[n + 1][...]))
        else:
            halves.append(refs[n][...])
        n += 2 if is_pair else 1
    (w1_ref, w2_ref, h_ref, m_ref, g_ref, wr1_ref, wr2_ref, br_ref,
     hn_ref, z_ref, ri_ref, rw_ref, cnt_ref, carry_ref) = refs[n:]
    bm = h_ref.shape[0]

    @pl.when(i == 0)
    def _():
        carry_ref[...] = jnp.zeros_like(carry_ref)

    m = m_ref[...]
    o = _dot(halves[0], w1_ref[...]) + _dot(halves[1], w2_ref[...])
    hn = h_ref[...] + m[2:3] * o
    hn_ref[...] = hn
    z = _norm_mod(hn, g_ref[...], m[3:4], m[4:5])
    z_hi = z.astype(BF16)
    z_ref[...] = z_hi
    z_lo = (z - z_hi.astype(F32)).astype(BF16)
    l2 = _dot(z_hi, wr1_ref[...])
    lg = l2[:, :ROUTER_W] + l2[:, ROUTER_W:] + _dot(z_lo, wr2_ref[...]) + br_ref[...]

    lane = lax.broadcasted_iota(jnp.int32, (bm, ROUTER_W), 1)
    low = jnp.float32(-3e38)
    is_g = lane < N_GROUPS
    glog = jnp.where(is_g, lg, low)
    gmax = glog.max(axis=-1, keepdims=True)
    g_idx = jnp.where(glog == gmax, lane, ROUTER_W).min(axis=-1, keepdims=True)
    g_gate = 1.0 / jnp.where(is_g, jnp.exp(lg - gmax), 0.0).sum(axis=-1, keepdims=True)
    lo = N_GROUPS + EXPERTS_PER_GROUP * g_idx
    el = jnp.where((lane >= lo) & (lane < lo + EXPERTS_PER_GROUP), lg, low)
    v1 = el.max(axis=-1, keepdims=True)
    i1 = jnp.where(el == v1, lane, ROUTER_W).min(axis=-1, keepdims=True)
    el2 = jnp.where(lane == i1, low, el)
    v2 = el2.max(axis=-1, keepdims=True)
    i2 = jnp.where(el2 == v2, lane, ROUTER_W).min(axis=-1, keepdims=True)
    e21 = jnp.exp(v2 - v1)
    w1 = g_gate * (1.0 / (1.0 + e21))
    w2 = g_gate * (e21 / (1.0 + e21))

    hit1 = lane == i1
    hit2 = lane == i2
    onehot = jnp.where(hit1 | hit2, 1.0, 0.0)
    r_i = lax.broadcasted_iota(jnp.int32, (bm, bm), 0)
    c_i = lax.broadcasted_iota(jnp.int32, (bm, bm), 1)
    before = _dot(jnp.where(r_i > c_i, 1.0, 0.0).astype(BF16), onehot.astype(BF16)) + carry_ref[0:1, :]
    rank1 = jnp.where(hit1, before, 0.0).sum(axis=-1, keepdims=True).astype(jnp.int32)
    rank2 = jnp.where(hit2, before, 0.0).sum(axis=-1, keepdims=True).astype(jnp.int32)
    total = carry_ref[0:1, :] + onehot.sum(axis=0, keepdims=True)
    carry_ref[...] = jnp.broadcast_to(total, carry_ref.shape)
    cnt_ref[...] = jnp.broadcast_to(total, cnt_ref.shape)

    ri_ref[...] = jnp.where(lane == 0, i1 - N_GROUPS, jnp.where(lane == 1, i2 - N_GROUPS,
                            jnp.where(lane == 2, rank1, jnp.where(lane == 3, rank2, 0))))
    rw_ref[...] = jnp.where(lane == 0, w1, jnp.where(lane == 1, w2, 0.0))


def _out_proj(rows, n_blocks, halves, w_out, h, mod, g2, wr, br):
    t, d = h.shape
    bm = rows.bm
    t_out = n_blocks * bm
    half = w_out.shape[0] // 2
    w = w_out.astype(BF16)
    wr_hi = wr.astype(BF16)
    wr_lo = (wr - wr_hi.astype(F32)).astype(BF16)
    const = lambda i: (0, 0)
    row = lambda i: (i, 0)
    lat_blocks = rows.lat_blocks
    a_specs, a_args = [], []
    for lat, ctx, col in halves:
        if ctx is None:
            a_specs.append(pl.BlockSpec((bm, half), lambda i, col=col: (i, col)))
            a_args.append(lat)
        else:
            a_specs.append(pl.BlockSpec((bm, half), lambda i, col=col: (jnp.minimum(i, lat_blocks - 1), col)))
            a_specs.append(pl.BlockSpec((bm, half), lambda i, col=col: (jnp.maximum(i - lat_blocks, 0), col)))
            a_args += [lat, ctx]
    return pl.pallas_call(
        functools.partial(_out_kernel, split=tuple(ctx is not None for _, ctx, _ in halves), lat_blocks=lat_blocks),
        out_shape=(jax.ShapeDtypeStruct((t_out, d), F32),
                   jax.ShapeDtypeStruct((t_out, d), BF16),
                   jax.ShapeDtypeStruct((t_out, ROUTER_W), jnp.int32),
                   jax.ShapeDtypeStruct((t_out, ROUTER_W), F32),
                   jax.ShapeDtypeStruct((8, ROUTER_W), F32)),
        grid=(n_blocks,),
        in_specs=a_specs + [
                  pl.BlockSpec((half, d), lambda i: (0, 0)),
                  pl.BlockSpec((half, d), lambda i: (1, 0)),
                  pl.BlockSpec((bm, d), row),
                  pl.BlockSpec((None, N_MOD, d), lambda i: (rows.mod_index(i), 0, 0)),
                  pl.BlockSpec((1, d), const),
                  pl.BlockSpec((d, 2 * ROUTER_W), const),
                  pl.BlockSpec((d, ROUTER_W), const),
                  pl.BlockSpec((1, ROUTER_W), const)],
        out_specs=(pl.BlockSpec((bm, d), row),
                   pl.BlockSpec((bm, d), row),
                   pl.BlockSpec((bm, ROUTER_W), row),
                   pl.BlockSpec((bm, ROUTER_W), row),
                   pl.BlockSpec((8, ROUTER_W), const)),
        scratch_shapes=[pltpu.VMEM((8, ROUTER_W), F32)],
        compiler_params=_cparams(1),
        name="out_proj",
    )(*a_args, w, w, h, mod, g2.reshape(1, d), jnp.concatenate([wr_hi, wr_lo], axis=1), wr_hi, br)


def _in_c_kernel(h_ref, m_ref, g_ref, win_ref, cos_ref, sin_ref, q_ref, k_ref, v_ref, *, q_scale):
    m = m_ref[...]
    z = _norm_mod(h_ref[...], g_ref[...], m[0:1], m[1:2])
    p = _dot(z.astype(BF16), win_ref[...])
    cos = cos_ref[...]
    sin = sin_ref[...]
    lane = lax.broadcasted_iota(jnp.int32, (1, LANES), 1)
    first = (lane % 32) < 16

    def rope(x):
        partner = jnp.where(first, pltpu.roll(x, LANES - 16, 1), pltpu.roll(x, 16, 1))
        return x * cos + partner * sin

    for tile in range(C_Q_W // LANES):
        cols = slice(tile * LANES, (tile + 1) * LANES)
        q_ref[:, cols] = (rope(p[:, cols]) * q_scale).astype(BF16)
    low_half = lane < C_HEAD_DIM
    for tile in range(C_KV_W // LANES):
        kk = rope(p[:, C_Q_W + tile * LANES:C_Q_W + (tile + 1) * LANES])
        vv = p[:, C_Q_W + C_KV_W + tile * LANES:C_Q_W + C_KV_W + (tile + 1) * LANES]
        ones = jnp.where(lane == C_HEAD_DIM, 1.0, 0.0)
        for half, (kh, vh) in enumerate(((kk, vv), (pltpu.roll(kk, C_HEAD_DIM, 1), pltpu.roll(vv, C_HEAD_DIM, 1)))):
            c0 = (2 * tile + half) * C_KV_PAD
            k_ref[:, c0:c0 + C_KV_PAD] = jnp.where(low_half, kh, 0.0).astype(BF16)
            v_ref[:, c0:c0 + C_KV_PAD] = jnp.where(low_half, vh, ones).astype(BF16)


def _in_c(rows, h, mod, g1, w_in, cos2, sin2):
    t, d = h.shape
    bm = rows.bm
    n_in = w_in.shape[1]
    const = lambda i: (0, 0)
    row = lambda i: (i, 0)
    return pl.pallas_call(
        functools.partial(_in_c_kernel, q_scale=float(C_HEAD_DIM ** -0.5 * LOG2E)),
        out_shape=(jax.ShapeDtypeStruct((t, C_Q_W), BF16),
                   jax.ShapeDtypeStruct((t, C_KV_HEADS * C_KV_PAD), BF16),
                   jax.ShapeDtypeStruct((t, C_KV_HEADS * C_KV_PAD), BF16)),
        grid=(rows.all_blocks,),
        in_specs=[pl.BlockSpec((bm, d), row),
                  pl.BlockSpec((None, N_MOD, d), lambda i: (rows.mod_index(i), 0, 0)),
                  pl.BlockSpec((1, d), const),
                  pl.BlockSpec((d, n_in), const),
                  pl.BlockSpec((bm, LANES), lambda i: (rows.pos_index(i), 0)),
                  pl.BlockSpec((bm, LANES), lambda i: (rows.pos_index(i), 0))],
        out_specs=(pl.BlockSpec((bm, C_Q_W), row),
                   pl.BlockSpec((bm, C_KV_HEADS * C_KV_PAD), row),
                   pl.BlockSpec((bm, C_KV_HEADS * C_KV_PAD), row)),
        compiler_params=_cparams(1),
        name="in_proj_c",
    )(h, mod, g1.reshape(1, d), w_in.astype(BF16), cos2, sin2)


def _sink_attend(q_ref, sink_ref, keys, vals, masks, o_ref, bq):
    def scores(kv):
        q8 = jnp.concatenate([q_ref[:, (kv * C_GROUP + g) * C_HEAD_DIM:(kv * C_GROUP + g + 1) * C_HEAD_DIM]
                              for g in range(C_GROUP)], axis=0)
        return [_dot_t(q8, k[:, kv * C_KV_PAD:kv * C_KV_PAD + C_HEAD_DIM]) for k in keys]

    s_next = scores(0)
    for kv in range(C_KV_HEADS):
        s = s_next
        if kv + 1 < C_KV_HEADS:
            s_next = scores(kv + 1)
        e, mxs = [], []
        for g in range(C_GROUP):
            sg = [si[g * bq:(g + 1) * bq] for si in s]
            sg = [si if msk is None else jnp.where(msk, si, NEG_INF) for si, msk in zip(sg, masks)]
            sink = sink_ref[kv * C_GROUP + g] * LOG2E
            tiles = [si[:, j * LANES:(j + 1) * LANES] for si in sg for j in range(si.shape[1] // LANES)]
            mx = jnp.maximum(sink, functools.reduce(jnp.maximum, tiles).max(axis=-1, keepdims=True))
            e.append([jnp.exp2(si - mx).astype(BF16) for si in sg])
            mxs.append((sink, mx))
        oe = None
        for piece, v in enumerate(vals):
            part = _dot(jnp.concatenate([eg[piece] for eg in e], axis=0), v[:, kv * C_KV_PAD:(kv + 1) * C_KV_PAD])
            oe = part if oe is None else oe + part
        for g, (sink, mx) in enumerate(mxs):
            og = oe[g * bq:(g + 1) * bq]
            den = og[:, C_HEAD_DIM:C_HEAD_DIM + 1] + jnp.exp2(sink - mx)
            c0 = (kv * C_GROUP + g) * C_HEAD_DIM
            o_ref[:, c0:c0 + C_HEAD_DIM] = (og[:, :C_HEAD_DIM] / den).astype(BF16)


def _attn_c_lat_kernel(sink_ref, q_ref, kp_ref, kc_ref, kn_ref, kx_ref, vp_ref, vc_ref, vn_ref, vx_ref, o_ref,
                       *, n_blk):
    n = pl.program_id(1)
    bq = C_WINDOW
    band = 3 * bq
    kband = jnp.concatenate([kp_ref[...], kc_ref[...], kn_ref[...]], axis=0)
    vband = jnp.concatenate([vp_ref[...], vc_ref[...], vn_ref[...]], axis=0)
    qi = lax.broadcasted_iota(jnp.int32, (bq, band), 0)
    kj = lax.broadcasted_iota(jnp.int32, (bq, band), 1)
    rel = qi - (kj - bq)
    ok = (jnp.abs(rel) <= C_WINDOW) & ((kj >= bq) | (n > 0)) & ((kj < 2 * bq) | (n < n_blk - 1))
    _sink_attend(q_ref, sink_ref, [kband, kx_ref[...]], [vband, vx_ref[...]], [ok, None], o_ref, bq)


def _attn_c_ctx_kernel(sink_ref, q_ref, kx_ref, vx_ref, o_ref):
    _sink_attend(q_ref, sink_ref, [kx_ref[...]], [vx_ref[...]], [None], o_ref, q_ref.shape[0])


def _attn_c(rows, q, k, v, sink, ctx_out):
    t = q.shape[0]
    bq = C_WINDOW
    n_blk = rows.n_lat // bq
    n_ctx = rows.n_ctx
    ctx_blk = rows.t_lat // n_ctx
    kvw = C_KV_HEADS * C_KV_PAD
    del t
    smem = pl.BlockSpec(memory_space=pltpu.SMEM)
    prev = lambda b, n: (b * n_blk + jnp.maximum(n - 1, 0), 0)
    cur = lambda b, n: (b * n_blk + n, 0)
    nxt = lambda b, n: (b * n_blk + jnp.minimum(n + 1, n_blk - 1), 0)
    cx = lambda b, n: (ctx_blk + b, 0)
    kv_specs = [pl.BlockSpec((bq, kvw), prev), pl.BlockSpec((bq, kvw), cur),
                pl.BlockSpec((bq, kvw), nxt), pl.BlockSpec((n_ctx, kvw), cx)]
    lat = pl.pallas_call(
        functools.partial(_attn_c_lat_kernel, n_blk=n_blk),
        out_shape=jax.ShapeDtypeStruct((rows.t_lat, C_Q_W), BF16),
        grid=(rows.n_b, n_blk),
        in_specs=[smem, pl.BlockSpec((bq, C_Q_W), cur)] + kv_specs + kv_specs,
        out_specs=pl.BlockSpec((bq, C_Q_W), cur),
        compiler_params=_cparams(2),
        name="attn_c_lat",
    )(sink, q, k, k, k, k, v, v, v, v)
    if not ctx_out:
        return lat, None
    cxb = lambda b: (ctx_blk + b, 0)
    ctx = pl.pallas_call(
        _attn_c_ctx_kernel,
        out_shape=jax.ShapeDtypeStruct((rows.n_b * n_ctx, C_Q_W), BF16),
        grid=(rows.n_b,),
        in_specs=[smem, pl.BlockSpec((n_ctx, C_Q_W), cxb), pl.BlockSpec((n_ctx, kvw), cxb),
                  pl.BlockSpec((n_ctx, kvw), cxb)],
        out_specs=pl.BlockSpec((n_ctx, C_Q_W), lambda b: (b, 0)),
        compiler_params=_cparams(1),
        name="attn_c_ctx",
    )(sink, q, k, v)
    return lat, ctx


def _moe_kernel(be_ref, ne_ref, nu_ref, x_ref, wg_hbm, wu_hbm, wd_hbm, y_ref,
                wg_f, wu_f, wd_f, wgu_s, wd_s, sem, *, layer):
    i = pl.program_id(0)
    used = i < nu_ref[0]
    expert = be_ref[i]
    fresh = (i == 0) | (expert != be_ref[jnp.maximum(i - 1, 0)])

    def weight_copies(e):
        return (pltpu.make_async_copy(wg_hbm.at[layer, e], wg_f, sem.at[0]),
                pltpu.make_async_copy(wu_hbm.at[layer, e], wu_f, sem.at[1]),
                pltpu.make_async_copy(wd_hbm.at[layer, e], wd_f, sem.at[2]))

    @pl.when(used & (i == 0))
    def _():
        for c in weight_copies(expert):
            c.start()

    @pl.when(used & fresh)
    def _():
        for c in weight_copies(expert):
            c.wait()
        wgu_s[:, :D_EXPERT] = wg_f[...].astype(BF16)
        wgu_s[:, D_EXPERT:] = wu_f[...].astype(BF16)
        wd_s[...] = wd_f[...].astype(BF16)
        nxt = ne_ref[i]

        @pl.when(nxt >= 0)
        def _():
            for c in weight_copies(nxt):
                c.start()

    @pl.when(used)
    def _():
        gu = _dot(x_ref[...], wgu_s[...])
        g = gu[:, :D_EXPERT]
        a = (g * jax.nn.sigmoid(g)) * gu[:, D_EXPERT:]
        y_ref[...] = _dot(a.astype(BF16), wd_s[...]).astype(BF16)

    @pl.when(jnp.logical_not(used))
    def _():
        y_ref[...] = jnp.zeros_like(y_ref)


def _moe_blocks(layer, xs, block_e, next_e, n_used, w_gate, w_up, w_down):
    r, d = xs.shape
    bm = MOE_ROWS
    nb = r // bm
    hbm = pl.BlockSpec(memory_space=pl.ANY)
    return pl.pallas_call(
        functools.partial(_moe_kernel, layer=layer),
        out_shape=jax.ShapeDtypeStruct((r, d), BF16),
        grid_spec=pltpu.PrefetchScalarGridSpec(
            num_scalar_prefetch=3,
            grid=(nb,),
            in_specs=[pl.BlockSpec((bm, d), lambda i, be, ne, nu: (i, 0)), hbm, hbm, hbm],
            out_specs=pl.BlockSpec((bm, d), lambda i, be, ne, nu: (i, 0)),
            scratch_shapes=[pltpu.VMEM((d, D_EXPERT), F32), pltpu.VMEM((d, D_EXPERT), F32),
                            pltpu.VMEM((D_EXPERT, d), F32),
                            pltpu.VMEM((d, 2 * D_EXPERT), BF16), pltpu.VMEM((D_EXPERT, d), BF16),
                            pltpu.SemaphoreType.DMA((3,))]),
        compiler_params=_cparams(1),
        name="moe_experts",
    )(block_e, next_e, n_used, xs, w_gate, w_up, w_down)


def _plan(ri, cnt, bm):
    n_t = ri.shape[0]
    n_tk = n_t * TOP_K
    nb = -(-(n_tk + N_EXPERTS * (bm - 1)) // bm)
    counts = cnt[0, N_GROUPS:N_GROUPS + N_EXPERTS].astype(jnp.int32)
    padded = (counts + bm - 1) // bm * bm
    pends = jnp.cumsum(padded)
    dest = (pends - padded)[ri[:, :TOP_K]] + ri[:, TOP_K:2 * TOP_K]
    row_tok = (jnp.arange(nb * bm, dtype=jnp.int32) % n_t).at[dest.reshape(-1)].set(
        jnp.arange(n_tk, dtype=jnp.int32) // TOP_K, unique_indices=True, mode='promise_in_bounds')
    n_used = (pends[-1] // bm).astype(jnp.int32)
    blk = jnp.arange(nb, dtype=jnp.int32)
    block_e = jnp.sum((blk[:, None] * bm >= pends[None, :]).astype(jnp.int32), axis=1)
    block_e = jnp.minimum(block_e, N_EXPERTS - 1)
    block_e = jnp.where(blk < n_used, block_e, block_e[n_used - 1])
    later = (block_e[None, :] > block_e[:, None]) & (blk[None, :] < n_used)
    next_e = jnp.min(jnp.where(later, block_e[None, :], N_EXPERTS), axis=1)
    next_e = jnp.where(next_e < N_EXPERTS, next_e, -1).astype(jnp.int32)
    return row_tok, block_e, next_e, n_used.reshape(1), dest


def _combine_kernel(h_ref, y0_ref, y1_ref, rw_ref, m_ref, g_ref, o_ref, *, final):
    rw = rw_ref[...]
    y = rw[:, 0:1] * y0_ref[...].astype(F32) + rw[:, 1:2] * y1_ref[...].astype(F32)
    hn = h_ref[...] + m_ref[...][5:6] * y
    if final:
        hn = _rms(hn) * g_ref[...]
    o_ref[...] = hn


def _combine(rows, n_blocks, h, y0, y1, rw, mod, final_g, final):
    d = h.shape[1]
    bm = rows.bm
    row = lambda i: (i, 0)
    return pl.pallas_call(
        functools.partial(_combine_kernel, final=final),
        out_shape=jax.ShapeDtypeStruct((n_blocks * bm, d), F32),
        grid=(n_blocks,),
        in_specs=[pl.BlockSpec((bm, d), row), pl.BlockSpec((bm, d), row), pl.BlockSpec((bm, d), row),
                  pl.BlockSpec((bm, ROUTER_W), row),
                  pl.BlockSpec((None, N_MOD, d), lambda i: (rows.mod_index(i), 0, 0)),
                  pl.BlockSpec((1, d), lambda i: (0, 0))],
        out_specs=pl.BlockSpec((bm, d), row),
        compiler_params=_cparams(1),
        name="combine",
    )(h, y0, y1, rw, mod, final_g.reshape(1, d))


def kernel(x, c, ctx, c_ctx, mod_w, mod_b, norm1_g, norm2_g, final_g, a_w_in, a_q_norm_g, a_kv_norm_g, a_w_uq,
           a_w_ukv, a_w_pool, a_pool_scale, a_w_out, c_w_in, c_sink, c_w_out, r_w_group, r_b_group, r_w_expert,
           r_b_expert, e_w_gate, e_w_up, e_w_down):
    n_b, n_lat, d = x.shape
    n_ctx = ctx.shape[1]
    rows = _Rows(n_b, n_lat, n_ctx, ROW_BLOCK)
    pool_rows = _Rows(n_b, n_lat, n_ctx, POOL_ROWS)
    depth = mod_w.shape[0]

    cvec = jnp.zeros((8, d), F32).at[:n_b].set(c).at[n_b].set(c_ctx)
    mods = _modulation(cvec, mod_w, mod_b).reshape(depth, 8, N_MOD, d)

    cos, sin = _rope_tables(n_lat, ROW_BLOCK)
    zeros = jnp.zeros_like(cos)
    cos_a, sin_a = jnp.concatenate([cos, zeros], axis=1), jnp.concatenate([sin, zeros], axis=1)
    cos_c, sin_c = jnp.concatenate([cos, cos], axis=1), jnp.concatenate([sin, sin], axis=1)

    h = jnp.concatenate([x.reshape(-1, d), ctx.reshape(-1, d)], axis=0)
    for i in range(depth):
        ctx_out = i < depth - 1
        j = i // 2
        mod = mods[i]
        if i % 2 == 0:
            q, k, v, pp = _in_a(rows, h, mod, norm1_g[i], a_w_in[j], a_q_norm_g[j], a_w_uq[j], a_kv_norm_g[j],
                                a_w_ukv[j], cos_a, sin_a)
            a_lat, a_ctx = _attn_a(rows, q, k, v, ctx_out)
            halves = [(a_lat, a_ctx, 0), (_pool(pool_rows, pp, a_w_pool[j], a_pool_scale[j]), None, 0)]
            w_out = a_w_out[j]
        else:
            q, k, v = _in_c(rows, h, mod, norm1_g[i], c_w_in[j], cos_c, sin_c)
            a_lat, a_ctx = _attn_c(rows, q, k, v, c_sink[j], ctx_out)
            halves = [(a_lat, a_ctx, 0), (a_lat, a_ctx, 1)]
            w_out = c_w_out[j]
        n_blocks = rows.all_blocks if ctx_out else rows.lat_blocks
        wr = jnp.zeros((d, ROUTER_W), F32).at[:, :N_GROUPS].set(r_w_group[i])
        wr = wr.at[:, N_GROUPS:N_GROUPS + N_EXPERTS].set(r_w_expert[i])
        br = jnp.zeros((1, ROUTER_W), F32).at[0, :N_GROUPS].set(r_b_group[i])
        br = br.at[0, N_GROUPS:N_GROUPS + N_EXPERTS].set(r_b_expert[i])
        hn, z, ri, rw, cnt = _out_proj(rows, n_blocks, halves, w_out, h, mod, norm2_g[i], wr, br)

        row_tok, block_e, next_e, n_used, dest = _plan(ri, cnt, MOE_ROWS)
        xs = z.at[row_tok].get(mode='promise_in_bounds')
        y = _moe_blocks(i, xs, block_e, next_e, n_used, e_w_gate, e_w_up, e_w_down)
        y0 = y.at[dest[:, 0]].get(mode='promise_in_bounds')
        y1 = y.at[dest[:, 1]].get(mode='promise_in_bounds')
        h = _combine(rows, n_blocks, hn, y0, y1, rw, mod, final_g, final=not ctx_out)
    return h.reshape(n_b, n_lat, d)
```

```python
import functools

import numpy as np
import jax
import jax.numpy as jnp
from jax import lax
from jax.experimental import pallas as pl
from jax.experimental.pallas import tpu as pltpu

F32 = jnp.float32
BF16 = jnp.bfloat16

D_MODEL = 2048
DEPTH = 4
GRID_W = 64
EPS = 1e-6
ROPE_BASE = 10000.0
NEG_INF = -1e30
N_MOD = 6

A_NOPE = 128
A_ROPE = 64
A_V = 128
A_HEADS = 8
A_Q_RANK = 512
A_KV_RANK = 256
A_QK_PAD = 256
B_WINDOWS = (2, 4, 8, 16)
B_GROUP_W = 256
B_WIDTH = 1024
POOL_HALO = 8

C_HEAD_DIM = 64
C_HEADS = 32
C_KV_HEADS = 4
C_GROUP = 8
C_WINDOW = 128
C_Q_W = C_HEADS * C_HEAD_DIM
C_KV_W = C_KV_HEADS * C_HEAD_DIM

N_GROUPS = 4
EXPERTS_PER_GROUP = 8
N_EXPERTS = 32
TOP_K = 2
D_EXPERT = 512
ROUTER_W = 128

ROW_BLOCK = 512
IN_ROWS = 256
POOL_ROWS = 256
MOE_ROWS = 512
LANES = 128
VMEM_LIMIT = 56 * 1024 * 1024
LOG2E = 1.4426950408889634
ATTN_A_ROWS = 256
ATTN_A_KEYS = 512
C_KV_PAD = 128


def _cparams(n_axes):
    return pltpu.CompilerParams(dimension_semantics=("arbitrary",) * n_axes,
                                vmem_limit_bytes=VMEM_LIMIT)


def _dot(a, b):
    return jnp.dot(a, b, preferred_element_type=F32)


def _dot_t(a, b):
    return lax.dot_general(a, b, (((1,), (1,)), ((), ())), preferred_element_type=F32)


def _rms(x):
    return x * lax.rsqrt(jnp.mean(x * x, axis=-1, keepdims=True) + EPS)


def _norm_mod(h, g, shift, scale):
    return (_rms(h) * g) * (1 + scale) + shift


def _mod_kernel(s_ref, w_ref, b_ref, o_ref):
    s = s_ref[...]
    s = s * jax.nn.sigmoid(s)
    o_ref[...] = _dot(s.astype(BF16), w_ref[...].astype(BF16)) + b_ref[...]


def _modulation(cvec, mod_w, mod_b):
    depth, d, n = mod_w.shape
    tn = 1024
    return pl.pallas_call(
        _mod_kernel,
        out_shape=jax.ShapeDtypeStruct((depth, 8, n), F32),
        grid=(depth, n // tn),
        in_specs=[pl.BlockSpec((8, d), lambda l, j: (0, 0)),
                  pl.BlockSpec((None, d, tn), lambda l, j: (l, 0, j)),
                  pl.BlockSpec((None, 1, tn), lambda l, j: (l, 0, j))],
        out_specs=pl.BlockSpec((None, 8, tn), lambda l, j: (l, 0, j)),
        compiler_params=_cparams(2),
        name="modulation",
    )(cvec, mod_w, mod_b.reshape(depth, 1, n))


class _Rows:
    def __init__(self, n_b, n_lat, n_ctx, bm):
        self.n_b, self.n_lat, self.n_ctx, self.bm = n_b, n_lat, n_ctx, bm
        self.t_lat = n_b * n_lat
        self.t_all = self.t_lat + n_b * n_ctx
        assert n_lat % bm == 0 and (n_b * n_ctx) % bm == 0
        self.lat_blocks = self.t_lat // bm
        self.all_blocks = self.t_all // bm
        self.blocks_per_seq = n_lat // bm

    def mod_index(self, i):
        return jnp.minimum(i // self.blocks_per_seq, self.n_b)

    def pos_index(self, i):
        return jnp.where(i < self.lat_blocks, i % self.blocks_per_seq, self.blocks_per_seq)


def _rope_tables(n_lat, n_ctx):
    axis_dim = A_ROPE // 2
    inv_freq = ROPE_BASE ** (-jnp.arange(axis_dim // 2, dtype=F32) * 2.0 / axis_dim)
    rows = n_lat // GRID_W
    row = jnp.repeat(jnp.arange(rows, dtype=F32), GRID_W)
    col = jnp.tile(jnp.arange(GRID_W, dtype=F32), rows)
    ang_r = row[:, None] * inv_freq
    ang_c = col[:, None] * inv_freq
    cr, sr, cc, sc = jnp.cos(ang_r), jnp.sin(ang_r), jnp.cos(ang_c), jnp.sin(ang_c)
    cos = jnp.concatenate([cr, cr, cc, cc], axis=-1)
    sin = jnp.concatenate([-sr, sr, -sc, sc], axis=-1)
    cos = jnp.concatenate([cos, jnp.ones((n_ctx, 64), F32)], axis=0)
    sin = jnp.concatenate([sin, jnp.zeros((n_ctx, 64), F32)], axis=0)
    return cos, sin


_ROPE_SWAP = np.concatenate([np.arange(16, 32), np.arange(0, 16), np.arange(48, 64), np.arange(32, 48)])


def _moe_residual(hn_ref, y0_ref, y1_ref, rw_ref, m_ref):
    rw = rw_ref[...]
    y = rw[:, 0:1] * y0_ref[...].astype(F32) + rw[:, 1:2] * y1_ref[...].astype(F32)
    return hn_ref[...] + m_ref[...][5:6] * y


def _in_a_kernel(*refs, q_scale, pending):
    n_src = 5 if pending else 1
    (m_ref, g_ref, win_ref, gq_ref, wuq_ref, gkv_ref, wk_ref, wv_ref, cos_ref, sin_ref) = refs[n_src:n_src + 10]
    q_ref, k_ref, v_ref, pp_ref = refs[n_src + 10:n_src + 14]
    if pending:
        h = _moe_residual(*refs[:5])
        refs[n_src + 14][...] = h
    else:
        h = refs[0][...]
    m = m_ref[...]
    z = _norm_mod(h, g_ref[...], m[0:1], m[1:2])
    p = _dot(z.astype(BF16), win_ref[...])
    cos = cos_ref[...]
    sin = sin_ref[...]

    cqn = _rms(p[:, :A_Q_RANK]) * gq_ref[...]
    qraw = _dot(cqn.astype(BF16), wuq_ref[...])
    for hd in range(A_HEADS):
        c0 = hd * A_QK_PAD
        t = qraw[:, c0 + A_NOPE:c0 + A_QK_PAD]
        rot = t * cos + pltpu.roll(t, 64, 1) * sin
        q_ref[:, c0:c0 + A_NOPE] = (qraw[:, c0:c0 + A_NOPE] * q_scale).astype(BF16)
        q_ref[:, c0 + A_NOPE:c0 + A_QK_PAD] = (rot * q_scale).astype(BF16)

    ckvn = (_rms(p[:, A_Q_RANK:A_Q_RANK + A_KV_RANK]) * gkv_ref[...]).astype(BF16)
    kn = _dot(ckvn, wk_ref[...])
    kt = p[:, 768:896]
    krot = (kt * cos + pltpu.roll(kt, 64, 1) * sin).astype(BF16)
    for hd in range(A_HEADS):
        c0 = hd * A_QK_PAD
        k_ref[:, c0:c0 + A_NOPE] = kn[:, hd * A_NOPE:(hd + 1) * A_NOPE].astype(BF16)
        k_ref[:, c0 + A_NOPE:c0 + A_QK_PAD] = krot
    v_ref[...] = _dot(ckvn, wv_ref[...]).astype(BF16)
    pp_ref[...] = p[:, 896:]


def _stream_specs(rows, src):
    bm = rows.bm
    row = lambda i: (i, 0)
    if not isinstance(src, tuple):
        return [pl.BlockSpec((bm, src.shape[1]), row)], [src], [], []
    hn, y0, y1, rw, mod_prev = src
    d = hn.shape[1]
    specs = [pl.BlockSpec((bm, d), row), pl.BlockSpec((bm, d), row), pl.BlockSpec((bm, d), row),
             pl.BlockSpec((bm, ROUTER_W), row),
             pl.BlockSpec((None, N_MOD, d), lambda i: (rows.mod_index(i), 0, 0))]
    return specs, list(src), [jax.ShapeDtypeStruct(hn.shape, F32)], [pl.BlockSpec((bm, d), row)]


def _in_a(rows, src, mod, g1, w_in, gq, w_uq, gkv, w_ukv, cos2, sin2):
    s_specs, s_args, h_shape, h_spec = _stream_specs(rows, src)
    t, d = s_args[0].shape
    off_rope = A_Q_RANK + A_KV_RANK
    win = jnp.concatenate([w_in[:, :off_rope + A_ROPE], w_in[:, off_rope + _ROPE_SWAP],
                           w_in[:, off_rope + A_ROPE:]], axis=1).astype(BF16)
    wq = w_uq.reshape(A_Q_RANK, A_HEADS, A_NOPE + A_ROPE)
    wuq = jnp.concatenate([wq, wq[:, :, A_NOPE + _ROPE_SWAP]], axis=-1).reshape(A_Q_RANK, A_HEADS * A_QK_PAD)
    wkv = w_ukv.reshape(A_KV_RANK, A_HEADS, A_NOPE + A_V)
    wk = wkv[:, :, :A_NOPE].reshape(A_KV_RANK, A_HEADS * A_NOPE).astype(BF16)
    wv = wkv[:, :, A_NOPE:].reshape(A_KV_RANK, A_HEADS * A_V).astype(BF16)
    n_in = win.shape[1]
    bm = rows.bm
    const = lambda i: (0, 0)
    row = lambda i: (i, 0)
    return pl.pallas_call(
        functools.partial(_in_a_kernel, q_scale=float((A_NOPE + A_ROPE) ** -0.5 * LOG2E),
                          pending=isinstance(src, tuple)),
        out_shape=[jax.ShapeDtypeStruct((t, A_HEADS * A_QK_PAD), BF16),
                   jax.ShapeDtypeStruct((t, A_HEADS * A_QK_PAD), BF16),
                   jax.ShapeDtypeStruct((t, A_HEADS * A_V), BF16),
                   jax.ShapeDtypeStruct((t, B_WIDTH), F32)] + h_shape,
        grid=(rows.all_blocks,),
        in_specs=s_specs + [
                  pl.BlockSpec((None, N_MOD, d), lambda i: (rows.mod_index(i), 0, 0)),
                  pl.BlockSpec((1, d), const),
                  pl.BlockSpec((d, n_in), const),
                  pl.BlockSpec((1, A_Q_RANK), const),
                  pl.BlockSpec((A_Q_RANK, A_HEADS * A_QK_PAD), const),
                  pl.BlockSpec((1, A_KV_RANK), const),
                  pl.BlockSpec((A_KV_RANK, A_HEADS * A_NOPE), const),
                  pl.BlockSpec((A_KV_RANK, A_HEADS * A_V), const),
                  pl.BlockSpec((bm, LANES), lambda i: (rows.pos_index(i), 0)),
                  pl.BlockSpec((bm, LANES), lambda i: (rows.pos_index(i), 0))],
        out_specs=[pl.BlockSpec((bm, A_HEADS * A_QK_PAD), row),
                   pl.BlockSpec((bm, A_HEADS * A_QK_PAD), row),
                   pl.BlockSpec((bm, A_HEADS * A_V), row),
                   pl.BlockSpec((bm, B_WIDTH), row)] + h_spec,
        compiler_params=_cparams(1),
        name="in_proj_a",
    )(*s_args, mod, g1.reshape(1, d), win, gq.reshape(1, -1), wuq.astype(BF16), gkv.reshape(1, -1), wk, wv, cos2, sin2)


def _attn_a_kernel(*refs, n_kv):
    q_ref = refs[0]
    k_refs = refs[1:1 + n_kv]
    v_refs = refs[1 + n_kv:1 + 2 * n_kv]
    o_ref = refs[1 + 2 * n_kv]
    chunks = []
    for k, v in zip(k_refs, v_refs):
        for r0 in range(0, k.shape[0], ATTN_A_KEYS):
            chunks.append((k, v, r0, min(ATTN_A_KEYS, k.shape[0] - r0)))
    bq = min(ATTN_A_ROWS, q_ref.shape[0])

    def scores(rows):
        q = q_ref[rows, :]
        s = []
        top = None
        for k, _, r0, n in chunks:
            si = _dot_t(q, k[r0:r0 + n, :])
            s.append(si)
            for j in range(n // LANES):
                tile = si[:, j * LANES:(j + 1) * LANES]
                top = tile if top is None else jnp.maximum(top, tile)
        return s, top.max(axis=-1, keepdims=True)

    def attend(rows, s, mx):
        o = None
        den = None
        for si, (_, v, r0, n) in zip(s, chunks):
            e = jnp.exp2(si - mx)
            for j in range(n // LANES):
                tile = e[:, j * LANES:(j + 1) * LANES]
                den = tile if den is None else den + tile
            part = _dot(e.astype(BF16), v[r0:r0 + n, :])
            o = part if o is None else o + part
        o_ref[rows, :] = (o / den.sum(axis=-1, keepdims=True)).astype(BF16)

    pair = 2 if q_ref.shape[0] % (2 * bq) == 0 else 1

    def block(i, carry):
        rows = [pl.ds(pl.multiple_of((i * pair + u) * bq, bq), bq) for u in range(pair)]
        staged = [scores(r) for r in rows]
        for r, (s, mx) in zip(rows, staged):
            attend(r, s, mx)
        return carry

    lax.fori_loop(0, q_ref.shape[0] // (pair * bq), block, 0)


def _attn_a(rows, q, k, v, ctx_out):
    n_lat, n_ctx = rows.n_lat, rows.n_ctx
    ctx_blk = rows.t_lat // n_ctx
    lat = pl.pallas_call(
        functools.partial(_attn_a_kernel, n_kv=2),
        out_shape=jax.ShapeDtypeStruct((rows.t_lat, A_HEADS * A_V), BF16),
        grid=(rows.n_b, A_HEADS),
        in_specs=[pl.BlockSpec((n_lat, A_QK_PAD), lambda b, h: (b, h)),
                  pl.BlockSpec((n_lat, A_QK_PAD), lambda b, h: (b, h)),
                  pl.BlockSpec((n_ctx, A_QK_PAD), lambda b, h: (ctx_blk + b, h)),
                  pl.BlockSpec((n_lat, A_V), lambda b, h: (b, h)),
                  pl.BlockSpec((n_ctx, A_V), lambda b, h: (ctx_blk + b, h))],
        out_specs=pl.BlockSpec((n_lat, A_V), lambda b, h: (b, h)),
        compiler_params=_cparams(2),
        name="attn_a_lat",
    )(q, k, k, v, v)
    if not ctx_out:
        return lat, None
    ctx = pl.pallas_call(
        functools.partial(_attn_a_kernel, n_kv=1),
        out_shape=jax.ShapeDtypeStruct((rows.n_b * n_ctx, A_HEADS * A_V), BF16),
        grid=(rows.n_b, A_HEADS),
        in_specs=[pl.BlockSpec((n_ctx, A_QK_PAD), lambda b, h: (ctx_blk + b, h)),
                  pl.BlockSpec((n_ctx, A_QK_PAD), lambda b, h: (ctx_blk + b, h)),
                  pl.BlockSpec((n_ctx, A_V), lambda b, h: (ctx_blk + b, h))],
        out_specs=pl.BlockSpec((n_ctx, A_V), lambda b, h: (b, h)),
        compiler_params=_cparams(2),
        name="attn_a_ctx",
    )(q, k, v)
    return lat, ctx


def _pool_kernel(prev_ref, cur_ref, next_ref, w_ref, s_ref, o_ref, ext_ref, *, rows):
    i = pl.program_id(0)
    bm = rows.bm
    is_ctx = i >= rows.lat_blocks
    n_l = jnp.where(is_ctx, rows.n_ctx, rows.n_lat)
    pos0 = jnp.where(is_ctx, 0, (i % rows.blocks_per_seq) * bm)
    ext_ref[0:POOL_HALO, :] = prev_ref[...]
    ext_ref[POOL_HALO:POOL_HALO + bm, :] = cur_ref[...]
    ext_ref[POOL_HALO + bm:, :] = next_ref[...]
    pos = pos0 + lax.broadcasted_iota(jnp.int32, (bm, 1), 0)
    for gi, w in enumerate(B_WINDOWS):
        cols = slice(gi * B_GROUP_W, (gi + 1) * B_GROUP_W)
        acc = jnp.zeros((bm, B_GROUP_W), F32)
        for dlt in range(-(w // 2), w // 2):
            x = ext_ref[POOL_HALO + dlt:POOL_HALO + dlt + bm, cols]
            ok = (pos + dlt >= 0) & (pos + dlt < n_l)
            acc = acc + jnp.where(ok, x, 0.0)
        cnt = jnp.minimum(pos + (w // 2 - 1), n_l - 1) - jnp.maximum(pos - w // 2, 0) + 1
        dev = acc / cnt.astype(F32) - cur_ref[:, cols]
        y = _dot(dev.astype(BF16), w_ref[gi]) * s_ref[:, cols]
        o_ref[:, cols] = y.astype(BF16)


def _pool(rows, pp, w_pool, pool_scale):
    t = pp.shape[0]
    bm = rows.bm
    assert rows.n_ctx == bm
    per = bm // POOL_HALO
    last = t // POOL_HALO - 1
    return pl.pallas_call(
        functools.partial(_pool_kernel, rows=rows),
        out_shape=jax.ShapeDtypeStruct((t, B_WIDTH), BF16),
        grid=(rows.all_blocks,),
        in_specs=[pl.BlockSpec((POOL_HALO, B_WIDTH), lambda i: (jnp.maximum(i * per - 1, 0), 0)),
                  pl.BlockSpec((bm, B_WIDTH), lambda i: (i, 0)),
                  pl.BlockSpec((POOL_HALO, B_WIDTH), lambda i: (jnp.minimum((i + 1) * per, last), 0)),
                  pl.BlockSpec((len(B_WINDOWS), B_GROUP_W, B_GROUP_W), lambda i: (0, 0, 0)),
                  pl.BlockSpec((1, B_WIDTH), lambda i: (0, 0))],
        out_specs=pl.BlockSpec((bm, B_WIDTH), lambda i: (i, 0)),
        scratch_shapes=[pltpu.VMEM((bm + 2 * POOL_HALO, B_WIDTH), F32)],
        compiler_params=_cparams(1),
        name="pool",
    )(pp, pp, pp, w_pool.astype(BF16), pool_scale.reshape(1, -1))


def _out_kernel(*refs, split, lat_blocks):
    i = pl.program_id(0)
    halves = []
    n = 0
    for is_pair in split:
        if is_pair:
            halves.append(jnp.where(i < lat_blocks, refs[n][...], refs[n + 1][...]))
        else:
            halves.append(refs[n][...])
        n += 2 if is_pair else 1
    (w1_ref, w2_ref, h_ref, m_ref, g_ref, wr1_ref, wr2_ref, br_ref,
     hn_ref, z_ref, ri_ref, rw_ref, cnt_ref, carry_ref) = refs[n:]
    bm = h_ref.shape[0]

    @pl.when(i == 0)
    def _():
        carry_ref[...] = jnp.zeros_like(carry_ref)

    m = m_ref[...]
    o = _dot(halves[0], w1_ref[...]) + _dot(halves[1], w2_ref[...])
    hn = h_ref[...] + m[2:3] * o
    hn_ref[...] = hn
    z = _norm_mod(hn, g_ref[...], m[3:4], m[4:5])
    z_hi = z.astype(BF16)
    z_ref[...] = z_hi
    z_lo = (z - z_hi.astype(F32)).astype(BF16)
    l2 = _dot(z_hi, wr1_ref[...])
    lg = l2[:, :ROUTER_W] + l2[:, ROUTER_W:] + _dot(z_lo, wr2_ref[...]) + br_ref[...]

    lane = lax.broadcasted_iota(jnp.int32, (bm, ROUTER_W), 1)
    low = jnp.float32(-3e38)
    is_g = lane < N_GROUPS
    glog = jnp.where(is_g, lg, low)
    gmax = glog.max(axis=-1, keepdims=True)
    g_idx = jnp.where(glog == gmax, lane, ROUTER_W).min(axis=-1, keepdims=True)
    g_gate = 1.0 / jnp.where(is_g, jnp.exp(lg - gmax), 0.0).sum(axis=-1, keepdims=True)
    lo = N_GROUPS + EXPERTS_PER_GROUP * g_idx
    el = jnp.where((lane >= lo) & (lane < lo + EXPERTS_PER_GROUP), lg, low)
    v1 = el.max(axis=-1, keepdims=True)
    i1 = jnp.where(el == v1, lane, ROUTER_W).min(axis=-1, keepdims=True)
    el2 = jnp.where(lane == i1, low, el)
    v2 = el2.max(axis=-1, keepdims=True)
    i2 = jnp.where(el2 == v2, lane, ROUTER_W).min(axis=-1, keepdims=True)
    e21 = jnp.exp(v2 - v1)
    w1 = g_gate * (1.0 / (1.0 + e21))
    w2 = g_gate * (e21 / (1.0 + e21))

    hit1 = lane == i1
    hit2 = lane == i2
    onehot = jnp.where(hit1 | hit2, 1.0, 0.0)
    r_i = lax.broadcasted_iota(jnp.int32, (bm, bm), 0)
    c_i = lax.broadcasted_iota(jnp.int32, (bm, bm), 1)
    before = _dot(jnp.where(r_i > c_i, 1.0, 0.0).astype(BF16), onehot.astype(BF16)) + carry_ref[0:1, :]
    rank1 = jnp.where(hit1, before, 0.0).sum(axis=-1, keepdims=True).astype(jnp.int32)
    rank2 = jnp.where(hit2, before, 0.0).sum(axis=-1, keepdims=True).astype(jnp.int32)
    total = carry_ref[0:1, :] + onehot.sum(axis=0, keepdims=True)
    carry_ref[...] = jnp.broadcast_to(total, carry_ref.shape)
    cnt_ref[...] = jnp.broadcast_to(total, cnt_ref.shape)

    ri_ref[...] = jnp.where(lane == 0, i1 - N_GROUPS, jnp.where(lane == 1, i2 - N_GROUPS,
                            jnp.where(lane == 2, rank1, jnp.where(lane == 3, rank2, 0))))
    rw_ref[...] = jnp.where(lane == 0, w1, jnp.where(lane == 1, w2, 0.0))


def _out_proj(rows, n_blocks, halves, w_out, h, mod, g2, wr, br):
    t, d = h.shape
    bm = rows.bm
    t_out = n_blocks * bm
    half = w_out.shape[0] // 2
    w = w_out.astype(BF16)
    wr_hi = wr.astype(BF16)
    wr_lo = (wr - wr_hi.astype(F32)).astype(BF16)
    const = lambda i: (0, 0)
    row = lambda i: (i, 0)
    lat_blocks = rows.lat_blocks
    a_specs, a_args = [], []
    for lat, ctx, col in halves:
        if ctx is None:
            a_specs.append(pl.BlockSpec((bm, half), lambda i, col=col: (i, col)))
            a_args.append(lat)
        else:
            a_specs.append(pl.BlockSpec((bm, half), lambda i, col=col: (jnp.minimum(i, lat_blocks - 1), col)))
            a_specs.append(pl.BlockSpec((bm, half), lambda i, col=col: (jnp.maximum(i - lat_blocks, 0), col)))
            a_args += [lat, ctx]
    return pl.pallas_call(
        functools.partial(_out_kernel, split=tuple(ctx is not None for _, ctx, _ in halves), lat_blocks=lat_blocks),
        out_shape=(jax.ShapeDtypeStruct((t_out, d), F32),
                   jax.ShapeDtypeStruct((t_out, d), BF16),
                   jax.ShapeDtypeStruct((t_out, ROUTER_W), jnp.int32),
                   jax.ShapeDtypeStruct((t_out, ROUTER_W), F32),
                   jax.ShapeDtypeStruct((8, ROUTER_W), F32)),
        grid=(n_blocks,),
        in_specs=a_specs + [
                  pl.BlockSpec((half, d), lambda i: (0, 0)),
                  pl.BlockSpec((half, d), lambda i: (1, 0)),
                  pl.BlockSpec((bm, d), row),
                  pl.BlockSpec((None, N_MOD, d), lambda i: (rows.mod_index(i), 0, 0)),
                  pl.BlockSpec((1, d), const),
                  pl.BlockSpec((d, 2 * ROUTER_W), const),
                  pl.BlockSpec((d, ROUTER_W), const),
                  pl.BlockSpec((1, ROUTER_W), const)],
        out_specs=(pl.BlockSpec((bm, d), row),
                   pl.BlockSpec((bm, d), row),
                   pl.BlockSpec((bm, ROUTER_W), row),
                   pl.BlockSpec((bm, ROUTER_W), row),
                   pl.BlockSpec((8, ROUTER_W), const)),
        scratch_shapes=[pltpu.VMEM((8, ROUTER_W), F32)],
        compiler_params=_cparams(1),
        name="out_proj",
    )(*a_args, w, w, h, mod, g2.reshape(1, d), jnp.concatenate([wr_hi, wr_lo], axis=1), wr_hi, br)


def _in_c_kernel(*refs, q_scale, pending):
    n_src = 5 if pending else 1
    m_ref, g_ref, win_ref, cos_ref, sin_ref = refs[n_src:n_src + 5]
    q_ref, k_ref, v_ref = refs[n_src + 5:n_src + 8]
    if pending:
        h = _moe_residual(*refs[:5])
        refs[n_src + 8][...] = h
    else:
        h = refs[0][...]
    m = m_ref[...]
    z = _norm_mod(h, g_ref[...], m[0:1], m[1:2])
    p = _dot(z.astype(BF16), win_ref[...])
    cos = cos_ref[...]
    sin = sin_ref[...]
    lane = lax.broadcasted_iota(jnp.int32, (1, LANES), 1)
    first = (lane % 32) < 16

    def rope(x):
        partner = jnp.where(first, pltpu.roll(x, LANES - 16, 1), pltpu.roll(x, 16, 1))
        return x * cos + partner * sin

    for tile in range(C_Q_W // LANES):
        cols = slice(tile * LANES, (tile + 1) * LANES)
        q_ref[:, cols] = (rope(p[:, cols]) * q_scale).astype(BF16)
    low_half = lane < C_HEAD_DIM
    for tile in range(C_KV_W // LANES):
        kk = rope(p[:, C_Q_W + tile * LANES:C_Q_W + (tile + 1) * LANES])
        vv = p[:, C_Q_W + C_KV_W + tile * LANES:C_Q_W + C_KV_W + (tile + 1) * LANES]
        ones = jnp.where(lane == C_HEAD_DIM, 1.0, 0.0)
        for half, (kh, vh) in enumerate(((kk, vv), (pltpu.roll(kk, C_HEAD_DIM, 1), pltpu.roll(vv, C_HEAD_DIM, 1)))):
            c0 = (2 * tile + half) * C_KV_PAD
            k_ref[:, c0:c0 + C_KV_PAD] = jnp.where(low_half, kh, 0.0).astype(BF16)
            v_ref[:, c0:c0 + C_KV_PAD] = jnp.where(low_half, vh, ones).astype(BF16)


def _in_c(rows, src, mod, g1, w_in, cos2, sin2):
    s_specs, s_args, h_shape, h_spec = _stream_specs(rows, src)
    t, d = s_args[0].shape
    bm = rows.bm
    n_in = w_in.shape[1]
    const = lambda i: (0, 0)
    row = lambda i: (i, 0)
    return pl.pallas_call(
        functools.partial(_in_c_kernel, q_scale=float(C_HEAD_DIM ** -0.5 * LOG2E), pending=isinstance(src, tuple)),
        out_shape=[jax.ShapeDtypeStruct((t, C_Q_W), BF16),
                   jax.ShapeDtypeStruct((t, C_KV_HEADS * C_KV_PAD), BF16),
                   jax.ShapeDtypeStruct((t, C_KV_HEADS * C_KV_PAD), BF16)] + h_shape,
        grid=(rows.all_blocks,),
        in_specs=s_specs + [
                  pl.BlockSpec((None, N_MOD, d), lambda i: (rows.mod_index(i), 0, 0)),
                  pl.BlockSpec((1, d), const),
                  pl.BlockSpec((d, n_in), const),
                  pl.BlockSpec((bm, LANES), lambda i: (rows.pos_index(i), 0)),
                  pl.BlockSpec((bm, LANES), lambda i: (rows.pos_index(i), 0))],
        out_specs=[pl.BlockSpec((bm, C_Q_W), row),
                   pl.BlockSpec((bm, C_KV_HEADS * C_KV_PAD), row),
                   pl.BlockSpec((bm, C_KV_HEADS * C_KV_PAD), row)] + h_spec,
        compiler_params=_cparams(1),
        name="in_proj_c",
    )(*s_args, mod, g1.reshape(1, d), w_in.astype(BF16), cos2, sin2)


def _sink_attend(q_ref, sink_ref, keys, vals, masks, o_ref, bq):
    def scores(kv):
        q8 = jnp.concatenate([q_ref[:, (kv * C_GROUP + g) * C_HEAD_DIM:(kv * C_GROUP + g + 1) * C_HEAD_DIM]
                              for g in range(C_GROUP)], axis=0)
        return [_dot_t(q8, k[:, kv * C_KV_PAD:kv * C_KV_PAD + C_HEAD_DIM]) for k in keys]

    s_next = scores(0)
    for kv in range(C_KV_HEADS):
        s = s_next
        if kv + 1 < C_KV_HEADS:
            s_next = scores(kv + 1)
        e, mxs = [], []
        for g in range(C_GROUP):
            sg = [si[g * bq:(g + 1) * bq] for si in s]
            sg = [si if msk is None else jnp.where(msk, si, NEG_INF) for si, msk in zip(sg, masks)]
            sink = sink_ref[kv * C_GROUP + g] * LOG2E
            tiles = [si[:, j * LANES:(j + 1) * LANES] for si in sg for j in range(si.shape[1] // LANES)]
            mx = jnp.maximum(sink, functools.reduce(jnp.maximum, tiles).max(axis=-1, keepdims=True))
            e.append([jnp.exp2(si - mx).astype(BF16) for si in sg])
            mxs.append((sink, mx))
        oe = None
        for piece, v in enumerate(vals):
            part = _dot(jnp.concatenate([eg[piece] for eg in e], axis=0), v[:, kv * C_KV_PAD:(kv + 1) * C_KV_PAD])
            oe = part if oe is None else oe + part
        for g, (sink, mx) in enumerate(mxs):
            og = oe[g * bq:(g + 1) * bq]
            den = og[:, C_HEAD_DIM:C_HEAD_DIM + 1] + jnp.exp2(sink - mx)
            c0 = (kv * C_GROUP + g) * C_HEAD_DIM
            o_ref[:, c0:c0 + C_HEAD_DIM] = (og[:, :C_HEAD_DIM] / den).astype(BF16)


def _attn_c_lat_kernel(sink_ref, q_ref, kp_ref, kc_ref, kn_ref, kx_ref, vp_ref, vc_ref, vn_ref, vx_ref, o_ref,
                       *, n_blk):
    n = pl.program_id(1)
    bq = C_WINDOW
    band = 3 * bq
    kband = jnp.concatenate([kp_ref[...], kc_ref[...], kn_ref[...]], axis=0)
    vband = jnp.concatenate([vp_ref[...], vc_ref[...], vn_ref[...]], axis=0)
    qi = lax.broadcasted_iota(jnp.int32, (bq, band), 0)
    kj = lax.broadcasted_iota(jnp.int32, (bq, band), 1)
    rel = qi - (kj - bq)
    ok = (jnp.abs(rel) <= C_WINDOW) & ((kj >= bq) | (n > 0)) & ((kj < 2 * bq) | (n < n_blk - 1))
    _sink_attend(q_ref, sink_ref, [kband, kx_ref[...]], [vband, vx_ref[...]], [ok, None], o_ref, bq)


def _attn_c_ctx_kernel(sink_ref, q_ref, kx_ref, vx_ref, o_ref):
    _sink_attend(q_ref, sink_ref, [kx_ref[...]], [vx_ref[...]], [None], o_ref, q_ref.shape[0])


def _attn_c(rows, q, k, v, sink, ctx_out):
    t = q.shape[0]
    bq = C_WINDOW
    n_blk = rows.n_lat // bq
    n_ctx = rows.n_ctx
    ctx_blk = rows.t_lat // n_ctx
    kvw = C_KV_HEADS * C_KV_PAD
    del t
    smem = pl.BlockSpec(memory_space=pltpu.SMEM)
    prev = lambda b, n: (b * n_blk + jnp.maximum(n - 1, 0), 0)
    cur = lambda b, n: (b * n_blk + n, 0)
    nxt = lambda b, n: (b * n_blk + jnp.minimum(n + 1, n_blk - 1), 0)
    cx = lambda b, n: (ctx_blk + b, 0)
    kv_specs = [pl.BlockSpec((bq, kvw), prev), pl.BlockSpec((bq, kvw), cur),
                pl.BlockSpec((bq, kvw), nxt), pl.BlockSpec((n_ctx, kvw), cx)]
    lat = pl.pallas_call(
        functools.partial(_attn_c_lat_kernel, n_blk=n_blk),
        out_shape=jax.ShapeDtypeStruct((rows.t_lat, C_Q_W), BF16),
        grid=(rows.n_b, n_blk),
        in_specs=[smem, pl.BlockSpec((bq, C_Q_W), cur)] + kv_specs + kv_specs,
        out_specs=pl.BlockSpec((bq, C_Q_W), cur),
        compiler_params=_cparams(2),
        name="attn_c_lat",
    )(sink, q, k, k, k, k, v, v, v, v)
    if not ctx_out:
        return lat, None
    cxb = lambda b: (ctx_blk + b, 0)
    ctx = pl.pallas_call(
        _attn_c_ctx_kernel,
        out_shape=jax.ShapeDtypeStruct((rows.n_b * n_ctx, C_Q_W), BF16),
        grid=(rows.n_b,),
        in_specs=[smem, pl.BlockSpec((n_ctx, C_Q_W), cxb), pl.BlockSpec((n_ctx, kvw), cxb),
                  pl.BlockSpec((n_ctx, kvw), cxb)],
        out_specs=pl.BlockSpec((n_ctx, C_Q_W), lambda b: (b, 0)),
        compiler_params=_cparams(1),
        name="attn_c_ctx",
    )(sink, q, k, v)
    return lat, ctx


def _moe_kernel(be_ref, ne_ref, nu_ref, x_ref, wg_hbm, wu_hbm, wd_hbm, y_ref,
                wg_f, wu_f, wd_f, wgu_s, wd_s, sem, *, layer):
    i = pl.program_id(0)
    used = i < nu_ref[0]
    expert = be_ref[i]
    fresh = (i == 0) | (expert != be_ref[jnp.maximum(i - 1, 0)])

    def weight_copies(e):
        return (pltpu.make_async_copy(wg_hbm.at[layer, e], wg_f, sem.at[0]),
                pltpu.make_async_copy(wu_hbm.at[layer, e], wu_f, sem.at[1]),
                pltpu.make_async_copy(wd_hbm.at[layer, e], wd_f, sem.at[2]))

    @pl.when(used & (i == 0))
    def _():
        for c in weight_copies(expert):
            c.start()

    @pl.when(used & fresh)
    def _():
        for c in weight_copies(expert):
            c.wait()
        wgu_s[:, :D_EXPERT] = wg_f[...].astype(BF16)
        wgu_s[:, D_EXPERT:] = wu_f[...].astype(BF16)
        wd_s[...] = wd_f[...].astype(BF16)
        nxt = ne_ref[i]

        @pl.when(nxt >= 0)
        def _():
            for c in weight_copies(nxt):
                c.start()

    @pl.when(used)
    def _():
        gu = _dot(x_ref[...], wgu_s[...])
        g = gu[:, :D_EXPERT]
        a = (g * jax.nn.sigmoid(g)) * gu[:, D_EXPERT:]
        y_ref[...] = _dot(a.astype(BF16), wd_s[...]).astype(BF16)

    @pl.when(jnp.logical_not(used))
    def _():
        y_ref[...] = jnp.zeros_like(y_ref)


def _moe_blocks(layer, xs, block_e, next_e, n_used, w_gate, w_up, w_down):
    r, d = xs.shape
    bm = MOE_ROWS
    nb = r // bm
    hbm = pl.BlockSpec(memory_space=pl.ANY)
    return pl.pallas_call(
        functools.partial(_moe_kernel, layer=layer),
        out_shape=jax.ShapeDtypeStruct((r, d), BF16),
        grid_spec=pltpu.PrefetchScalarGridSpec(
            num_scalar_prefetch=3,
            grid=(nb,),
            in_specs=[pl.BlockSpec((bm, d), lambda i, be, ne, nu: (i, 0)), hbm, hbm, hbm],
            out_specs=pl.BlockSpec((bm, d), lambda i, be, ne, nu: (i, 0)),
            scratch_shapes=[pltpu.VMEM((d, D_EXPERT), F32), pltpu.VMEM((d, D_EXPERT), F32),
                            pltpu.VMEM((D_EXPERT, d), F32),
                            pltpu.VMEM((d, 2 * D_EXPERT), BF16), pltpu.VMEM((D_EXPERT, d), BF16),
                            pltpu.SemaphoreType.DMA((3,))]),
        compiler_params=_cparams(1),
        name="moe_experts",
    )(block_e, next_e, n_used, xs, w_gate, w_up, w_down)


def _plan(ri, cnt, bm):
    n_t = ri.shape[0]
    n_tk = n_t * TOP_K
    nb = -(-(n_tk + N_EXPERTS * (bm - 1)) // bm)
    counts = cnt[0, N_GROUPS:N_GROUPS + N_EXPERTS].astype(jnp.int32)
    padded = (counts + bm - 1) // bm * bm
    pends = jnp.cumsum(padded)
    dest = (pends - padded)[ri[:, :TOP_K]] + ri[:, TOP_K:2 * TOP_K]
    row_tok = (jnp.arange(nb * bm, dtype=jnp.int32) % n_t).at[dest.reshape(-1)].set(
        jnp.arange(n_tk, dtype=jnp.int32) // TOP_K, unique_indices=True, mode='promise_in_bounds')
    n_used = (pends[-1] // bm).astype(jnp.int32)
    blk = jnp.arange(nb, dtype=jnp.int32)
    block_e = jnp.sum((blk[:, None] * bm >= pends[None, :]).astype(jnp.int32), axis=1)
    block_e = jnp.minimum(block_e, N_EXPERTS - 1)
    block_e = jnp.where(blk < n_used, block_e, block_e[n_used - 1])
    later = (block_e[None, :] > block_e[:, None]) & (blk[None, :] < n_used)
    next_e = jnp.min(jnp.where(later, block_e[None, :], N_EXPERTS), axis=1)
    next_e = jnp.where(next_e < N_EXPERTS, next_e, -1).astype(jnp.int32)
    return row_tok, block_e, next_e, n_used.reshape(1), dest


def _final_kernel(h_ref, y0_ref, y1_ref, rw_ref, m_ref, g_ref, o_ref):
    o_ref[...] = _rms(_moe_residual(h_ref, y0_ref, y1_ref, rw_ref, m_ref)) * g_ref[...]


def _final(rows, n_blocks, h, y0, y1, rw, mod, final_g):
    d = h.shape[1]
    bm = rows.bm
    row = lambda i: (i, 0)
    return pl.pallas_call(
        _final_kernel,
        out_shape=jax.ShapeDtypeStruct((n_blocks * bm, d), F32),
        grid=(n_blocks,),
        in_specs=[pl.BlockSpec((bm, d), row), pl.BlockSpec((bm, d), row), pl.BlockSpec((bm, d), row),
                  pl.BlockSpec((bm, ROUTER_W), row),
                  pl.BlockSpec((None, N_MOD, d), lambda i: (rows.mod_index(i), 0, 0)),
                  pl.BlockSpec((1, d), lambda i: (0, 0))],
        out_specs=pl.BlockSpec((bm, d), row),
        compiler_params=_cparams(1),
        name="final",
    )(h, y0, y1, rw, mod, final_g.reshape(1, d))


def kernel(x, c, ctx, c_ctx, mod_w, mod_b, norm1_g, norm2_g, final_g, a_w_in, a_q_norm_g, a_kv_norm_g, a_w_uq,
           a_w_ukv, a_w_pool, a_pool_scale, a_w_out, c_w_in, c_sink, c_w_out, r_w_group, r_b_group, r_w_expert,
           r_b_expert, e_w_gate, e_w_up, e_w_down):
    n_b, n_lat, d = x.shape
    n_ctx = ctx.shape[1]
    rows = _Rows(n_b, n_lat, n_ctx, ROW_BLOCK)
    in_rows = _Rows(n_b, n_lat, n_ctx, IN_ROWS)
    pool_rows = _Rows(n_b, n_lat, n_ctx, POOL_ROWS)
    depth = mod_w.shape[0]

    cvec = jnp.zeros((8, d), F32).at[:n_b].set(c).at[n_b].set(c_ctx)
    mods = _modulation(cvec, mod_w, mod_b).reshape(depth, 8, N_MOD, d)

    cos, sin = _rope_tables(n_lat, ROW_BLOCK)
    zeros = jnp.zeros_like(cos)
    cos_a, sin_a = jnp.concatenate([cos, zeros], axis=1), jnp.concatenate([sin, zeros], axis=1)
    cos_c, sin_c = jnp.concatenate([cos, cos], axis=1), jnp.concatenate([sin, sin], axis=1)

    src = jnp.concatenate([x.reshape(-1, d), ctx.reshape(-1, d)], axis=0)
    for i in range(depth):
        ctx_out = i < depth - 1
        j = i // 2
        mod = mods[i]
        if i % 2 == 0:
            q, k, v, pp, *h = _in_a(in_rows, src, mod, norm1_g[i], a_w_in[j], a_q_norm_g[j], a_w_uq[j],
                                    a_kv_norm_g[j], a_w_ukv[j], cos_a, sin_a)
            a_lat, a_ctx = _attn_a(rows, q, k, v, ctx_out)
            halves = [(a_lat, a_ctx, 0), (_pool(pool_rows, pp, a_w_pool[j], a_pool_scale[j]), None, 0)]
            w_out = a_w_out[j]
        else:
            q, k, v, *h = _in_c(in_rows, src, mod, norm1_g[i], c_w_in[j], cos_c, sin_c)
            a_lat, a_ctx = _attn_c(rows, q, k, v, c_sink[j], ctx_out)
            halves = [(a_lat, a_ctx, 0), (a_lat, a_ctx, 1)]
            w_out = c_w_out[j]
        h = h[0] if h else src
        n_blocks = rows.all_blocks if ctx_out else rows.lat_blocks
        wr = jnp.zeros((d, ROUTER_W), F32).at[:, :N_GROUPS].set(r_w_group[i])
        wr = wr.at[:, N_GROUPS:N_GROUPS + N_EXPERTS].set(r_w_expert[i])
        br = jnp.zeros((1, ROUTER_W), F32).at[0, :N_GROUPS].set(r_b_group[i])
        br = br.at[0, N_GROUPS:N_GROUPS + N_EXPERTS].set(r_b_expert[i])
        hn, z, ri, rw, cnt = _out_proj(rows, n_blocks, halves, w_out, h, mod, norm2_g[i], wr, br)

        row_tok, block_e, next_e, n_used, dest = _plan(ri, cnt, MOE_ROWS)
        xs = z.at[row_tok].get(mode='promise_in_bounds')
        y = _moe_blocks(i, xs, block_e, next_e, n_used, e_w_gate, e_w_up, e_w_down)
        y0 = y.at[dest[:, 0]].get(mode='promise_in_bounds')
        y1 = y.at[dest[:, 1]].get(mode='promise_in_bounds')
        src = (hn, y0, y1, rw, mod)
    return _final(rows, rows.lat_blocks, *src, final_g).reshape(n_b, n_lat, d)
```

```python
import functools

import numpy as np
import jax
import jax.numpy as jnp
from jax import lax
from jax.experimental import pallas as pl
from jax.experimental.pallas import tpu as pltpu

F32 = jnp.float32
BF16 = jnp.bfloat16

D_MODEL = 2048
DEPTH = 4
GRID_W = 64
EPS = 1e-6
ROPE_BASE = 10000.0
NEG_INF = -1e30
N_MOD = 6

A_NOPE = 128
A_ROPE = 64
A_V = 128
A_HEADS = 8
A_Q_RANK = 512
A_KV_RANK = 256
A_QK_PAD = 256
B_WINDOWS = (2, 4, 8, 16)
B_GROUP_W = 256
B_WIDTH = 1024
POOL_HALO = 8

C_HEAD_DIM = 64
C_HEADS = 32
C_KV_HEADS = 4
C_GROUP = 8
C_WINDOW = 128
C_Q_W = C_HEADS * C_HEAD_DIM
C_KV_W = C_KV_HEADS * C_HEAD_DIM

N_GROUPS = 4
EXPERTS_PER_GROUP = 8
N_EXPERTS = 32
TOP_K = 2
D_EXPERT = 512
ROUTER_W = 128

ROW_BLOCK = 512
IN_ROWS = 256
POOL_ROWS = 256
MOE_ROWS = 512
LANES = 128
VMEM_LIMIT = 56 * 1024 * 1024
LOG2E = 1.4426950408889634
ATTN_A_ROWS = 256
ATTN_A_KEYS = 512
C_KV_PAD = 128


def _cparams(n_axes):
    return pltpu.CompilerParams(dimension_semantics=("arbitrary",) * n_axes,
                                vmem_limit_bytes=VMEM_LIMIT)


def _dot(a, b):
    return jnp.dot(a, b, preferred_element_type=F32)


def _dot_t(a, b):
    return lax.dot_general(a, b, (((1,), (1,)), ((), ())), preferred_element_type=F32)


def _rms(x):
    return x * lax.rsqrt(jnp.mean(x * x, axis=-1, keepdims=True) + EPS)


def _norm_mod(h, g, shift, scale):
    return (_rms(h) * g) * (1 + scale) + shift


def _mod_kernel(s_ref, w_ref, b_ref, o_ref):
    s = s_ref[...]
    s = s * jax.nn.sigmoid(s)
    o_ref[...] = _dot(s.astype(BF16), w_ref[...].astype(BF16)) + b_ref[...]


def _modulation(cvec, mod_w, mod_b):
    depth, d, n = mod_w.shape
    tn = 1024
    return pl.pallas_call(
        _mod_kernel,
        out_shape=jax.ShapeDtypeStruct((depth, 8, n), F32),
        grid=(depth, n // tn),
        in_specs=[pl.BlockSpec((8, d), lambda l, j: (0, 0)),
                  pl.BlockSpec((None, d, tn), lambda l, j: (l, 0, j)),
                  pl.BlockSpec((None, 1, tn), lambda l, j: (l, 0, j))],
        out_specs=pl.BlockSpec((None, 8, tn), lambda l, j: (l, 0, j)),
        compiler_params=_cparams(2),
        name="modulation",
    )(cvec, mod_w, mod_b.reshape(depth, 1, n))


class _Rows:
    def __init__(self, n_b, n_lat, n_ctx, bm):
        self.n_b, self.n_lat, self.n_ctx, self.bm = n_b, n_lat, n_ctx, bm
        self.t_lat = n_b * n_lat
        self.t_all = self.t_lat + n_b * n_ctx
        assert n_lat % bm == 0 and (n_b * n_ctx) % bm == 0
        self.lat_blocks = self.t_lat // bm
        self.all_blocks = self.t_all // bm
        self.blocks_per_seq = n_lat // bm

    def mod_index(self, i):
        return jnp.minimum(i // self.blocks_per_seq, self.n_b)

    def pos_index(self, i):
        return jnp.where(i < self.lat_blocks, i % self.blocks_per_seq, self.blocks_per_seq)


def _rope_tables(n_lat, n_ctx):
    axis_dim = A_ROPE // 2
    inv_freq = ROPE_BASE ** (-jnp.arange(axis_dim // 2, dtype=F32) * 2.0 / axis_dim)
    rows = n_lat // GRID_W
    row = jnp.repeat(jnp.arange(rows, dtype=F32), GRID_W)
    col = jnp.tile(jnp.arange(GRID_W, dtype=F32), rows)
    ang_r = row[:, None] * inv_freq
    ang_c = col[:, None] * inv_freq
    cr, sr, cc, sc = jnp.cos(ang_r), jnp.sin(ang_r), jnp.cos(ang_c), jnp.sin(ang_c)
    cos = jnp.concatenate([cr, cr, cc, cc], axis=-1)
    sin = jnp.concatenate([-sr, sr, -sc, sc], axis=-1)
    cos = jnp.concatenate([cos, jnp.ones((n_ctx, 64), F32)], axis=0)
    sin = jnp.concatenate([sin, jnp.zeros((n_ctx, 64), F32)], axis=0)
    return cos, sin


_ROPE_SWAP = np.concatenate([np.arange(16, 32), np.arange(0, 16), np.arange(48, 64), np.arange(32, 48)])


def _moe_residual(hn_ref, y0_ref, y1_ref, rw_ref, m_ref):
    rw = rw_ref[...]
    y = rw[:, 0:1] * y0_ref[...].astype(F32) + rw[:, 1:2] * y1_ref[...].astype(F32)
    return hn_ref[...] + m_ref[...][5:6] * y


def _in_a_kernel(*refs, q_scale, pending):
    n_src = 5 if pending else 1
    (m_ref, g_ref, win_ref, gq_ref, wuq_ref, gkv_ref, wk_ref, wv_ref, cos_ref, sin_ref) = refs[n_src:n_src + 10]
    q_ref, k_ref, v_ref, pp_ref = refs[n_src + 10:n_src + 14]
    if pending:
        h = _moe_residual(*refs[:5])
        refs[n_src + 14][...] = h
    else:
        h = refs[0][...]
    m = m_ref[...]
    z = _norm_mod(h, g_ref[...], m[0:1], m[1:2])
    p = _dot(z.astype(BF16), win_ref[...])
    cos = cos_ref[...]
    sin = sin_ref[...]

    cqn = _rms(p[:, :A_Q_RANK]) * gq_ref[...]
    qraw = _dot(cqn.astype(BF16), wuq_ref[...])
    for hd in range(A_HEADS):
        c0 = hd * A_QK_PAD
        t = qraw[:, c0 + A_NOPE:c0 + A_QK_PAD]
        rot = t * cos + pltpu.roll(t, 64, 1) * sin
        q_ref[:, c0:c0 + A_NOPE] = (qraw[:, c0:c0 + A_NOPE] * q_scale).astype(BF16)
        q_ref[:, c0 + A_NOPE:c0 + A_QK_PAD] = (rot * q_scale).astype(BF16)

    ckvn = (_rms(p[:, A_Q_RANK:A_Q_RANK + A_KV_RANK]) * gkv_ref[...]).astype(BF16)
    kn = _dot(ckvn, wk_ref[...])
    kt = p[:, 768:896]
    krot = (kt * cos + pltpu.roll(kt, 64, 1) * sin).astype(BF16)
    for hd in range(A_HEADS):
        c0 = hd * A_QK_PAD
        k_ref[:, c0:c0 + A_NOPE] = kn[:, hd * A_NOPE:(hd + 1) * A_NOPE].astype(BF16)
        k_ref[:, c0 + A_NOPE:c0 + A_QK_PAD] = krot
    v_ref[...] = _dot(ckvn, wv_ref[...]).astype(BF16)
    pp_ref[...] = p[:, 896:]


def _stream_specs(rows, src):
    bm = rows.bm
    row = lambda i: (i, 0)
    if not isinstance(src, tuple):
        return [pl.BlockSpec((bm, src.shape[1]), row)], [src], [], []
    hn, y0, y1, rw, mod_prev = src
    d = hn.shape[1]
    specs = [pl.BlockSpec((bm, d), row), pl.BlockSpec((bm, d), row), pl.BlockSpec((bm, d), row),
             pl.BlockSpec((bm, ROUTER_W), row),
             pl.BlockSpec((None, N_MOD, d), lambda i: (rows.mod_index(i), 0, 0))]
    return specs, list(src), [jax.ShapeDtypeStruct(hn.shape, F32)], [pl.BlockSpec((bm, d), row)]


def _in_a(rows, src, mod, g1, w_in, gq, w_uq, gkv, w_ukv, cos2, sin2):
    s_specs, s_args, h_shape, h_spec = _stream_specs(rows, src)
    t, d = s_args[0].shape
    off_rope = A_Q_RANK + A_KV_RANK
    win = jnp.concatenate([w_in[:, :off_rope + A_ROPE], w_in[:, off_rope + _ROPE_SWAP],
                           w_in[:, off_rope + A_ROPE:]], axis=1).astype(BF16)
    wq = w_uq.reshape(A_Q_RANK, A_HEADS, A_NOPE + A_ROPE)
    wuq = jnp.concatenate([wq, wq[:, :, A_NOPE + _ROPE_SWAP]], axis=-1).reshape(A_Q_RANK, A_HEADS * A_QK_PAD)
    wkv = w_ukv.reshape(A_KV_RANK, A_HEADS, A_NOPE + A_V)
    wk = wkv[:, :, :A_NOPE].reshape(A_KV_RANK, A_HEADS * A_NOPE).astype(BF16)
    wv = wkv[:, :, A_NOPE:].reshape(A_KV_RANK, A_HEADS * A_V).astype(BF16)
    n_in = win.shape[1]
    bm = rows.bm
    const = lambda i: (0, 0)
    row = lambda i: (i, 0)
    return pl.pallas_call(
        functools.partial(_in_a_kernel, q_scale=float((A_NOPE + A_ROPE) ** -0.5 * LOG2E),
                          pending=isinstance(src, tuple)),
        out_shape=[jax.ShapeDtypeStruct((t, A_HEADS * A_QK_PAD), BF16),
                   jax.ShapeDtypeStruct((t, A_HEADS * A_QK_PAD), BF16),
                   jax.ShapeDtypeStruct((t, A_HEADS * A_V), BF16),
                   jax.ShapeDtypeStruct((t, B_WIDTH), F32)] + h_shape,
        grid=(rows.all_blocks,),
        in_specs=s_specs + [
                  pl.BlockSpec((None, N_MOD, d), lambda i: (rows.mod_index(i), 0, 0)),
                  pl.BlockSpec((1, d), const),
                  pl.BlockSpec((d, n_in), const),
                  pl.BlockSpec((1, A_Q_RANK), const),
                  pl.BlockSpec((A_Q_RANK, A_HEADS * A_QK_PAD), const),
                  pl.BlockSpec((1, A_KV_RANK), const),
                  pl.BlockSpec((A_KV_RANK, A_HEADS * A_NOPE), const),
                  pl.BlockSpec((A_KV_RANK, A_HEADS * A_V), const),
                  pl.BlockSpec((bm, LANES), lambda i: (rows.pos_index(i), 0)),
                  pl.BlockSpec((bm, LANES), lambda i: (rows.pos_index(i), 0))],
        out_specs=[pl.BlockSpec((bm, A_HEADS * A_QK_PAD), row),
                   pl.BlockSpec((bm, A_HEADS * A_QK_PAD), row),
                   pl.BlockSpec((bm, A_HEADS * A_V), row),
                   pl.BlockSpec((bm, B_WIDTH), row)] + h_spec,
        compiler_params=_cparams(1),
        name="in_proj_a",
    )(*s_args, mod, g1.reshape(1, d), win, gq.reshape(1, -1), wuq.astype(BF16), gkv.reshape(1, -1), wk, wv, cos2, sin2)


def _attn_a_kernel(*refs, n_kv):
    q_ref = refs[0]
    k_refs = refs[1:1 + n_kv]
    v_refs = refs[1 + n_kv:1 + 2 * n_kv]
    o_ref = refs[1 + 2 * n_kv]
    chunks = []
    for k, v in zip(k_refs, v_refs):
        for r0 in range(0, k.shape[0], ATTN_A_KEYS):
            chunks.append((k, v, r0, min(ATTN_A_KEYS, k.shape[0] - r0)))
    bq = min(ATTN_A_ROWS, q_ref.shape[0])

    def scores(rows):
        q = q_ref[rows, :]
        s = []
        top = None
        for k, _, r0, n in chunks:
            si = _dot_t(q, k[r0:r0 + n, :])
            s.append(si)
            for j in range(n // LANES):
                tile = si[:, j * LANES:(j + 1) * LANES]
                top = tile if top is None else jnp.maximum(top, tile)
        return s, top.max(axis=-1, keepdims=True)

    def attend(rows, s, mx):
        o = None
        den = None
        for si, (_, v, r0, n) in zip(s, chunks):
            e = jnp.exp2(si - mx)
            for j in range(n // LANES):
                tile = e[:, j * LANES:(j + 1) * LANES]
                den = tile if den is None else den + tile
            part = _dot(e.astype(BF16), v[r0:r0 + n, :])
            o = part if o is None else o + part
        o_ref[rows, :] = (o / den.sum(axis=-1, keepdims=True)).astype(BF16)

    pair = 2 if q_ref.shape[0] % (2 * bq) == 0 else 1

    def block(i, carry):
        rows = [pl.ds(pl.multiple_of((i * pair + u) * bq, bq), bq) for u in range(pair)]
        staged = [scores(r) for r in rows]
        for r, (s, mx) in zip(rows, staged):
            attend(r, s, mx)
        return carry

    lax.fori_loop(0, q_ref.shape[0] // (pair * bq), block, 0)


def _attn_a(rows, q, k, v, ctx_out):
    n_lat, n_ctx = rows.n_lat, rows.n_ctx
    ctx_blk = rows.t_lat // n_ctx
    lat = pl.pallas_call(
        functools.partial(_attn_a_kernel, n_kv=2),
        out_shape=jax.ShapeDtypeStruct((rows.t_lat, A_HEADS * A_V), BF16),
        grid=(rows.n_b, A_HEADS),
        in_specs=[pl.BlockSpec((n_lat, A_QK_PAD), lambda b, h: (b, h)),
                  pl.BlockSpec((n_lat, A_QK_PAD), lambda b, h: (b, h)),
                  pl.BlockSpec((n_ctx, A_QK_PAD), lambda b, h: (ctx_blk + b, h)),
                  pl.BlockSpec((n_lat, A_V), lambda b, h: (b, h)),
                  pl.BlockSpec((n_ctx, A_V), lambda b, h: (ctx_blk + b, h))],
        out_specs=pl.BlockSpec((n_lat, A_V), lambda b, h: (b, h)),
        compiler_params=_cparams(2),
        name="attn_a_lat",
    )(q, k, k, v, v)
    if not ctx_out:
        return lat, None
    ctx = pl.pallas_call(
        functools.partial(_attn_a_kernel, n_kv=1),
        out_shape=jax.ShapeDtypeStruct((rows.n_b * n_ctx, A_HEADS * A_V), BF16),
        grid=(rows.n_b, A_HEADS),
        in_specs=[pl.BlockSpec((n_ctx, A_QK_PAD), lambda b, h: (ctx_blk + b, h)),
                  pl.BlockSpec((n_ctx, A_QK_PAD), lambda b, h: (ctx_blk + b, h)),
                  pl.BlockSpec((n_ctx, A_V), lambda b, h: (ctx_blk + b, h))],
        out_specs=pl.BlockSpec((n_ctx, A_V), lambda b, h: (b, h)),
        compiler_params=_cparams(2),
        name="attn_a_ctx",
    )(q, k, v)
    return lat, ctx


def _pool_kernel(prev_ref, cur_ref, next_ref, w_ref, s_ref, o_ref, ext_ref, *, rows):
    i = pl.program_id(0)
    bm = rows.bm
    is_ctx = i >= rows.lat_blocks
    n_l = jnp.where(is_ctx, rows.n_ctx, rows.n_lat)
    pos0 = jnp.where(is_ctx, 0, (i % rows.blocks_per_seq) * bm)
    ext_ref[0:POOL_HALO, :] = jnp.where(pos0 > 0, prev_ref[...], 0.0)
    ext_ref[POOL_HALO:POOL_HALO + bm, :] = cur_ref[...]
    ext_ref[POOL_HALO + bm:, :] = jnp.where(pos0 + bm < n_l, next_ref[...], 0.0)
    pos = pos0 + lax.broadcasted_iota(jnp.int32, (bm, 1), 0)
    for gi, w in enumerate(B_WINDOWS):
        cols = slice(gi * B_GROUP_W, (gi + 1) * B_GROUP_W)
        acc = ext_ref[POOL_HALO - w // 2:POOL_HALO - w // 2 + bm, cols]
        for dlt in range(-(w // 2) + 1, w // 2):
            acc = acc + ext_ref[POOL_HALO + dlt:POOL_HALO + dlt + bm, cols]
        cnt = jnp.minimum(pos + (w // 2 - 1), n_l - 1) - jnp.maximum(pos - w // 2, 0) + 1
        dev = acc / cnt.astype(F32) - cur_ref[:, cols]
        y = _dot(dev.astype(BF16), w_ref[gi]) * s_ref[:, cols]
        o_ref[:, cols] = y.astype(BF16)


def _pool(rows, pp, w_pool, pool_scale):
    t = pp.shape[0]
    bm = rows.bm
    assert rows.n_ctx == bm
    per = bm // POOL_HALO
    last = t // POOL_HALO - 1
    return pl.pallas_call(
        functools.partial(_pool_kernel, rows=rows),
        out_shape=jax.ShapeDtypeStruct((t, B_WIDTH), BF16),
        grid=(rows.all_blocks,),
        in_specs=[pl.BlockSpec((POOL_HALO, B_WIDTH), lambda i: (jnp.maximum(i * per - 1, 0), 0)),
                  pl.BlockSpec((bm, B_WIDTH), lambda i: (i, 0)),
                  pl.BlockSpec((POOL_HALO, B_WIDTH), lambda i: (jnp.minimum((i + 1) * per, last), 0)),
                  pl.BlockSpec((len(B_WINDOWS), B_GROUP_W, B_GROUP_W), lambda i: (0, 0, 0)),
                  pl.BlockSpec((1, B_WIDTH), lambda i: (0, 0))],
        out_specs=pl.BlockSpec((bm, B_WIDTH), lambda i: (i, 0)),
        scratch_shapes=[pltpu.VMEM((bm + 2 * POOL_HALO, B_WIDTH), F32)],
        compiler_params=_cparams(1),
        name="pool",
    )(pp, pp, pp, w_pool.astype(BF16), pool_scale.reshape(1, -1))


def _out_kernel(*refs, split, lat_blocks):
    i = pl.program_id(0)
    halves = []
    n = 0
    for is_pair in split:
        if is_pair:
            halves.append(jnp.where(i < lat_blocks, refs[n][...], refs[n + 1][...]))
        else:
            halves.append(refs[n][...])
        n += 2 if is_pair else 1
    (w1_ref, w2_ref, h_ref, m_ref, g_ref, wr1_ref, wr2_ref, br_ref,
     hn_ref, z_ref, ri_ref, rw_ref, cnt_ref, carry_ref) = refs[n:]
    bm = h_ref.shape[0]

    @pl.when(i == 0)
    def _():
        carry_ref[...] = jnp.zeros_like(carry_ref)

    m = m_ref[...]
    o = _dot(halves[0], w1_ref[...]) + _dot(halves[1], w2_ref[...])
    hn = h_ref[...] + m[2:3] * o
    hn_ref[...] = hn
    z = _norm_mod(hn, g_ref[...], m[3:4], m[4:5])
    z_hi = z.astype(BF16)
    z_ref[...] = z_hi
    z_lo = (z - z_hi.astype(F32)).astype(BF16)
    l2 = _dot(z_hi, wr1_ref[...])
    lg = l2[:, :ROUTER_W] + l2[:, ROUTER_W:] + _dot(z_lo, wr2_ref[...]) + br_ref[...]

    lane = lax.broadcasted_iota(jnp.int32, (bm, ROUTER_W), 1)
    low = jnp.float32(-3e38)
    is_g = lane < N_GROUPS
    glog = jnp.where(is_g, lg, low)
    gmax = glog.max(axis=-1, keepdims=True)
    g_idx = jnp.where(glog == gmax, lane, ROUTER_W).min(axis=-1, keepdims=True)
    g_gate = 1.0 / jnp.where(is_g, jnp.exp(lg - gmax), 0.0).sum(axis=-1, keepdims=True)
    lo = N_GROUPS + EXPERTS_PER_GROUP * g_idx
    el = jnp.where((lane >= lo) & (lane < lo + EXPERTS_PER_GROUP), lg, low)
    v1 = el.max(axis=-1, keepdims=True)
    i1 = jnp.where(el == v1, lane, ROUTER_W).min(axis=-1, keepdims=True)
    el2 = jnp.where(lane == i1, low, el)
    v2 = el2.max(axis=-1, keepdims=True)
    i2 = jnp.where(el2 == v2, lane, ROUTER_W).min(axis=-1, keepdims=True)
    e21 = jnp.exp(v2 - v1)
    w1 = g_gate * (1.0 / (1.0 + e21))
    w2 = g_gate * (e21 / (1.0 + e21))

    hit1 = lane == i1
    hit2 = lane == i2
    onehot = jnp.where(hit1 | hit2, 1.0, 0.0)
    r_i = lax.broadcasted_iota(jnp.int32, (bm, bm), 0)
    c_i = lax.broadcasted_iota(jnp.int32, (bm, bm), 1)
    before = _dot(jnp.where(r_i > c_i, 1.0, 0.0).astype(BF16), onehot.astype(BF16)) + carry_ref[0:1, :]
    rank1 = jnp.where(hit1, before, 0.0).sum(axis=-1, keepdims=True).astype(jnp.int32)
    rank2 = jnp.where(hit2, before, 0.0).sum(axis=-1, keepdims=True).astype(jnp.int32)
    total = carry_ref[0:1, :] + onehot.sum(axis=0, keepdims=True)
    carry_ref[...] = jnp.broadcast_to(total, carry_ref.shape)
    cnt_ref[...] = jnp.broadcast_to(total, cnt_ref.shape)

    ri = jnp.where(lane == 0, i1 - N_GROUPS, jnp.where(lane == 1, i2 - N_GROUPS,
                   jnp.where(lane == 2, rank1, jnp.where(lane == 3, rank2, 0))))
    ri_ref[...] = ri.T[:8]
    rw_ref[...] = jnp.where(lane == 0, w1, jnp.where(lane == 1, w2, 0.0))


def _out_proj(rows, n_blocks, halves, w_out, h, mod, g2, wr, br):
    t, d = h.shape
    bm = rows.bm
    t_out = n_blocks * bm
    half = w_out.shape[0] // 2
    w = w_out.astype(BF16)
    wr_hi = wr.astype(BF16)
    wr_lo = (wr - wr_hi.astype(F32)).astype(BF16)
    const = lambda i: (0, 0)
    row = lambda i: (i, 0)
    lat_blocks = rows.lat_blocks
    a_specs, a_args = [], []
    for lat, ctx, col in halves:
        if ctx is None:
            a_specs.append(pl.BlockSpec((bm, half), lambda i, col=col: (i, col)))
            a_args.append(lat)
        else:
            a_specs.append(pl.BlockSpec((bm, half), lambda i, col=col: (jnp.minimum(i, lat_blocks - 1), col)))
            a_specs.append(pl.BlockSpec((bm, half), lambda i, col=col: (jnp.maximum(i - lat_blocks, 0), col)))
            a_args += [lat, ctx]
    return pl.pallas_call(
        functools.partial(_out_kernel, split=tuple(ctx is not None for _, ctx, _ in halves), lat_blocks=lat_blocks),
        out_shape=(jax.ShapeDtypeStruct((t_out, d), F32),
                   jax.ShapeDtypeStruct((t_out, d), BF16),
                   jax.ShapeDtypeStruct((8, t_out), jnp.int32),
                   jax.ShapeDtypeStruct((t_out, ROUTER_W), F32),
                   jax.ShapeDtypeStruct((8, ROUTER_W), F32)),
        grid=(n_blocks,),
        in_specs=a_specs + [
                  pl.BlockSpec((half, d), lambda i: (0, 0)),
                  pl.BlockSpec((half, d), lambda i: (1, 0)),
                  pl.BlockSpec((bm, d), row),
                  pl.BlockSpec((None, N_MOD, d), lambda i: (rows.mod_index(i), 0, 0)),
                  pl.BlockSpec((1, d), const),
                  pl.BlockSpec((d, 2 * ROUTER_W), const),
                  pl.BlockSpec((d, ROUTER_W), const),
                  pl.BlockSpec((1, ROUTER_W), const)],
        out_specs=(pl.BlockSpec((bm, d), row),
                   pl.BlockSpec((bm, d), row),
                   pl.BlockSpec((8, bm), lambda i: (0, i)),
                   pl.BlockSpec((bm, ROUTER_W), row),
                   pl.BlockSpec((8, ROUTER_W), const)),
        scratch_shapes=[pltpu.VMEM((8, ROUTER_W), F32)],
        compiler_params=_cparams(1),
        name="out_proj",
    )(*a_args, w, w, h, mod, g2.reshape(1, d), jnp.concatenate([wr_hi, wr_lo], axis=1), wr_hi, br)


def _in_c_kernel(*refs, q_scale, pending):
    n_src = 5 if pending else 1
    m_ref, g_ref, win_ref, cos_ref, sin_ref = refs[n_src:n_src + 5]
    q_ref, k_ref, v_ref = refs[n_src + 5:n_src + 8]
    if pending:
        h = _moe_residual(*refs[:5])
        refs[n_src + 8][...] = h
    else:
        h = refs[0][...]
    m = m_ref[...]
    z = _norm_mod(h, g_ref[...], m[0:1], m[1:2])
    p = _dot(z.astype(BF16), win_ref[...])
    cos = cos_ref[...]
    sin = sin_ref[...]
    lane = lax.broadcasted_iota(jnp.int32, (1, LANES), 1)
    first = (lane % 32) < 16

    def rope(x):
        partner = jnp.where(first, pltpu.roll(x, LANES - 16, 1), pltpu.roll(x, 16, 1))
        return x * cos + partner * sin

    for tile in range(C_Q_W // LANES):
        cols = slice(tile * LANES, (tile + 1) * LANES)
        q_ref[:, cols] = (rope(p[:, cols]) * q_scale).astype(BF16)
    low_half = lane < C_HEAD_DIM
    for tile in range(C_KV_W // LANES):
        kk = rope(p[:, C_Q_W + tile * LANES:C_Q_W + (tile + 1) * LANES])
        vv = p[:, C_Q_W + C_KV_W + tile * LANES:C_Q_W + C_KV_W + (tile + 1) * LANES]
        ones = jnp.where(lane == C_HEAD_DIM, 1.0, 0.0)
        for half, (kh, vh) in enumerate(((kk, vv), (pltpu.roll(kk, C_HEAD_DIM, 1), pltpu.roll(vv, C_HEAD_DIM, 1)))):
            c0 = (2 * tile + half) * C_KV_PAD
            k_ref[:, c0:c0 + C_KV_PAD] = jnp.where(low_half, kh, 0.0).astype(BF16)
            v_ref[:, c0:c0 + C_KV_PAD] = jnp.where(low_half, vh, ones).astype(BF16)


def _in_c(rows, src, mod, g1, w_in, cos2, sin2):
    s_specs, s_args, h_shape, h_spec = _stream_specs(rows, src)
    t, d = s_args[0].shape
    bm = rows.bm
    n_in = w_in.shape[1]
    const = lambda i: (0, 0)
    row = lambda i: (i, 0)
    return pl.pallas_call(
        functools.partial(_in_c_kernel, q_scale=float(C_HEAD_DIM ** -0.5 * LOG2E), pending=isinstance(src, tuple)),
        out_shape=[jax.ShapeDtypeStruct((t, C_Q_W), BF16),
                   jax.ShapeDtypeStruct((t, C_KV_HEADS * C_KV_PAD), BF16),
                   jax.ShapeDtypeStruct((t, C_KV_HEADS * C_KV_PAD), BF16)] + h_shape,
        grid=(rows.all_blocks,),
        in_specs=s_specs + [
                  pl.BlockSpec((None, N_MOD, d), lambda i: (rows.mod_index(i), 0, 0)),
                  pl.BlockSpec((1, d), const),
                  pl.BlockSpec((d, n_in), const),
                  pl.BlockSpec((bm, LANES), lambda i: (rows.pos_index(i), 0)),
                  pl.BlockSpec((bm, LANES), lambda i: (rows.pos_index(i), 0))],
        out_specs=[pl.BlockSpec((bm, C_Q_W), row),
                   pl.BlockSpec((bm, C_KV_HEADS * C_KV_PAD), row),
                   pl.BlockSpec((bm, C_KV_HEADS * C_KV_PAD), row)] + h_spec,
        compiler_params=_cparams(1),
        name="in_proj_c",
    )(*s_args, mod, g1.reshape(1, d), w_in.astype(BF16), cos2, sin2)


def _sink_attend(q_ref, sink_ref, keys, vals, masks, o_ref, bq):
    def scores(kv):
        q8 = jnp.concatenate([q_ref[:, (kv * C_GROUP + g) * C_HEAD_DIM:(kv * C_GROUP + g + 1) * C_HEAD_DIM]
                              for g in range(C_GROUP)], axis=0)
        return [_dot_t(q8, k[:, kv * C_KV_PAD:kv * C_KV_PAD + C_HEAD_DIM]) for k in keys]

    s_next = scores(0)
    for kv in range(C_KV_HEADS):
        s = s_next
        if kv + 1 < C_KV_HEADS:
            s_next = scores(kv + 1)
        e, mxs = [], []
        for g in range(C_GROUP):
            sg = [si[g * bq:(g + 1) * bq] for si in s]
            sg = [si if msk is None else jnp.where(msk, si, NEG_INF) for si, msk in zip(sg, masks)]
            sink = sink_ref[kv * C_GROUP + g] * LOG2E
            tiles = [si[:, j * LANES:(j + 1) * LANES] for si in sg for j in range(si.shape[1] // LANES)]
            mx = jnp.maximum(sink, functools.reduce(jnp.maximum, tiles).max(axis=-1, keepdims=True))
            e.append([jnp.exp2(si - mx).astype(BF16) for si in sg])
            mxs.append((sink, mx))
        oe = None
        for piece, v in enumerate(vals):
            part = _dot(jnp.concatenate([eg[piece] for eg in e], axis=0), v[:, kv * C_KV_PAD:(kv + 1) * C_KV_PAD])
            oe = part if oe is None else oe + part
        for g, (sink, mx) in enumerate(mxs):
            og = oe[g * bq:(g + 1) * bq]
            den = og[:, C_HEAD_DIM:C_HEAD_DIM + 1] + jnp.exp2(sink - mx)
            c0 = (kv * C_GROUP + g) * C_HEAD_DIM
            o_ref[:, c0:c0 + C_HEAD_DIM] = (og[:, :C_HEAD_DIM] / den).astype(BF16)


def _attn_c_lat_kernel(sink_ref, q_ref, kp_ref, kc_ref, kn_ref, kx_ref, vp_ref, vc_ref, vn_ref, vx_ref, o_ref,
                       *, n_blk):
    n = pl.program_id(1)
    bq = C_WINDOW
    band = 3 * bq
    kband = jnp.concatenate([kp_ref[...], kc_ref[...], kn_ref[...]], axis=0)
    vband = jnp.concatenate([vp_ref[...], vc_ref[...], vn_ref[...]], axis=0)
    qi = lax.broadcasted_iota(jnp.int32, (bq, band), 0)
    kj = lax.broadcasted_iota(jnp.int32, (bq, band), 1)
    rel = qi - (kj - bq)
    ok = (jnp.abs(rel) <= C_WINDOW) & ((kj >= bq) | (n > 0)) & ((kj < 2 * bq) | (n < n_blk - 1))
    _sink_attend(q_ref, sink_ref, [kband, kx_ref[...]], [vband, vx_ref[...]], [ok, None], o_ref, bq)


def _attn_c_ctx_kernel(sink_ref, q_ref, kx_ref, vx_ref, o_ref):
    _sink_attend(q_ref, sink_ref, [kx_ref[...]], [vx_ref[...]], [None], o_ref, q_ref.shape[0])


def _attn_c(rows, q, k, v, sink, ctx_out):
    t = q.shape[0]
    bq = C_WINDOW
    n_blk = rows.n_lat // bq
    n_ctx = rows.n_ctx
    ctx_blk = rows.t_lat // n_ctx
    kvw = C_KV_HEADS * C_KV_PAD
    del t
    smem = pl.BlockSpec(memory_space=pltpu.SMEM)
    prev = lambda b, n: (b * n_blk + jnp.maximum(n - 1, 0), 0)
    cur = lambda b, n: (b * n_blk + n, 0)
    nxt = lambda b, n: (b * n_blk + jnp.minimum(n + 1, n_blk - 1), 0)
    cx = lambda b, n: (ctx_blk + b, 0)
    kv_specs = [pl.BlockSpec((bq, kvw), prev), pl.BlockSpec((bq, kvw), cur),
                pl.BlockSpec((bq, kvw), nxt), pl.BlockSpec((n_ctx, kvw), cx)]
    lat = pl.pallas_call(
        functools.partial(_attn_c_lat_kernel, n_blk=n_blk),
        out_shape=jax.ShapeDtypeStruct((rows.t_lat, C_Q_W), BF16),
        grid=(rows.n_b, n_blk),
        in_specs=[smem, pl.BlockSpec((bq, C_Q_W), cur)] + kv_specs + kv_specs,
        out_specs=pl.BlockSpec((bq, C_Q_W), cur),
        compiler_params=_cparams(2),
        name="attn_c_lat",
    )(sink, q, k, k, k, k, v, v, v, v)
    if not ctx_out:
        return lat, None
    cxb = lambda b: (ctx_blk + b, 0)
    ctx = pl.pallas_call(
        _attn_c_ctx_kernel,
        out_shape=jax.ShapeDtypeStruct((rows.n_b * n_ctx, C_Q_W), BF16),
        grid=(rows.n_b,),
        in_specs=[smem, pl.BlockSpec((n_ctx, C_Q_W), cxb), pl.BlockSpec((n_ctx, kvw), cxb),
                  pl.BlockSpec((n_ctx, kvw), cxb)],
        out_specs=pl.BlockSpec((n_ctx, C_Q_W), lambda b: (b, 0)),
        compiler_params=_cparams(1),
        name="attn_c_ctx",
    )(sink, q, k, v)
    return lat, ctx


def _moe_kernel(be_ref, ne_ref, nu_ref, x_ref, wg_hbm, wu_hbm, wd_hbm, y_ref,
                wg_f, wu_f, wd_f, wgu_s, wd_s, sem, *, layer):
    i = pl.program_id(0)
    used = i < nu_ref[0]
    expert = be_ref[i]
    fresh = (i == 0) | (expert != be_ref[jnp.maximum(i - 1, 0)])

    def weight_copies(e):
        return (pltpu.make_async_copy(wg_hbm.at[layer, e], wg_f, sem.at[0]),
                pltpu.make_async_copy(wu_hbm.at[layer, e], wu_f, sem.at[1]),
                pltpu.make_async_copy(wd_hbm.at[layer, e], wd_f, sem.at[2]))

    @pl.when(used & (i == 0))
    def _():
        for c in weight_copies(expert):
            c.start()

    @pl.when(used & fresh)
    def _():
        for c in weight_copies(expert):
            c.wait()
        wgu_s[:, :D_EXPERT] = wg_f[...].astype(BF16)
        wgu_s[:, D_EXPERT:] = wu_f[...].astype(BF16)
        wd_s[...] = wd_f[...].astype(BF16)
        nxt = ne_ref[i]

        @pl.when(nxt >= 0)
        def _():
            for c in weight_copies(nxt):
                c.start()

    @pl.when(used)
    def _():
        gu = _dot(x_ref[...], wgu_s[...])
        g = gu[:, :D_EXPERT]
        a = (g * jax.nn.sigmoid(g)) * gu[:, D_EXPERT:]
        y_ref[...] = _dot(a.astype(BF16), wd_s[...]).astype(BF16)

    @pl.when(jnp.logical_not(used))
    def _():
        y_ref[...] = jnp.zeros_like(y_ref)


def _moe_blocks(layer, xs, block_e, next_e, n_used, w_gate, w_up, w_down):
    r, d = xs.shape
    bm = MOE_ROWS
    nb = r // bm
    hbm = pl.BlockSpec(memory_space=pl.ANY)
    return pl.pallas_call(
        functools.partial(_moe_kernel, layer=layer),
        out_shape=jax.ShapeDtypeStruct((r, d), BF16),
        grid_spec=pltpu.PrefetchScalarGridSpec(
            num_scalar_prefetch=3,
            grid=(nb,),
            in_specs=[pl.BlockSpec((bm, d), lambda i, be, ne, nu: (i, 0)), hbm, hbm, hbm],
            out_specs=pl.BlockSpec((bm, d), lambda i, be, ne, nu: (i, 0)),
            scratch_shapes=[pltpu.VMEM((d, D_EXPERT), F32), pltpu.VMEM((d, D_EXPERT), F32),
                            pltpu.VMEM((D_EXPERT, d), F32),
                            pltpu.VMEM((d, 2 * D_EXPERT), BF16), pltpu.VMEM((D_EXPERT, d), BF16),
                            pltpu.SemaphoreType.DMA((3,))]),
        compiler_params=_cparams(1),
        name="moe_experts",
    )(block_e, next_e, n_used, xs, w_gate, w_up, w_down)


def _plan(ri, cnt, bm):
    n_t = ri.shape[1]
    n_tk = n_t * TOP_K
    nb = -(-(n_tk + N_EXPERTS * (bm - 1)) // bm)
    counts = cnt[0, N_GROUPS:N_GROUPS + N_EXPERTS].astype(jnp.int32)
    padded = (counts + bm - 1) // bm * bm
    pends = jnp.cumsum(padded)
    dest = (pends - padded)[ri[:TOP_K]] + ri[TOP_K:2 * TOP_K]
    row_tok = (jnp.arange(nb * bm, dtype=jnp.int32) % n_t).at[dest.reshape(-1)].set(
        jnp.arange(n_tk, dtype=jnp.int32) % n_t, unique_indices=True, mode='promise_in_bounds')
    n_used = (pends[-1] // bm).astype(jnp.int32)
    blk = jnp.arange(nb, dtype=jnp.int32)
    block_e = jnp.sum((blk[:, None] * bm >= pends[None, :]).astype(jnp.int32), axis=1)
    block_e = jnp.minimum(block_e, N_EXPERTS - 1)
    block_e = jnp.where(blk < n_used, block_e, block_e[n_used - 1])
    later = (block_e[None, :] > block_e[:, None]) & (blk[None, :] < n_used)
    next_e = jnp.min(jnp.where(later, block_e[None, :], N_EXPERTS), axis=1)
    next_e = jnp.where(next_e < N_EXPERTS, next_e, -1).astype(jnp.int32)
    return row_tok, block_e, next_e, n_used.reshape(1), dest


def _final_kernel(h_ref, y0_ref, y1_ref, rw_ref, m_ref, g_ref, o_ref):
    o_ref[...] = _rms(_moe_residual(h_ref, y0_ref, y1_ref, rw_ref, m_ref)) * g_ref[...]


def _final(rows, n_blocks, h, y0, y1, rw, mod, final_g):
    d = h.shape[1]
    bm = rows.bm
    row = lambda i: (i, 0)
    return pl.pallas_call(
        _final_kernel,
        out_shape=jax.ShapeDtypeStruct((n_blocks * bm, d), F32),
        grid=(n_blocks,),
        in_specs=[pl.BlockSpec((bm, d), row), pl.BlockSpec((bm, d), row), pl.BlockSpec((bm, d), row),
                  pl.BlockSpec((bm, ROUTER_W), row),
                  pl.BlockSpec((None, N_MOD, d), lambda i: (rows.mod_index(i), 0, 0)),
                  pl.BlockSpec((1, d), lambda i: (0, 0))],
        out_specs=pl.BlockSpec((bm, d), row),
        compiler_params=_cparams(1),
        name="final",
    )(h, y0, y1, rw, mod, final_g.reshape(1, d))


def kernel(x, c, ctx, c_ctx, mod_w, mod_b, norm1_g, norm2_g, final_g, a_w_in, a_q_norm_g, a_kv_norm_g, a_w_uq,
           a_w_ukv, a_w_pool, a_pool_scale, a_w_out, c_w_in, c_sink, c_w_out, r_w_group, r_b_group, r_w_expert,
           r_b_expert, e_w_gate, e_w_up, e_w_down):
    n_b, n_lat, d = x.shape
    n_ctx = ctx.shape[1]
    rows = _Rows(n_b, n_lat, n_ctx, ROW_BLOCK)
    in_rows = _Rows(n_b, n_lat, n_ctx, IN_ROWS)
    pool_rows = _Rows(n_b, n_lat, n_ctx, POOL_ROWS)
    depth = mod_w.shape[0]

    cvec = jnp.zeros((8, d), F32).at[:n_b].set(c).at[n_b].set(c_ctx)
    mods = _modulation(cvec, mod_w, mod_b).reshape(depth, 8, N_MOD, d)

    cos, sin = _rope_tables(n_lat, ROW_BLOCK)
    zeros = jnp.zeros_like(cos)
    cos_a, sin_a = jnp.concatenate([cos, zeros], axis=1), jnp.concatenate([sin, zeros], axis=1)
    cos_c, sin_c = jnp.concatenate([cos, cos], axis=1), jnp.concatenate([sin, sin], axis=1)

    src = jnp.concatenate([x.reshape(-1, d), ctx.reshape(-1, d)], axis=0)
    for i in range(depth):
        ctx_out = i < depth - 1
        j = i // 2
        mod = mods[i]
        if i % 2 == 0:
            q, k, v, pp, *h = _in_a(in_rows, src, mod, norm1_g[i], a_w_in[j], a_q_norm_g[j], a_w_uq[j],
                                    a_kv_norm_g[j], a_w_ukv[j], cos_a, sin_a)
            a_lat, a_ctx = _attn_a(rows, q, k, v, ctx_out)
            halves = [(a_lat, a_ctx, 0), (_pool(pool_rows, pp, a_w_pool[j], a_pool_scale[j]), None, 0)]
            w_out = a_w_out[j]
        else:
            q, k, v, *h = _in_c(in_rows, src, mod, norm1_g[i], c_w_in[j], cos_c, sin_c)
            a_lat, a_ctx = _attn_c(rows, q, k, v, c_sink[j], ctx_out)
            halves = [(a_lat, a_ctx, 0), (a_lat, a_ctx, 1)]
            w_out = c_w_out[j]
        h = h[0] if h else src
        n_blocks = rows.all_blocks if ctx_out else rows.lat_blocks
        wr = jnp.zeros((d, ROUTER_W), F32).at[:, :N_GROUPS].set(r_w_group[i])
        wr = wr.at[:, N_GROUPS:N_GROUPS + N_EXPERTS].set(r_w_expert[i])
        br = jnp.zeros((1, ROUTER_W), F32).at[0, :N_GROUPS].set(r_b_group[i])
        br = br.at[0, N_GROUPS:N_GROUPS + N_EXPERTS].set(r_b_expert[i])
        hn, z, ri, rw, cnt = _out_proj(rows, n_blocks, halves, w_out, h, mod, norm2_g[i], wr, br)

        row_tok, block_e, next_e, n_used, dest = _plan(ri, cnt, MOE_ROWS)
        xs = z.at[row_tok].get(mode='promise_in_bounds')
        y = _moe_blocks(i, xs, block_e, next_e, n_used, e_w_gate, e_w_up, e_w_down)
        y0 = y.at[dest[0]].get(mode='promise_in_bounds')
        y1 = y.at[dest[1]].get(mode='promise_in_bounds')
        src = (hn, y0, y1, rw, mod)
    return _final(rows, rows.lat_blocks, *src, final_g).reshape(n_b, n_lat, d)
```

```python
import functools

import numpy as np
import jax
import jax.numpy as jnp
from jax import lax
from jax.experimental import pallas as pl
from jax.experimental.pallas import tpu as pltpu

F32 = jnp.float32
BF16 = jnp.bfloat16

D_MODEL = 2048
DEPTH = 4
GRID_W = 64
EPS = 1e-6
ROPE_BASE = 10000.0
NEG_INF = -1e30
N_MOD = 6

A_NOPE = 128
A_ROPE = 64
A_V = 128
A_HEADS = 8
A_Q_RANK = 512
A_KV_RANK = 256
A_QK_PAD = 256
B_WINDOWS = (2, 4, 8, 16)
B_GROUP_W = 256
B_WIDTH = 1024
POOL_HALO = 8

C_HEAD_DIM = 64
C_HEADS = 32
C_KV_HEADS = 4
C_GROUP = 8
C_WINDOW = 128
C_Q_W = C_HEADS * C_HEAD_DIM
C_KV_W = C_KV_HEADS * C_HEAD_DIM

N_GROUPS = 4
EXPERTS_PER_GROUP = 8
N_EXPERTS = 32
TOP_K = 2
D_EXPERT = 512
ROUTER_W = 128

ROW_BLOCK = 512
IN_ROWS = 256
POOL_ROWS = 256
MOE_ROWS = 512
LANES = 128
VMEM_LIMIT = 56 * 1024 * 1024
LOG2E = 1.4426950408889634
ATTN_A_ROWS = 256
ATTN_A_KEYS = 512
C_KV_PAD = 128


def _cparams(n_axes):
    return pltpu.CompilerParams(dimension_semantics=("arbitrary",) * n_axes,
                                vmem_limit_bytes=VMEM_LIMIT)


def _dot(a, b):
    return jnp.dot(a, b, preferred_element_type=F32)


def _dot_t(a, b):
    return lax.dot_general(a, b, (((1,), (1,)), ((), ())), preferred_element_type=F32)


def _rms(x):
    return x * lax.rsqrt(jnp.mean(x * x, axis=-1, keepdims=True) + EPS)


def _norm_mod(h, g, shift, scale):
    return (_rms(h) * g) * (1 + scale) + shift


def _mod_kernel(s_ref, w_ref, b_ref, o_ref):
    s = s_ref[...]
    s = s * jax.nn.sigmoid(s)
    o_ref[...] = _dot(s.astype(BF16), w_ref[...].astype(BF16)) + b_ref[...]


def _modulation(cvec, mod_w, mod_b):
    depth, d, n = mod_w.shape
    tn = 1024
    return pl.pallas_call(
        _mod_kernel,
        out_shape=jax.ShapeDtypeStruct((depth, 8, n), F32),
        grid=(depth, n // tn),
        in_specs=[pl.BlockSpec((8, d), lambda l, j: (0, 0)),
                  pl.BlockSpec((None, d, tn), lambda l, j: (l, 0, j)),
                  pl.BlockSpec((None, 1, tn), lambda l, j: (l, 0, j))],
        out_specs=pl.BlockSpec((None, 8, tn), lambda l, j: (l, 0, j)),
        compiler_params=_cparams(2),
        name="modulation",
    )(cvec, mod_w, mod_b.reshape(depth, 1, n))


class _Rows:
    def __init__(self, n_b, n_lat, n_ctx, bm):
        self.n_b, self.n_lat, self.n_ctx, self.bm = n_b, n_lat, n_ctx, bm
        self.t_lat = n_b * n_lat
        self.t_all = self.t_lat + n_b * n_ctx
        assert n_lat % bm == 0 and (n_b * n_ctx) % bm == 0
        self.lat_blocks = self.t_lat // bm
        self.all_blocks = self.t_all // bm
        self.blocks_per_seq = n_lat // bm

    def mod_index(self, i):
        return jnp.minimum(i // self.blocks_per_seq, self.n_b)

    def pos_index(self, i):
        return jnp.where(i < self.lat_blocks, i % self.blocks_per_seq, self.blocks_per_seq)


def _rope_tables(n_lat, n_ctx):
    axis_dim = A_ROPE // 2
    inv_freq = ROPE_BASE ** (-jnp.arange(axis_dim // 2, dtype=F32) * 2.0 / axis_dim)
    rows = n_lat // GRID_W
    row = jnp.repeat(jnp.arange(rows, dtype=F32), GRID_W)
    col = jnp.tile(jnp.arange(GRID_W, dtype=F32), rows)
    ang_r = row[:, None] * inv_freq
    ang_c = col[:, None] * inv_freq
    cr, sr, cc, sc = jnp.cos(ang_r), jnp.sin(ang_r), jnp.cos(ang_c), jnp.sin(ang_c)
    cos = jnp.concatenate([cr, cr, cc, cc], axis=-1)
    sin = jnp.concatenate([-sr, sr, -sc, sc], axis=-1)
    cos = jnp.concatenate([cos, jnp.ones((n_ctx, 64), F32)], axis=0)
    sin = jnp.concatenate([sin, jnp.zeros((n_ctx, 64), F32)], axis=0)
    return cos, sin


_ROPE_SWAP = np.concatenate([np.arange(16, 32), np.arange(0, 16), np.arange(48, 64), np.arange(32, 48)])


def _moe_residual(hn_ref, y0_ref, y1_ref, rw_ref, m_ref):
    rw = rw_ref[...]
    y = rw[:, 0:1] * y0_ref[...].astype(F32) + rw[:, 1:2] * y1_ref[...].astype(F32)
    return hn_ref[...] + m_ref[...][5:6] * y


def _in_a_kernel(*refs, q_scale, pending):
    n_src = 5 if pending else 1
    (m_ref, g_ref, win_ref, gq_ref, wuq_ref, gkv_ref, wk_ref, wv_ref, cos_ref, sin_ref) = refs[n_src:n_src + 10]
    q_ref, k_ref, v_ref, pp_ref = refs[n_src + 10:n_src + 14]
    if pending:
        h = _moe_residual(*refs[:5])
        refs[n_src + 14][...] = h
    else:
        h = refs[0][...]
    m = m_ref[...]
    z = _norm_mod(h, g_ref[...], m[0:1], m[1:2])
    p = _dot(z.astype(BF16), win_ref[...])
    cos = cos_ref[...]
    sin = sin_ref[...]

    cqn = _rms(p[:, :A_Q_RANK]) * gq_ref[...]
    qraw = _dot(cqn.astype(BF16), wuq_ref[...])
    for hd in range(A_HEADS):
        c0 = hd * A_QK_PAD
        t = qraw[:, c0 + A_NOPE:c0 + A_QK_PAD]
        rot = t * cos + pltpu.roll(t, 64, 1) * sin
        q_ref[:, c0:c0 + A_NOPE] = (qraw[:, c0:c0 + A_NOPE] * q_scale).astype(BF16)
        q_ref[:, c0 + A_NOPE:c0 + A_QK_PAD] = (rot * q_scale).astype(BF16)

    ckvn = (_rms(p[:, A_Q_RANK:A_Q_RANK + A_KV_RANK]) * gkv_ref[...]).astype(BF16)
    kn = _dot(ckvn, wk_ref[...])
    kt = p[:, 768:896]
    krot = (kt * cos + pltpu.roll(kt, 64, 1) * sin).astype(BF16)
    for hd in range(A_HEADS):
        c0 = hd * A_QK_PAD
        k_ref[:, c0:c0 + A_NOPE] = kn[:, hd * A_NOPE:(hd + 1) * A_NOPE].astype(BF16)
        k_ref[:, c0 + A_NOPE:c0 + A_QK_PAD] = krot
    v_ref[...] = _dot(ckvn, wv_ref[...]).astype(BF16)
    pp_ref[...] = p[:, 896:]


def _stream_specs(rows, src):
    bm = rows.bm
    row = lambda i: (i, 0)
    if not isinstance(src, tuple):
        return [pl.BlockSpec((bm, src.shape[1]), row)], [src], [], []
    hn, y0, y1, rw, mod_prev = src
    d = hn.shape[1]
    specs = [pl.BlockSpec((bm, d), row), pl.BlockSpec((bm, d), row), pl.BlockSpec((bm, d), row),
             pl.BlockSpec((bm, ROUTER_W), row),
             pl.BlockSpec((None, N_MOD, d), lambda i: (rows.mod_index(i), 0, 0))]
    return specs, list(src), [jax.ShapeDtypeStruct(hn.shape, F32)], [pl.BlockSpec((bm, d), row)]


def _in_a(rows, src, mod, g1, w_in, gq, w_uq, gkv, w_ukv, cos2, sin2):
    s_specs, s_args, h_shape, h_spec = _stream_specs(rows, src)
    t, d = s_args[0].shape
    off_rope = A_Q_RANK + A_KV_RANK
    win = jnp.concatenate([w_in[:, :off_rope + A_ROPE], w_in[:, off_rope + _ROPE_SWAP],
                           w_in[:, off_rope + A_ROPE:]], axis=1).astype(BF16)
    wq = w_uq.reshape(A_Q_RANK, A_HEADS, A_NOPE + A_ROPE)
    wuq = jnp.concatenate([wq, wq[:, :, A_NOPE + _ROPE_SWAP]], axis=-1).reshape(A_Q_RANK, A_HEADS * A_QK_PAD)
    wkv = w_ukv.reshape(A_KV_RANK, A_HEADS, A_NOPE + A_V)
    wk = wkv[:, :, :A_NOPE].reshape(A_KV_RANK, A_HEADS * A_NOPE).astype(BF16)
    wv = wkv[:, :, A_NOPE:].reshape(A_KV_RANK, A_HEADS * A_V).astype(BF16)
    n_in = win.shape[1]
    bm = rows.bm
    const = lambda i: (0, 0)
    row = lambda i: (i, 0)
    return pl.pallas_call(
        functools.partial(_in_a_kernel, q_scale=float((A_NOPE + A_ROPE) ** -0.5 * LOG2E),
                          pending=isinstance(src, tuple)),
        out_shape=[jax.ShapeDtypeStruct((t, A_HEADS * A_QK_PAD), BF16),
                   jax.ShapeDtypeStruct((t, A_HEADS * A_QK_PAD), BF16),
                   jax.ShapeDtypeStruct((t, A_HEADS * A_V), BF16),
                   jax.ShapeDtypeStruct((t, B_WIDTH), F32)] + h_shape,
        grid=(rows.all_blocks,),
        in_specs=s_specs + [
                  pl.BlockSpec((None, N_MOD, d), lambda i: (rows.mod_index(i), 0, 0)),
                  pl.BlockSpec((1, d), const),
                  pl.BlockSpec((d, n_in), const),
                  pl.BlockSpec((1, A_Q_RANK), const),
                  pl.BlockSpec((A_Q_RANK, A_HEADS * A_QK_PAD), const),
                  pl.BlockSpec((1, A_KV_RANK), const),
                  pl.BlockSpec((A_KV_RANK, A_HEADS * A_NOPE), const),
                  pl.BlockSpec((A_KV_RANK, A_HEADS * A_V), const),
                  pl.BlockSpec((bm, LANES), lambda i: (rows.pos_index(i), 0)),
                  pl.BlockSpec((bm, LANES), lambda i: (rows.pos_index(i), 0))],
        out_specs=[pl.BlockSpec((bm, A_HEADS * A_QK_PAD), row),
                   pl.BlockSpec((bm, A_HEADS * A_QK_PAD), row),
                   pl.BlockSpec((bm, A_HEADS * A_V), row),
                   pl.BlockSpec((bm, B_WIDTH), row)] + h_spec,
        compiler_params=_cparams(1),
        name="in_proj_a",
    )(*s_args, mod, g1.reshape(1, d), win, gq.reshape(1, -1), wuq.astype(BF16), gkv.reshape(1, -1), wk, wv, cos2, sin2)


def _attn_a_kernel(*refs, n_kv):
    q_ref = refs[0]
    k_refs = refs[1:1 + n_kv]
    v_refs = refs[1 + n_kv:1 + 2 * n_kv]
    o_ref = refs[1 + 2 * n_kv]
    chunks = []
    for k, v in zip(k_refs, v_refs):
        for r0 in range(0, k.shape[0], ATTN_A_KEYS):
            chunks.append((k, v, r0, min(ATTN_A_KEYS, k.shape[0] - r0)))
    bq = min(ATTN_A_ROWS, q_ref.shape[0])

    def scores(rows):
        q = q_ref[rows, :]
        s = []
        top = None
        for k, _, r0, n in chunks:
            si = _dot_t(q, k[r0:r0 + n, :])
            s.append(si)
            for j in range(n // LANES):
                tile = si[:, j * LANES:(j + 1) * LANES]
                top = tile if top is None else jnp.maximum(top, tile)
        return s, top.max(axis=-1, keepdims=True)

    def attend(rows, s, mx):
        o = None
        den = None
        for si, (_, v, r0, n) in zip(s, chunks):
            e = jnp.exp2(si - mx)
            for j in range(n // LANES):
                tile = e[:, j * LANES:(j + 1) * LANES]
                den = tile if den is None else den + tile
            part = _dot(e.astype(BF16), v[r0:r0 + n, :])
            o = part if o is None else o + part
        o_ref[rows, :] = (o / den.sum(axis=-1, keepdims=True)).astype(BF16)

    pair = 2 if q_ref.shape[0] % (2 * bq) == 0 else 1

    def block(i, carry):
        rows = [pl.ds(pl.multiple_of((i * pair + u) * bq, bq), bq) for u in range(pair)]
        staged = [scores(r) for r in rows]
        for r, (s, mx) in zip(rows, staged):
            attend(r, s, mx)
        return carry

    lax.fori_loop(0, q_ref.shape[0] // (pair * bq), block, 0)


def _attn_a(rows, q, k, v, ctx_out):
    n_lat, n_ctx = rows.n_lat, rows.n_ctx
    ctx_blk = rows.t_lat // n_ctx
    lat = pl.pallas_call(
        functools.partial(_attn_a_kernel, n_kv=2),
        out_shape=jax.ShapeDtypeStruct((rows.t_lat, A_HEADS * A_V), BF16),
        grid=(rows.n_b, A_HEADS),
        in_specs=[pl.BlockSpec((n_lat, A_QK_PAD), lambda b, h: (b, h)),
                  pl.BlockSpec((n_lat, A_QK_PAD), lambda b, h: (b, h)),
                  pl.BlockSpec((n_ctx, A_QK_PAD), lambda b, h: (ctx_blk + b, h)),
                  pl.BlockSpec((n_lat, A_V), lambda b, h: (b, h)),
                  pl.BlockSpec((n_ctx, A_V), lambda b, h: (ctx_blk + b, h))],
        out_specs=pl.BlockSpec((n_lat, A_V), lambda b, h: (b, h)),
        compiler_params=_cparams(2),
        name="attn_a_lat",
    )(q, k, k, v, v)
    if not ctx_out:
        return lat, None
    ctx = pl.pallas_call(
        functools.partial(_attn_a_kernel, n_kv=1),
        out_shape=jax.ShapeDtypeStruct((rows.n_b * n_ctx, A_HEADS * A_V), BF16),
        grid=(rows.n_b, A_HEADS),
        in_specs=[pl.BlockSpec((n_ctx, A_QK_PAD), lambda b, h: (ctx_blk + b, h)),
                  pl.BlockSpec((n_ctx, A_QK_PAD), lambda b, h: (ctx_blk + b, h)),
                  pl.BlockSpec((n_ctx, A_V), lambda b, h: (ctx_blk + b, h))],
        out_specs=pl.BlockSpec((n_ctx, A_V), lambda b, h: (b, h)),
        compiler_params=_cparams(2),
        name="attn_a_ctx",
    )(q, k, v)
    return lat, ctx


def _pool_kernel(prev_ref, cur_ref, next_ref, w_ref, s_ref, o_ref, ext_ref, *, rows):
    i = pl.program_id(0)
    bm = rows.bm
    is_ctx = i >= rows.lat_blocks
    n_l = jnp.where(is_ctx, rows.n_ctx, rows.n_lat)
    pos0 = jnp.where(is_ctx, 0, (i % rows.blocks_per_seq) * bm)
    ext_ref[0:POOL_HALO, :] = jnp.where(pos0 > 0, prev_ref[...], 0.0)
    ext_ref[POOL_HALO:POOL_HALO + bm, :] = cur_ref[...]
    ext_ref[POOL_HALO + bm:, :] = jnp.where(pos0 + bm < n_l, next_ref[...], 0.0)
    pos = pos0 + lax.broadcasted_iota(jnp.int32, (bm, 1), 0)
    for gi, w in enumerate(B_WINDOWS):
        cols = slice(gi * B_GROUP_W, (gi + 1) * B_GROUP_W)
        acc = ext_ref[POOL_HALO - w // 2:POOL_HALO - w // 2 + bm, cols]
        for dlt in range(-(w // 2) + 1, w // 2):
            acc = acc + ext_ref[POOL_HALO + dlt:POOL_HALO + dlt + bm, cols]
        cnt = jnp.minimum(pos + (w // 2 - 1), n_l - 1) - jnp.maximum(pos - w // 2, 0) + 1
        dev = acc / cnt.astype(F32) - cur_ref[:, cols]
        y = _dot(dev.astype(BF16), w_ref[gi]) * s_ref[:, cols]
        o_ref[:, cols] = y.astype(BF16)


def _pool(rows, pp, w_pool, pool_scale):
    t = pp.shape[0]
    bm = rows.bm
    assert rows.n_ctx == bm
    per = bm // POOL_HALO
    last = t // POOL_HALO - 1
    return pl.pallas_call(
        functools.partial(_pool_kernel, rows=rows),
        out_shape=jax.ShapeDtypeStruct((t, B_WIDTH), BF16),
        grid=(rows.all_blocks,),
        in_specs=[pl.BlockSpec((POOL_HALO, B_WIDTH), lambda i: (jnp.maximum(i * per - 1, 0), 0)),
                  pl.BlockSpec((bm, B_WIDTH), lambda i: (i, 0)),
                  pl.BlockSpec((POOL_HALO, B_WIDTH), lambda i: (jnp.minimum((i + 1) * per, last), 0)),
                  pl.BlockSpec((len(B_WINDOWS), B_GROUP_W, B_GROUP_W), lambda i: (0, 0, 0)),
                  pl.BlockSpec((1, B_WIDTH), lambda i: (0, 0))],
        out_specs=pl.BlockSpec((bm, B_WIDTH), lambda i: (i, 0)),
        scratch_shapes=[pltpu.VMEM((bm + 2 * POOL_HALO, B_WIDTH), F32)],
        compiler_params=_cparams(1),
        name="pool",
    )(pp, pp, pp, w_pool.astype(BF16), pool_scale.reshape(1, -1))


def _out_kernel(*refs, split, lat_blocks):
    i = pl.program_id(0)
    halves = []
    n = 0
    for is_pair in split:
        if is_pair:
            halves.append(jnp.where(i < lat_blocks, refs[n][...], refs[n + 1][...]))
        else:
            halves.append(refs[n][...])
        n += 2 if is_pair else 1
    (w1_ref, w2_ref, h_ref, m_ref, g_ref, wr1_ref, wr2_ref, br_ref,
     hn_ref, z_ref, ri_ref, rw_ref, cnt_ref, carry_ref) = refs[n:]
    bm = h_ref.shape[0]

    @pl.when(i == 0)
    def _():
        carry_ref[...] = jnp.zeros_like(carry_ref)

    m = m_ref[...]
    o = _dot(halves[0], w1_ref[...]) + _dot(halves[1], w2_ref[...])
    hn = h_ref[...] + m[2:3] * o
    hn_ref[...] = hn
    z = _norm_mod(hn, g_ref[...], m[3:4], m[4:5])
    z_hi = z.astype(BF16)
    z_ref[...] = z_hi
    z_lo = (z - z_hi.astype(F32)).astype(BF16)
    l2 = _dot(z_hi, wr1_ref[...])
    lg = l2[:, :ROUTER_W] + l2[:, ROUTER_W:] + _dot(z_lo, wr2_ref[...]) + br_ref[...]

    lane = lax.broadcasted_iota(jnp.int32, (bm, ROUTER_W), 1)
    low = jnp.float32(-3e38)
    is_g = lane < N_GROUPS
    glog = jnp.where(is_g, lg, low)
    gmax = glog.max(axis=-1, keepdims=True)
    g_idx = jnp.where(glog == gmax, lane, ROUTER_W).min(axis=-1, keepdims=True)
    g_gate = 1.0 / jnp.where(is_g, jnp.exp(lg - gmax), 0.0).sum(axis=-1, keepdims=True)
    lo = N_GROUPS + EXPERTS_PER_GROUP * g_idx
    el = jnp.where((lane >= lo) & (lane < lo + EXPERTS_PER_GROUP), lg, low)
    v1 = el.max(axis=-1, keepdims=True)
    i1 = jnp.where(el == v1, lane, ROUTER_W).min(axis=-1, keepdims=True)
    el2 = jnp.where(lane == i1, low, el)
    v2 = el2.max(axis=-1, keepdims=True)
    i2 = jnp.where(el2 == v2, lane, ROUTER_W).min(axis=-1, keepdims=True)
    e21 = jnp.exp(v2 - v1)
    w1 = g_gate * (1.0 / (1.0 + e21))
    w2 = g_gate * (e21 / (1.0 + e21))

    hit1 = lane == i1
    hit2 = lane == i2
    onehot = jnp.where(hit1 | hit2, 1.0, 0.0)
    r_i = lax.broadcasted_iota(jnp.int32, (bm, bm), 0)
    c_i = lax.broadcasted_iota(jnp.int32, (bm, bm), 1)
    before = _dot(jnp.where(r_i > c_i, 1.0, 0.0).astype(BF16), onehot.astype(BF16)) + carry_ref[0:1, :]
    rank1 = jnp.where(hit1, before, 0.0).sum(axis=-1, keepdims=True).astype(jnp.int32)
    rank2 = jnp.where(hit2, before, 0.0).sum(axis=-1, keepdims=True).astype(jnp.int32)
    total = carry_ref[0:1, :] + onehot.sum(axis=0, keepdims=True)
    carry_ref[...] = jnp.broadcast_to(total, carry_ref.shape)
    cnt_ref[...] = jnp.broadcast_to(total, cnt_ref.shape)

    ri = jnp.where(lane == 0, i1 - N_GROUPS, jnp.where(lane == 1, i2 - N_GROUPS,
                   jnp.where(lane == 2, rank1, jnp.where(lane == 3, rank2, 0))))
    ri_ref[...] = ri.T[:8]
    rw_ref[...] = jnp.where(lane == 0, w1, jnp.where(lane == 1, w2, 0.0))


def _out_proj(rows, n_blocks, halves, w_out, h, mod, g2, wr, br):
    t, d = h.shape
    bm = rows.bm
    t_out = n_blocks * bm
    half = w_out.shape[0] // 2
    w = w_out.astype(BF16)
    wr_hi = wr.astype(BF16)
    wr_lo = (wr - wr_hi.astype(F32)).astype(BF16)
    const = lambda i: (0, 0)
    row = lambda i: (i, 0)
    lat_blocks = rows.lat_blocks
    a_specs, a_args = [], []
    for lat, ctx, col in halves:
        if ctx is None:
            a_specs.append(pl.BlockSpec((bm, half), lambda i, col=col: (i, col)))
            a_args.append(lat)
        else:
            a_specs.append(pl.BlockSpec((bm, half), lambda i, col=col: (jnp.minimum(i, lat_blocks - 1), col)))
            a_specs.append(pl.BlockSpec((bm, half), lambda i, col=col: (jnp.maximum(i - lat_blocks, 0), col)))
            a_args += [lat, ctx]
    return pl.pallas_call(
        functools.partial(_out_kernel, split=tuple(ctx is not None for _, ctx, _ in halves), lat_blocks=lat_blocks),
        out_shape=(jax.ShapeDtypeStruct((t_out, d), F32),
                   jax.ShapeDtypeStruct((t_out, d), BF16),
                   jax.ShapeDtypeStruct((8, t_out), jnp.int32),
                   jax.ShapeDtypeStruct((t_out, ROUTER_W), F32),
                   jax.ShapeDtypeStruct((8, ROUTER_W), F32)),
        grid=(n_blocks,),
        in_specs=a_specs + [
                  pl.BlockSpec((half, d), lambda i: (0, 0)),
                  pl.BlockSpec((half, d), lambda i: (1, 0)),
                  pl.BlockSpec((bm, d), row),
                  pl.BlockSpec((None, N_MOD, d), lambda i: (rows.mod_index(i), 0, 0)),
                  pl.BlockSpec((1, d), const),
                  pl.BlockSpec((d, 2 * ROUTER_W), const),
                  pl.BlockSpec((d, ROUTER_W), const),
                  pl.BlockSpec((1, ROUTER_W), const)],
        out_specs=(pl.BlockSpec((bm, d), row),
                   pl.BlockSpec((bm, d), row),
                   pl.BlockSpec((8, bm), lambda i: (0, i)),
                   pl.BlockSpec((bm, ROUTER_W), row),
                   pl.BlockSpec((8, ROUTER_W), const)),
        scratch_shapes=[pltpu.VMEM((8, ROUTER_W), F32)],
        compiler_params=_cparams(1),
        name="out_proj",
    )(*a_args, w, w, h, mod, g2.reshape(1, d), jnp.concatenate([wr_hi, wr_lo], axis=1), wr_hi, br)


def _in_c_kernel(*refs, q_scale, pending):
    n_src = 5 if pending else 1
    m_ref, g_ref, win_ref, cos_ref, sin_ref = refs[n_src:n_src + 5]
    q_ref, k_ref, v_ref = refs[n_src + 5:n_src + 8]
    if pending:
        h = _moe_residual(*refs[:5])
        refs[n_src + 8][...] = h
    else:
        h = refs[0][...]
    m = m_ref[...]
    z = _norm_mod(h, g_ref[...], m[0:1], m[1:2])
    p = _dot(z.astype(BF16), win_ref[...])
    cos = cos_ref[...]
    sin = sin_ref[...]
    lane = lax.broadcasted_iota(jnp.int32, (1, LANES), 1)
    first = (lane % 32) < 16

    def rope(x):
        partner = jnp.where(first, pltpu.roll(x, LANES - 16, 1), pltpu.roll(x, 16, 1))
        return x * cos + partner * sin

    for tile in range(C_Q_W // LANES):
        cols = slice(tile * LANES, (tile + 1) * LANES)
        q_ref[:, cols] = (rope(p[:, cols]) * q_scale).astype(BF16)
    low_half = lane < C_HEAD_DIM
    for tile in range(C_KV_W // LANES):
        kk = rope(p[:, C_Q_W + tile * LANES:C_Q_W + (tile + 1) * LANES])
        vv = p[:, C_Q_W + C_KV_W + tile * LANES:C_Q_W + C_KV_W + (tile + 1) * LANES]
        ones = jnp.where(lane == C_HEAD_DIM, 1.0, 0.0)
        for half, (kh, vh) in enumerate(((kk, vv), (pltpu.roll(kk, C_HEAD_DIM, 1), pltpu.roll(vv, C_HEAD_DIM, 1)))):
            c0 = (2 * tile + half) * C_KV_PAD
            k_ref[:, c0:c0 + C_KV_PAD] = jnp.where(low_half, kh, 0.0).astype(BF16)
            v_ref[:, c0:c0 + C_KV_PAD] = jnp.where(low_half, vh, ones).astype(BF16)


def _in_c(rows, src, mod, g1, w_in, cos2, sin2):
    s_specs, s_args, h_shape, h_spec = _stream_specs(rows, src)
    t, d = s_args[0].shape
    bm = rows.bm
    n_in = w_in.shape[1]
    const = lambda i: (0, 0)
    row = lambda i: (i, 0)
    return pl.pallas_call(
        functools.partial(_in_c_kernel, q_scale=float(C_HEAD_DIM ** -0.5 * LOG2E), pending=isinstance(src, tuple)),
        out_shape=[jax.ShapeDtypeStruct((t, C_Q_W), BF16),
                   jax.ShapeDtypeStruct((t, C_KV_HEADS * C_KV_PAD), BF16),
                   jax.ShapeDtypeStruct((t, C_KV_HEADS * C_KV_PAD), BF16)] + h_shape,
        grid=(rows.all_blocks,),
        in_specs=s_specs + [
                  pl.BlockSpec((None, N_MOD, d), lambda i: (rows.mod_index(i), 0, 0)),
                  pl.BlockSpec((1, d), const),
                  pl.BlockSpec((d, n_in), const),
                  pl.BlockSpec((bm, LANES), lambda i: (rows.pos_index(i), 0)),
                  pl.BlockSpec((bm, LANES), lambda i: (rows.pos_index(i), 0))],
        out_specs=[pl.BlockSpec((bm, C_Q_W), row),
                   pl.BlockSpec((bm, C_KV_HEADS * C_KV_PAD), row),
                   pl.BlockSpec((bm, C_KV_HEADS * C_KV_PAD), row)] + h_spec,
        compiler_params=_cparams(1),
        name="in_proj_c",
    )(*s_args, mod, g1.reshape(1, d), w_in.astype(BF16), cos2, sin2)


def _sink_attend(q_ref, sink_ref, keys, vals, masks, o_ref, bq):
    def scores(kv):
        q8 = jnp.concatenate([q_ref[:, (kv * C_GROUP + g) * C_HEAD_DIM:(kv * C_GROUP + g + 1) * C_HEAD_DIM]
                              for g in range(C_GROUP)], axis=0)
        return [_dot_t(q8, k[:, kv * C_KV_PAD:kv * C_KV_PAD + C_HEAD_DIM]) for k in keys]

    s_next = scores(0)
    for kv in range(C_KV_HEADS):
        s = s_next
        if kv + 1 < C_KV_HEADS:
            s_next = scores(kv + 1)
        e, mxs = [], []
        for g in range(C_GROUP):
            sg = [si[g * bq:(g + 1) * bq] for si in s]
            sg = [si if msk is None else jnp.where(msk, si, NEG_INF) for si, msk in zip(sg, masks)]
            sink = sink_ref[kv * C_GROUP + g] * LOG2E
            tiles = [si[:, j * LANES:(j + 1) * LANES] for si in sg for j in range(si.shape[1] // LANES)]
            mx = jnp.maximum(sink, functools.reduce(jnp.maximum, tiles).max(axis=-1, keepdims=True))
            e.append([jnp.exp2(si - mx).astype(BF16) for si in sg])
            mxs.append((sink, mx))
        oe = None
        for piece, v in enumerate(vals):
            part = _dot(jnp.concatenate([eg[piece] for eg in e], axis=0), v[:, kv * C_KV_PAD:(kv + 1) * C_KV_PAD])
            oe = part if oe is None else oe + part
        for g, (sink, mx) in enumerate(mxs):
            og = oe[g * bq:(g + 1) * bq]
            den = og[:, C_HEAD_DIM:C_HEAD_DIM + 1] + jnp.exp2(sink - mx)
            c0 = (kv * C_GROUP + g) * C_HEAD_DIM
            o_ref[:, c0:c0 + C_HEAD_DIM] = (og[:, :C_HEAD_DIM] / den).astype(BF16)


def _attn_c_lat_kernel(sink_ref, q_ref, kp_ref, kc_ref, kn_ref, kx_ref, vp_ref, vc_ref, vn_ref, vx_ref, o_ref,
                       *, n_blk):
    n = pl.program_id(1)
    bq = C_WINDOW
    band = 3 * bq
    kband = jnp.concatenate([kp_ref[...], kc_ref[...], kn_ref[...]], axis=0)
    vband = jnp.concatenate([vp_ref[...], vc_ref[...], vn_ref[...]], axis=0)
    qi = lax.broadcasted_iota(jnp.int32, (bq, band), 0)
    kj = lax.broadcasted_iota(jnp.int32, (bq, band), 1)
    rel = qi - (kj - bq)
    ok = (jnp.abs(rel) <= C_WINDOW) & ((kj >= bq) | (n > 0)) & ((kj < 2 * bq) | (n < n_blk - 1))
    _sink_attend(q_ref, sink_ref, [kband, kx_ref[...]], [vband, vx_ref[...]], [ok, None], o_ref, bq)


def _attn_c_ctx_kernel(sink_ref, q_ref, kx_ref, vx_ref, o_ref):
    _sink_attend(q_ref, sink_ref, [kx_ref[...]], [vx_ref[...]], [None], o_ref, q_ref.shape[0])


def _attn_c(rows, q, k, v, sink, ctx_out):
    t = q.shape[0]
    bq = C_WINDOW
    n_blk = rows.n_lat // bq
    n_ctx = rows.n_ctx
    ctx_blk = rows.t_lat // n_ctx
    kvw = C_KV_HEADS * C_KV_PAD
    del t
    smem = pl.BlockSpec(memory_space=pltpu.SMEM)
    prev = lambda b, n: (b * n_blk + jnp.maximum(n - 1, 0), 0)
    cur = lambda b, n: (b * n_blk + n, 0)
    nxt = lambda b, n: (b * n_blk + jnp.minimum(n + 1, n_blk - 1), 0)
    cx = lambda b, n: (ctx_blk + b, 0)
    kv_specs = [pl.BlockSpec((bq, kvw), prev), pl.BlockSpec((bq, kvw), cur),
                pl.BlockSpec((bq, kvw), nxt), pl.BlockSpec((n_ctx, kvw), cx)]
    lat = pl.pallas_call(
        functools.partial(_attn_c_lat_kernel, n_blk=n_blk),
        out_shape=jax.ShapeDtypeStruct((rows.t_lat, C_Q_W), BF16),
        grid=(rows.n_b, n_blk),
        in_specs=[smem, pl.BlockSpec((bq, C_Q_W), cur)] + kv_specs + kv_specs,
        out_specs=pl.BlockSpec((bq, C_Q_W), cur),
        compiler_params=_cparams(2),
        name="attn_c_lat",
    )(sink, q, k, k, k, k, v, v, v, v)
    if not ctx_out:
        return lat, None
    cxb = lambda b: (ctx_blk + b, 0)
    ctx = pl.pallas_call(
        _attn_c_ctx_kernel,
        out_shape=jax.ShapeDtypeStruct((rows.n_b * n_ctx, C_Q_W), BF16),
        grid=(rows.n_b,),
        in_specs=[smem, pl.BlockSpec((n_ctx, C_Q_W), cxb), pl.BlockSpec((n_ctx, kvw), cxb),
                  pl.BlockSpec((n_ctx, kvw), cxb)],
        out_specs=pl.BlockSpec((n_ctx, C_Q_W), lambda b: (b, 0)),
        compiler_params=_cparams(1),
        name="attn_c_ctx",
    )(sink, q, k, v)
    return lat, ctx


def _moe_kernel(be_ref, ne_ref, nu_ref, x_ref, wg_hbm, wu_hbm, wd_hbm, y_ref,
                wg_f, wu_f, wd_f, wgu_s, wd_s, sem, *, layer):
    i = pl.program_id(0)
    used = i < nu_ref[0]
    expert = be_ref[i]
    fresh = (i == 0) | (expert != be_ref[jnp.maximum(i - 1, 0)])

    def weight_copies(e):
        return (pltpu.make_async_copy(wg_hbm.at[layer, e], wg_f, sem.at[0]),
                pltpu.make_async_copy(wu_hbm.at[layer, e], wu_f, sem.at[1]),
                pltpu.make_async_copy(wd_hbm.at[layer, e], wd_f, sem.at[2]))

    @pl.when(used & (i == 0))
    def _():
        for c in weight_copies(expert):
            c.start()

    @pl.when(used & fresh)
    def _():
        for c in weight_copies(expert):
            c.wait()
        wgu_s[:, :D_EXPERT] = wg_f[...].astype(BF16)
        wgu_s[:, D_EXPERT:] = wu_f[...].astype(BF16)
        wd_s[...] = wd_f[...].astype(BF16)
        nxt = ne_ref[i]

        @pl.when(nxt >= 0)
        def _():
            for c in weight_copies(nxt):
                c.start()

    @pl.when(used)
    def _():
        gu = _dot(x_ref[...], wgu_s[...])
        g = gu[:, :D_EXPERT]
        a = (g * jax.nn.sigmoid(g)) * gu[:, D_EXPERT:]
        y_ref[...] = _dot(a.astype(BF16), wd_s[...]).astype(BF16)

    @pl.when(jnp.logical_not(used))
    def _():
        y_ref[...] = jnp.zeros_like(y_ref)


def _moe_blocks(layer, xs, block_e, next_e, n_used, w_gate, w_up, w_down):
    r, d = xs.shape
    bm = MOE_ROWS
    nb = r // bm
    hbm = pl.BlockSpec(memory_space=pl.ANY)
    return pl.pallas_call(
        functools.partial(_moe_kernel, layer=layer),
        out_shape=jax.ShapeDtypeStruct((r, d), BF16),
        grid_spec=pltpu.PrefetchScalarGridSpec(
            num_scalar_prefetch=3,
            grid=(nb,),
            in_specs=[pl.BlockSpec((bm, d), lambda i, be, ne, nu: (jnp.minimum(i, nu[0] - 1), 0)), hbm, hbm, hbm],
            out_specs=pl.BlockSpec((bm, d), lambda i, be, ne, nu: (i, 0)),
            scratch_shapes=[pltpu.VMEM((d, D_EXPERT), F32), pltpu.VMEM((d, D_EXPERT), F32),
                            pltpu.VMEM((D_EXPERT, d), F32),
                            pltpu.VMEM((d, 2 * D_EXPERT), BF16), pltpu.VMEM((D_EXPERT, d), BF16),
                            pltpu.SemaphoreType.DMA((3,))]),
        compiler_params=_cparams(1),
        name="moe_experts",
    )(block_e, next_e, n_used, xs, w_gate, w_up, w_down)


def _plan(ri, cnt, bm):
    n_t = ri.shape[1]
    n_tk = n_t * TOP_K
    nb = -(-(n_tk + N_EXPERTS * (bm - 1)) // bm)
    counts = cnt[0, N_GROUPS:N_GROUPS + N_EXPERTS].astype(jnp.int32)
    padded = (counts + bm - 1) // bm * bm
    pends = jnp.cumsum(padded)
    is_e = ri[None, :TOP_K] == jnp.arange(N_EXPERTS, dtype=jnp.int32)[:, None, None]
    dest = jnp.sum(jnp.where(is_e, (pends - padded)[:, None, None], 0), axis=0) + ri[TOP_K:2 * TOP_K]
    row_tok = (jnp.arange(nb * bm, dtype=jnp.int32) % n_t).at[dest.reshape(-1)].set(
        jnp.arange(n_tk, dtype=jnp.int32) % n_t, unique_indices=True, mode='promise_in_bounds')
    n_used = (pends[-1] // bm).astype(jnp.int32)
    blk = jnp.arange(nb, dtype=jnp.int32)
    block_e = jnp.sum((blk[:, None] * bm >= pends[None, :]).astype(jnp.int32), axis=1)
    block_e = jnp.minimum(block_e, N_EXPERTS - 1)
    block_e = jnp.where(blk < n_used, block_e, block_e[n_used - 1])
    later = (block_e[None, :] > block_e[:, None]) & (blk[None, :] < n_used)
    next_e = jnp.min(jnp.where(later, block_e[None, :], N_EXPERTS), axis=1)
    next_e = jnp.where(next_e < N_EXPERTS, next_e, -1).astype(jnp.int32)
    return row_tok, block_e, next_e, n_used.reshape(1), dest


def _final_kernel(h_ref, y0_ref, y1_ref, rw_ref, m_ref, g_ref, o_ref):
    o_ref[...] = _rms(_moe_residual(h_ref, y0_ref, y1_ref, rw_ref, m_ref)) * g_ref[...]


def _final(rows, n_blocks, h, y0, y1, rw, mod, final_g):
    d = h.shape[1]
    bm = rows.bm
    row = lambda i: (i, 0)
    return pl.pallas_call(
        _final_kernel,
        out_shape=jax.ShapeDtypeStruct((n_blocks * bm, d), F32),
        grid=(n_blocks,),
        in_specs=[pl.BlockSpec((bm, d), row), pl.BlockSpec((bm, d), row), pl.BlockSpec((bm, d), row),
                  pl.BlockSpec((bm, ROUTER_W), row),
                  pl.BlockSpec((None, N_MOD, d), lambda i: (rows.mod_index(i), 0, 0)),
                  pl.BlockSpec((1, d), lambda i: (0, 0))],
        out_specs=pl.BlockSpec((bm, d), row),
        compiler_params=_cparams(1),
        name="final",
    )(h, y0, y1, rw, mod, final_g.reshape(1, d))


def kernel(x, c, ctx, c_ctx, mod_w, mod_b, norm1_g, norm2_g, final_g, a_w_in, a_q_norm_g, a_kv_norm_g, a_w_uq,
           a_w_ukv, a_w_pool, a_pool_scale, a_w_out, c_w_in, c_sink, c_w_out, r_w_group, r_b_group, r_w_expert,
           r_b_expert, e_w_gate, e_w_up, e_w_down):
    n_b, n_lat, d = x.shape
    n_ctx = ctx.shape[1]
    rows = _Rows(n_b, n_lat, n_ctx, ROW_BLOCK)
    in_rows = _Rows(n_b, n_lat, n_ctx, IN_ROWS)
    pool_rows = _Rows(n_b, n_lat, n_ctx, POOL_ROWS)
    depth = mod_w.shape[0]

    cvec = jnp.zeros((8, d), F32).at[:n_b].set(c).at[n_b].set(c_ctx)
    mods = _modulation(cvec, mod_w, mod_b).reshape(depth, 8, N_MOD, d)

    cos, sin = _rope_tables(n_lat, ROW_BLOCK)
    zeros = jnp.zeros_like(cos)
    cos_a, sin_a = jnp.concatenate([cos, zeros], axis=1), jnp.concatenate([sin, zeros], axis=1)
    cos_c, sin_c = jnp.concatenate([cos, cos], axis=1), jnp.concatenate([sin, sin], axis=1)

    src = jnp.concatenate([x.reshape(-1, d), ctx.reshape(-1, d)], axis=0)
    for i in range(depth):
        ctx_out = i < depth - 1
        j = i // 2
        mod = mods[i]
        if i % 2 == 0:
            q, k, v, pp, *h = _in_a(in_rows, src, mod, norm1_g[i], a_w_in[j], a_q_norm_g[j], a_w_uq[j],
                                    a_kv_norm_g[j], a_w_ukv[j], cos_a, sin_a)
            a_lat, a_ctx = _attn_a(rows, q, k, v, ctx_out)
            halves = [(a_lat, a_ctx, 0), (_pool(pool_rows, pp, a_w_pool[j], a_pool_scale[j]), None, 0)]
            w_out = a_w_out[j]
        else:
            q, k, v, *h = _in_c(in_rows, src, mod, norm1_g[i], c_w_in[j], cos_c, sin_c)
            a_lat, a_ctx = _attn_c(rows, q, k, v, c_sink[j], ctx_out)
            halves = [(a_lat, a_ctx, 0), (a_lat, a_ctx, 1)]
            w_out = c_w_out[j]
        h = h[0] if h else src
        n_blocks = rows.all_blocks if ctx_out else rows.lat_blocks
        wr = jnp.zeros((d, ROUTER_W), F32).at[:, :N_GROUPS].set(r_w_group[i])
        wr = wr.at[:, N_GROUPS:N_GROUPS + N_EXPERTS].set(r_w_expert[i])
        br = jnp.zeros((1, ROUTER_W), F32).at[0, :N_GROUPS].set(r_b_group[i])
        br = br.at[0, N_GROUPS:N_GROUPS + N_EXPERTS].set(r_b_expert[i])
        hn, z, ri, rw, cnt = _out_proj(rows, n_blocks, halves, w_out, h, mod, norm2_g[i], wr, br)

        row_tok, block_e, next_e, n_used, dest = _plan(ri, cnt, MOE_ROWS)
        xs = z.at[row_tok].get(mode='promise_in_bounds')
        y = _moe_blocks(i, xs, block_e, next_e, n_used, e_w_gate, e_w_up, e_w_down)
        y0 = y.at[dest[0]].get(mode='promise_in_bounds')
        y1 = y.at[dest[1]].get(mode='promise_in_bounds')
        src = (hn, y0, y1, rw, mod)
    return _final(rows, rows.lat_blocks, *src, final_g).reshape(n_b, n_lat, d)
```

```python
import functools

import numpy as np
import jax
import jax.numpy as jnp
from jax import lax
from jax.experimental import pallas as pl
from jax.experimental.pallas import tpu as pltpu

F32 = jnp.float32
BF16 = jnp.bfloat16

D_MODEL = 2048
DEPTH = 4
GRID_W = 64
EPS = 1e-6
ROPE_BASE = 10000.0
NEG_INF = -1e30
N_MOD = 6

A_NOPE = 128
A_ROPE = 64
A_V = 128
A_HEADS = 8
A_Q_RANK = 512
A_KV_RANK = 256
A_QK_PAD = 256
B_WINDOWS = (2, 4, 8, 16)
B_GROUP_W = 256
B_WIDTH = 1024
POOL_HALO = 8

C_HEAD_DIM = 64
C_HEADS = 32
C_KV_HEADS = 4
C_GROUP = 8
C_WINDOW = 128
C_Q_W = C_HEADS * C_HEAD_DIM
C_KV_W = C_KV_HEADS * C_HEAD_DIM

N_GROUPS = 4
EXPERTS_PER_GROUP = 8
N_EXPERTS = 32
TOP_K = 2
D_EXPERT = 512
ROUTER_W = 128

ROW_BLOCK = 512
IN_ROWS = 256
POOL_ROWS = 256
MOE_ROWS = 256
LANES = 128
VMEM_LIMIT = 56 * 1024 * 1024
LOG2E = 1.4426950408889634
ATTN_A_ROWS = 256
ATTN_A_GROUP = 4
ATTN_A_KEYS = 512
C_KV_PAD = 128


def _cparams(n_axes):
    return pltpu.CompilerParams(dimension_semantics=("arbitrary",) * n_axes,
                                vmem_limit_bytes=VMEM_LIMIT)


def _dot(a, b):
    return jnp.dot(a, b, preferred_element_type=F32)


def _dot_t(a, b):
    return lax.dot_general(a, b, (((1,), (1,)), ((), ())), preferred_element_type=F32)


def _rms(x):
    return x * lax.rsqrt(jnp.mean(x * x, axis=-1, keepdims=True) + EPS)


def _norm_mod(h, g, shift, scale):
    return (_rms(h) * g) * (1 + scale) + shift


def _mod_kernel(s_ref, w_ref, b_ref, o_ref):
    s = s_ref[...]
    s = s * jax.nn.sigmoid(s)
    o_ref[...] = _dot(s.astype(BF16), w_ref[...].astype(BF16)) + b_ref[...]


def _modulation(cvec, mod_w, mod_b):
    depth, d, n = mod_w.shape
    tn = 1024
    return pl.pallas_call(
        _mod_kernel,
        out_shape=jax.ShapeDtypeStruct((depth, 8, n), F32),
        grid=(depth, n // tn),
        in_specs=[pl.BlockSpec((8, d), lambda l, j: (0, 0)),
                  pl.BlockSpec((None, d, tn), lambda l, j: (l, 0, j)),
                  pl.BlockSpec((None, 1, tn), lambda l, j: (l, 0, j))],
        out_specs=pl.BlockSpec((None, 8, tn), lambda l, j: (l, 0, j)),
        compiler_params=_cparams(2),
        name="modulation",
    )(cvec, mod_w, mod_b.reshape(depth, 1, n))


class _Rows:
    def __init__(self, n_b, n_lat, n_ctx, bm):
        self.n_b, self.n_lat, self.n_ctx, self.bm = n_b, n_lat, n_ctx, bm
        self.t_lat = n_b * n_lat
        self.t_all = self.t_lat + n_b * n_ctx
        assert n_lat % bm == 0 and (n_b * n_ctx) % bm == 0
        self.lat_blocks = self.t_lat // bm
        self.all_blocks = self.t_all // bm
        self.blocks_per_seq = n_lat // bm

    def mod_index(self, i):
        return jnp.minimum(i // self.blocks_per_seq, self.n_b)

    def pos_index(self, i):
        return jnp.where(i < self.lat_blocks, i % self.blocks_per_seq, self.blocks_per_seq)


def _rope_tables(n_lat, n_ctx):
    axis_dim = A_ROPE // 2
    inv_freq = ROPE_BASE ** (-jnp.arange(axis_dim // 2, dtype=F32) * 2.0 / axis_dim)
    rows = n_lat // GRID_W
    row = jnp.repeat(jnp.arange(rows, dtype=F32), GRID_W)
    col = jnp.tile(jnp.arange(GRID_W, dtype=F32), rows)
    ang_r = row[:, None] * inv_freq
    ang_c = col[:, None] * inv_freq
    cr, sr, cc, sc = jnp.cos(ang_r), jnp.sin(ang_r), jnp.cos(ang_c), jnp.sin(ang_c)
    cos = jnp.concatenate([cr, cr, cc, cc], axis=-1)
    sin = jnp.concatenate([-sr, sr, -sc, sc], axis=-1)
    cos = jnp.concatenate([cos, jnp.ones((n_ctx, 64), F32)], axis=0)
    sin = jnp.concatenate([sin, jnp.zeros((n_ctx, 64), F32)], axis=0)
    return cos, sin


_ROPE_SWAP = np.concatenate([np.arange(16, 32), np.arange(0, 16), np.arange(48, 64), np.arange(32, 48)])


def _moe_residual(hn_ref, y0_ref, y1_ref, rw_ref, m_ref):
    rw = rw_ref[...]
    y = rw[:, 0:1] * y0_ref[...].astype(F32) + rw[:, 1:2] * y1_ref[...].astype(F32)
    return hn_ref[...] + m_ref[...][5:6] * y


def _in_a_kernel(*refs, q_scale, pending):
    n_src = 5 if pending else 1
    (m_ref, g_ref, win_ref, gq_ref, wuq_ref, gkv_ref, wk_ref, wv_ref, cos_ref, sin_ref) = refs[n_src:n_src + 10]
    q_ref, k_ref, v_ref, pp_ref = refs[n_src + 10:n_src + 14]
    if pending:
        h = _moe_residual(*refs[:5])
        refs[n_src + 14][...] = h
    else:
        h = refs[0][...]
    m = m_ref[...]
    z = _norm_mod(h, g_ref[...], m[0:1], m[1:2])
    p = _dot(z.astype(BF16), win_ref[...])
    cos = cos_ref[...]
    sin = sin_ref[...]

    cqn = _rms(p[:, :A_Q_RANK]) * gq_ref[...]
    qraw = _dot(cqn.astype(BF16), wuq_ref[...])
    for hd in range(A_HEADS):
        c0 = hd * A_QK_PAD
        t = qraw[:, c0 + A_NOPE:c0 + A_QK_PAD]
        rot = t * cos + pltpu.roll(t, 64, 1) * sin
        q_ref[:, c0:c0 + A_NOPE] = (qraw[:, c0:c0 + A_NOPE] * q_scale).astype(BF16)
        q_ref[:, c0 + A_NOPE:c0 + A_QK_PAD] = (rot * q_scale).astype(BF16)

    ckvn = (_rms(p[:, A_Q_RANK:A_Q_RANK + A_KV_RANK]) * gkv_ref[...]).astype(BF16)
    kn = _dot(ckvn, wk_ref[...])
    kt = p[:, 768:896]
    krot = (kt * cos + pltpu.roll(kt, 64, 1) * sin).astype(BF16)
    for hd in range(A_HEADS):
        c0 = hd * A_QK_PAD
        k_ref[:, c0:c0 + A_NOPE] = kn[:, hd * A_NOPE:(hd + 1) * A_NOPE].astype(BF16)
        k_ref[:, c0 + A_NOPE:c0 + A_QK_PAD] = krot
    v_ref[...] = _dot(ckvn, wv_ref[...]).astype(BF16)
    pp_ref[...] = p[:, 896:]


def _stream_specs(rows, src):
    bm = rows.bm
    row = lambda i: (i, 0)
    if not isinstance(src, tuple):
        return [pl.BlockSpec((bm, src.shape[1]), row)], [src], [], []
    hn, y0, y1, rw, mod_prev = src
    d = hn.shape[1]
    specs = [pl.BlockSpec((bm, d), row), pl.BlockSpec((bm, d), row), pl.BlockSpec((bm, d), row),
             pl.BlockSpec((bm, ROUTER_W), row),
             pl.BlockSpec((None, N_MOD, d), lambda i: (rows.mod_index(i), 0, 0))]
    return specs, list(src), [jax.ShapeDtypeStruct(hn.shape, F32)], [pl.BlockSpec((bm, d), row)]


def _in_a(rows, src, mod, g1, w_in, gq, w_uq, gkv, w_ukv, cos2, sin2):
    s_specs, s_args, h_shape, h_spec = _stream_specs(rows, src)
    t, d = s_args[0].shape
    off_rope = A_Q_RANK + A_KV_RANK
    win = jnp.concatenate([w_in[:, :off_rope + A_ROPE], w_in[:, off_rope + _ROPE_SWAP],
                           w_in[:, off_rope + A_ROPE:]], axis=1).astype(BF16)
    wq = w_uq.reshape(A_Q_RANK, A_HEADS, A_NOPE + A_ROPE)
    wuq = jnp.concatenate([wq, wq[:, :, A_NOPE + _ROPE_SWAP]], axis=-1).reshape(A_Q_RANK, A_HEADS * A_QK_PAD)
    wkv = w_ukv.reshape(A_KV_RANK, A_HEADS, A_NOPE + A_V)
    wk = wkv[:, :, :A_NOPE].reshape(A_KV_RANK, A_HEADS * A_NOPE).astype(BF16)
    wv = wkv[:, :, A_NOPE:].reshape(A_KV_RANK, A_HEADS * A_V).astype(BF16)
    n_in = win.shape[1]
    bm = rows.bm
    const = lambda i: (0, 0)
    row = lambda i: (i, 0)
    return pl.pallas_call(
        functools.partial(_in_a_kernel, q_scale=float((A_NOPE + A_ROPE) ** -0.5 * LOG2E),
                          pending=isinstance(src, tuple)),
        out_shape=[jax.ShapeDtypeStruct((t, A_HEADS * A_QK_PAD), BF16),
                   jax.ShapeDtypeStruct((t, A_HEADS * A_QK_PAD), BF16),
                   jax.ShapeDtypeStruct((t, A_HEADS * A_V), BF16),
                   jax.ShapeDtypeStruct((t, B_WIDTH), F32)] + h_shape,
        grid=(rows.all_blocks,),
        in_specs=s_specs + [
                  pl.BlockSpec((None, N_MOD, d), lambda i: (rows.mod_index(i), 0, 0)),
                  pl.BlockSpec((1, d), const),
                  pl.BlockSpec((d, n_in), const),
                  pl.BlockSpec((1, A_Q_RANK), const),
                  pl.BlockSpec((A_Q_RANK, A_HEADS * A_QK_PAD), const),
                  pl.BlockSpec((1, A_KV_RANK), const),
                  pl.BlockSpec((A_KV_RANK, A_HEADS * A_NOPE), const),
                  pl.BlockSpec((A_KV_RANK, A_HEADS * A_V), const),
                  pl.BlockSpec((bm, LANES), lambda i: (rows.pos_index(i), 0)),
                  pl.BlockSpec((bm, LANES), lambda i: (rows.pos_index(i), 0))],
        out_specs=[pl.BlockSpec((bm, A_HEADS * A_QK_PAD), row),
                   pl.BlockSpec((bm, A_HEADS * A_QK_PAD), row),
                   pl.BlockSpec((bm, A_HEADS * A_V), row),
                   pl.BlockSpec((bm, B_WIDTH), row)] + h_spec,
        compiler_params=_cparams(1),
        name="in_proj_a",
    )(*s_args, mod, g1.reshape(1, d), win, gq.reshape(1, -1), wuq.astype(BF16), gkv.reshape(1, -1), wk, wv, cos2, sin2)


def _attn_a_kernel(*refs, n_kv):
    q_ref = refs[0]
    k_refs = refs[1:1 + n_kv]
    v_refs = refs[1 + n_kv:1 + 2 * n_kv]
    o_ref = refs[1 + 2 * n_kv]
    chunks = []
    for k, v in zip(k_refs, v_refs):
        for r0 in range(0, k.shape[0], ATTN_A_KEYS):
            chunks.append((k, v, r0, min(ATTN_A_KEYS, k.shape[0] - r0)))
    bq = min(ATTN_A_ROWS, q_ref.shape[0])

    def scores(rows):
        q = q_ref[rows, :]
        s = []
        top = None
        for k, _, r0, n in chunks:
            si = _dot_t(q, k[r0:r0 + n, :])
            s.append(si)
            for j in range(n // LANES):
                tile = si[:, j * LANES:(j + 1) * LANES]
                top = tile if top is None else jnp.maximum(top, tile)
        return s, top.max(axis=-1, keepdims=True)

    def attend(rows, s, mx):
        o = None
        den = None
        for si, (_, v, r0, n) in zip(s, chunks):
            e = jnp.exp2(si - mx)
            for j in range(n // LANES):
                tile = e[:, j * LANES:(j + 1) * LANES]
                den = tile if den is None else den + tile
            part = _dot(e.astype(BF16), v[r0:r0 + n, :])
            o = part if o is None else o + part
        o_ref[rows, :] = (o / den.sum(axis=-1, keepdims=True)).astype(BF16)

    pair = ATTN_A_GROUP if q_ref.shape[0] % (ATTN_A_GROUP * bq) == 0 else 1

    def block(i, carry):
        rows = [pl.ds(pl.multiple_of((i * pair + u) * bq, bq), bq) for u in range(pair)]
        staged = [scores(r) for r in rows]
        for r, (s, mx) in zip(rows, staged):
            attend(r, s, mx)
        return carry

    lax.fori_loop(0, q_ref.shape[0] // (pair * bq), block, 0)


def _attn_a(rows, q, k, v, ctx_out):
    n_lat, n_ctx = rows.n_lat, rows.n_ctx
    ctx_blk = rows.t_lat // n_ctx
    lat = pl.pallas_call(
        functools.partial(_attn_a_kernel, n_kv=2),
        out_shape=jax.ShapeDtypeStruct((rows.t_lat, A_HEADS * A_V), BF16),
        grid=(rows.n_b, A_HEADS),
        in_specs=[pl.BlockSpec((n_lat, A_QK_PAD), lambda b, h: (b, h)),
                  pl.BlockSpec((n_lat, A_QK_PAD), lambda b, h: (b, h)),
                  pl.BlockSpec((n_ctx, A_QK_PAD), lambda b, h: (ctx_blk + b, h)),
                  pl.BlockSpec((n_lat, A_V), lambda b, h: (b, h)),
                  pl.BlockSpec((n_ctx, A_V), lambda b, h: (ctx_blk + b, h))],
        out_specs=pl.BlockSpec((n_lat, A_V), lambda b, h: (b, h)),
        compiler_params=_cparams(2),
        name="attn_a_lat",
    )(q, k, k, v, v)
    if not ctx_out:
        return lat, None
    ctx = pl.pallas_call(
        functools.partial(_attn_a_kernel, n_kv=1),
        out_shape=jax.ShapeDtypeStruct((rows.n_b * n_ctx, A_HEADS * A_V), BF16),
        grid=(rows.n_b, A_HEADS),
        in_specs=[pl.BlockSpec((n_ctx, A_QK_PAD), lambda b, h: (ctx_blk + b, h)),
                  pl.BlockSpec((n_ctx, A_QK_PAD), lambda b, h: (ctx_blk + b, h)),
                  pl.BlockSpec((n_ctx, A_V), lambda b, h: (ctx_blk + b, h))],
        out_specs=pl.BlockSpec((n_ctx, A_V), lambda b, h: (b, h)),
        compiler_params=_cparams(2),
        name="attn_a_ctx",
    )(q, k, v)
    return lat, ctx


def _pool_kernel(prev_ref, cur_ref, next_ref, w_ref, s_ref, o_ref, ext_ref, *, rows):
    i = pl.program_id(0)
    bm = rows.bm
    is_ctx = i >= rows.lat_blocks
    n_l = jnp.where(is_ctx, rows.n_ctx, rows.n_lat)
    pos0 = jnp.where(is_ctx, 0, (i % rows.blocks_per_seq) * bm)
    ext_ref[0:POOL_HALO, :] = jnp.where(pos0 > 0, prev_ref[...], 0.0)
    ext_ref[POOL_HALO:POOL_HALO + bm, :] = cur_ref[...]
    ext_ref[POOL_HALO + bm:, :] = jnp.where(pos0 + bm < n_l, next_ref[...], 0.0)
    pos = pos0 + lax.broadcasted_iota(jnp.int32, (bm, 1), 0)
    for gi, w in enumerate(B_WINDOWS):
        cols = slice(gi * B_GROUP_W, (gi + 1) * B_GROUP_W)
        acc = ext_ref[POOL_HALO - w // 2:POOL_HALO - w // 2 + bm, cols]
        for dlt in range(-(w // 2) + 1, w // 2):
            acc = acc + ext_ref[POOL_HALO + dlt:POOL_HALO + dlt + bm, cols]
        cnt = jnp.minimum(pos + (w // 2 - 1), n_l - 1) - jnp.maximum(pos - w // 2, 0) + 1
        dev = acc / cnt.astype(F32) - cur_ref[:, cols]
        y = _dot(dev.astype(BF16), w_ref[gi]) * s_ref[:, cols]
        o_ref[:, cols] = y.astype(BF16)


def _pool(rows, pp, w_pool, pool_scale):
    t = pp.shape[0]
    bm = rows.bm
    assert rows.n_ctx == bm
    per = bm // POOL_HALO
    last = t // POOL_HALO - 1
    return pl.pallas_call(
        functools.partial(_pool_kernel, rows=rows),
        out_shape=jax.ShapeDtypeStruct((t, B_WIDTH), BF16),
        grid=(rows.all_blocks,),
        in_specs=[pl.BlockSpec((POOL_HALO, B_WIDTH), lambda i: (jnp.maximum(i * per - 1, 0), 0)),
                  pl.BlockSpec((bm, B_WIDTH), lambda i: (i, 0)),
                  pl.BlockSpec((POOL_HALO, B_WIDTH), lambda i: (jnp.minimum((i + 1) * per, last), 0)),
                  pl.BlockSpec((len(B_WINDOWS), B_GROUP_W, B_GROUP_W), lambda i: (0, 0, 0)),
                  pl.BlockSpec((1, B_WIDTH), lambda i: (0, 0))],
        out_specs=pl.BlockSpec((bm, B_WIDTH), lambda i: (i, 0)),
        scratch_shapes=[pltpu.VMEM((bm + 2 * POOL_HALO, B_WIDTH), F32)],
        compiler_params=_cparams(1),
        name="pool",
    )(pp, pp, pp, w_pool.astype(BF16), pool_scale.reshape(1, -1))


def _out_kernel(*refs, split, lat_blocks):
    i = pl.program_id(0)
    halves = []
    n = 0
    for is_pair in split:
        if is_pair:
            halves.append(jnp.where(i < lat_blocks, refs[n][...], refs[n + 1][...]))
        else:
            halves.append(refs[n][...])
        n += 2 if is_pair else 1
    (w1_ref, w2_ref, h_ref, m_ref, g_ref, wr1_ref, wr2_ref, br_ref,
     hn_ref, z_ref, ri_ref, rw_ref, cnt_ref, carry_ref) = refs[n:]
    bm = h_ref.shape[0]

    @pl.when(i == 0)
    def _():
        carry_ref[...] = jnp.zeros_like(carry_ref)

    m = m_ref[...]
    o = _dot(halves[0], w1_ref[...]) + _dot(halves[1], w2_ref[...])
    hn = h_ref[...] + m[2:3] * o
    hn_ref[...] = hn
    z = _norm_mod(hn, g_ref[...], m[3:4], m[4:5])
    z_hi = z.astype(BF16)
    z_ref[...] = z_hi
    z_lo = (z - z_hi.astype(F32)).astype(BF16)
    l2 = _dot(z_hi, wr1_ref[...])
    lg = l2[:, :ROUTER_W] + l2[:, ROUTER_W:] + _dot(z_lo, wr2_ref[...]) + br_ref[...]

    lane = lax.broadcasted_iota(jnp.int32, (bm, ROUTER_W), 1)
    low = jnp.float32(-3e38)
    is_g = lane < N_GROUPS
    glog = jnp.where(is_g, lg, low)
    gmax = glog.max(axis=-1, keepdims=True)
    g_idx = jnp.where(glog == gmax, lane, ROUTER_W).min(axis=-1, keepdims=True)
    g_gate = 1.0 / jnp.where(is_g, jnp.exp(lg - gmax), 0.0).sum(axis=-1, keepdims=True)
    lo = N_GROUPS + EXPERTS_PER_GROUP * g_idx
    el = jnp.where((lane >= lo) & (lane < lo + EXPERTS_PER_GROUP), lg, low)
    v1 = el.max(axis=-1, keepdims=True)
    i1 = jnp.where(el == v1, lane, ROUTER_W).min(axis=-1, keepdims=True)
    el2 = jnp.where(lane == i1, low, el)
    v2 = el2.max(axis=-1, keepdims=True)
    i2 = jnp.where(el2 == v2, lane, ROUTER_W).min(axis=-1, keepdims=True)
    e21 = jnp.exp(v2 - v1)
    w1 = g_gate * (1.0 / (1.0 + e21))
    w2 = g_gate * (e21 / (1.0 + e21))

    hit1 = lane == i1
    hit2 = lane == i2
    onehot = jnp.where(hit1 | hit2, 1.0, 0.0)
    r_i = lax.broadcasted_iota(jnp.int32, (bm, bm), 0)
    c_i = lax.broadcasted_iota(jnp.int32, (bm, bm), 1)
    before = _dot(jnp.where(r_i > c_i, 1.0, 0.0).astype(BF16), onehot.astype(BF16)) + carry_ref[0:1, :]
    rank1 = jnp.where(hit1, before, 0.0).sum(axis=-1, keepdims=True).astype(jnp.int32)
    rank2 = jnp.where(hit2, before, 0.0).sum(axis=-1, keepdims=True).astype(jnp.int32)
    total = carry_ref[0:1, :] + onehot.sum(axis=0, keepdims=True)
    carry_ref[...] = jnp.broadcast_to(total, carry_ref.shape)
    cnt_ref[...] = jnp.broadcast_to(total, cnt_ref.shape)

    ri = jnp.where(lane == 0, i1 - N_GROUPS, jnp.where(lane == 1, i2 - N_GROUPS,
                   jnp.where(lane == 2, rank1, jnp.where(lane == 3, rank2, 0))))
    ri_ref[...] = ri.T[:8]
    rw_ref[...] = jnp.where(lane == 0, w1, jnp.where(lane == 1, w2, 0.0))


def _out_proj(rows, n_blocks, halves, w_out, h, mod, g2, wr, br):
    t, d = h.shape
    bm = rows.bm
    t_out = n_blocks * bm
    half = w_out.shape[0] // 2
    w = w_out.astype(BF16)
    wr_hi = wr.astype(BF16)
    wr_lo = (wr - wr_hi.astype(F32)).astype(BF16)
    const = lambda i: (0, 0)
    row = lambda i: (i, 0)
    lat_blocks = rows.lat_blocks
    a_specs, a_args = [], []
    for lat, ctx, col in halves:
        if ctx is None:
            a_specs.append(pl.BlockSpec((bm, half), lambda i, col=col: (i, col)))
            a_args.append(lat)
        else:
            a_specs.append(pl.BlockSpec((bm, half), lambda i, col=col: (jnp.minimum(i, lat_blocks - 1), col)))
            a_specs.append(pl.BlockSpec((bm, half), lambda i, col=col: (jnp.maximum(i - lat_blocks, 0), col)))
            a_args += [lat, ctx]
    return pl.pallas_call(
        functools.partial(_out_kernel, split=tuple(ctx is not None for _, ctx, _ in halves), lat_blocks=lat_blocks),
        out_shape=(jax.ShapeDtypeStruct((t_out, d), F32),
                   jax.ShapeDtypeStruct((t_out, d), BF16),
                   jax.ShapeDtypeStruct((8, t_out), jnp.int32),
                   jax.ShapeDtypeStruct((t_out, ROUTER_W), F32),
                   jax.ShapeDtypeStruct((8, ROUTER_W), F32)),
        grid=(n_blocks,),
        in_specs=a_specs + [
                  pl.BlockSpec((half, d), lambda i: (0, 0)),
                  pl.BlockSpec((half, d), lambda i: (1, 0)),
                  pl.BlockSpec((bm, d), row),
                  pl.BlockSpec((None, N_MOD, d), lambda i: (rows.mod_index(i), 0, 0)),
                  pl.BlockSpec((1, d), const),
                  pl.BlockSpec((d, 2 * ROUTER_W), const),
                  pl.BlockSpec((d, ROUTER_W), const),
                  pl.BlockSpec((1, ROUTER_W), const)],
        out_specs=(pl.BlockSpec((bm, d), row),
                   pl.BlockSpec((bm, d), row),
                   pl.BlockSpec((8, bm), lambda i: (0, i)),
                   pl.BlockSpec((bm, ROUTER_W), row),
                   pl.BlockSpec((8, ROUTER_W), const)),
        scratch_shapes=[pltpu.VMEM((8, ROUTER_W), F32)],
        compiler_params=_cparams(1),
        name="out_proj",
    )(*a_args, w, w, h, mod, g2.reshape(1, d), jnp.concatenate([wr_hi, wr_lo], axis=1), wr_hi, br)


def _in_c_kernel(*refs, q_scale, pending):
    n_src = 5 if pending else 1
    m_ref, g_ref, win_ref, cos_ref, sin_ref = refs[n_src:n_src + 5]
    q_ref, k_ref, v_ref = refs[n_src + 5:n_src + 8]
    if pending:
        h = _moe_residual(*refs[:5])
        refs[n_src + 8][...] = h
    else:
        h = refs[0][...]
    m = m_ref[...]
    z = _norm_mod(h, g_ref[...], m[0:1], m[1:2])
    p = _dot(z.astype(BF16), win_ref[...])
    cos = cos_ref[...]
    sin = sin_ref[...]
    lane = lax.broadcasted_iota(jnp.int32, (1, LANES), 1)
    first = (lane % 32) < 16

    def rope(x):
        partner = jnp.where(first, pltpu.roll(x, LANES - 16, 1), pltpu.roll(x, 16, 1))
        return x * cos + partner * sin

    for tile in range(C_Q_W // LANES):
        cols = slice(tile * LANES, (tile + 1) * LANES)
        q_ref[:, cols] = (rope(p[:, cols]) * q_scale).astype(BF16)
    low_half = lane < C_HEAD_DIM
    for tile in range(C_KV_W // LANES):
        kk = rope(p[:, C_Q_W + tile * LANES:C_Q_W + (tile + 1) * LANES])
        vv = p[:, C_Q_W + C_KV_W + tile * LANES:C_Q_W + C_KV_W + (tile + 1) * LANES]
        ones = jnp.where(lane == C_HEAD_DIM, 1.0, 0.0)
        for half, (kh, vh) in enumerate(((kk, vv), (pltpu.roll(kk, C_HEAD_DIM, 1), pltpu.roll(vv, C_HEAD_DIM, 1)))):
            c0 = (2 * tile + half) * C_KV_PAD
            k_ref[:, c0:c0 + C_KV_PAD] = jnp.where(low_half, kh, 0.0).astype(BF16)
            v_ref[:, c0:c0 + C_KV_PAD] = jnp.where(low_half, vh, ones).astype(BF16)


def _in_c(rows, src, mod, g1, w_in, cos2, sin2):
    s_specs, s_args, h_shape, h_spec = _stream_specs(rows, src)
    t, d = s_args[0].shape
    bm = rows.bm
    n_in = w_in.shape[1]
    const = lambda i: (0, 0)
    row = lambda i: (i, 0)
    return pl.pallas_call(
        functools.partial(_in_c_kernel, q_scale=float(C_HEAD_DIM ** -0.5 * LOG2E), pending=isinstance(src, tuple)),
        out_shape=[jax.ShapeDtypeStruct((t, C_Q_W), BF16),
                   jax.ShapeDtypeStruct((t, C_KV_HEADS * C_KV_PAD), BF16),
                   jax.ShapeDtypeStruct((t, C_KV_HEADS * C_KV_PAD), BF16)] + h_shape,
        grid=(rows.all_blocks,),
        in_specs=s_specs + [
                  pl.BlockSpec((None, N_MOD, d), lambda i: (rows.mod_index(i), 0, 0)),
                  pl.BlockSpec((1, d), const),
                  pl.BlockSpec((d, n_in), const),
                  pl.BlockSpec((bm, LANES), lambda i: (rows.pos_index(i), 0)),
                  pl.BlockSpec((bm, LANES), lambda i: (rows.pos_index(i), 0))],
        out_specs=[pl.BlockSpec((bm, C_Q_W), row),
                   pl.BlockSpec((bm, C_KV_HEADS * C_KV_PAD), row),
                   pl.BlockSpec((bm, C_KV_HEADS * C_KV_PAD), row)] + h_spec,
        compiler_params=_cparams(1),
        name="in_proj_c",
    )(*s_args, mod, g1.reshape(1, d), w_in.astype(BF16), cos2, sin2)


def _sink_attend(q_ref, sink_ref, keys, vals, masks, o_ref, bq):
    def scores(kv):
        q8 = jnp.concatenate([q_ref[:, (kv * C_GROUP + g) * C_HEAD_DIM:(kv * C_GROUP + g + 1) * C_HEAD_DIM]
                              for g in range(C_GROUP)], axis=0)
        return [_dot_t(q8, k[:, kv * C_KV_PAD:kv * C_KV_PAD + C_HEAD_DIM]) for k in keys]

    s_next = scores(0)
    for kv in range(C_KV_HEADS):
        s = s_next
        if kv + 1 < C_KV_HEADS:
            s_next = scores(kv + 1)
        e, mxs = [], []
        for g in range(C_GROUP):
            sg = [si[g * bq:(g + 1) * bq] for si in s]
            sg = [si if msk is None else jnp.where(msk, si, NEG_INF) for si, msk in zip(sg, masks)]
            sink = sink_ref[kv * C_GROUP + g] * LOG2E
            tiles = [si[:, j * LANES:(j + 1) * LANES] for si in sg for j in range(si.shape[1] // LANES)]
            mx = jnp.maximum(sink, functools.reduce(jnp.maximum, tiles).max(axis=-1, keepdims=True))
            e.append([jnp.exp2(si - mx).astype(BF16) for si in sg])
            mxs.append((sink, mx))
        oe = None
        for piece, v in enumerate(vals):
            part = _dot(jnp.concatenate([eg[piece] for eg in e], axis=0), v[:, kv * C_KV_PAD:(kv + 1) * C_KV_PAD])
            oe = part if oe is None else oe + part
        for g, (sink, mx) in enumerate(mxs):
            og = oe[g * bq:(g + 1) * bq]
            den = og[:, C_HEAD_DIM:C_HEAD_DIM + 1] + jnp.exp2(sink - mx)
            c0 = (kv * C_GROUP + g) * C_HEAD_DIM
            o_ref[:, c0:c0 + C_HEAD_DIM] = (og[:, :C_HEAD_DIM] / den).astype(BF16)


def _attn_c_lat_kernel(sink_ref, q_ref, kp_ref, kc_ref, kn_ref, kx_ref, vp_ref, vc_ref, vn_ref, vx_ref, o_ref,
                       *, n_blk):
    n = pl.program_id(1)
    bq = C_WINDOW
    band = 3 * bq
    kband = jnp.concatenate([kp_ref[...], kc_ref[...], kn_ref[...]], axis=0)
    vband = jnp.concatenate([vp_ref[...], vc_ref[...], vn_ref[...]], axis=0)
    qi = lax.broadcasted_iota(jnp.int32, (bq, band), 0)
    kj = lax.broadcasted_iota(jnp.int32, (bq, band), 1)
    rel = qi - (kj - bq)
    ok = (jnp.abs(rel) <= C_WINDOW) & ((kj >= bq) | (n > 0)) & ((kj < 2 * bq) | (n < n_blk - 1))
    _sink_attend(q_ref, sink_ref, [kband, kx_ref[...]], [vband, vx_ref[...]], [ok, None], o_ref, bq)


def _attn_c_ctx_kernel(sink_ref, q_ref, kx_ref, vx_ref, o_ref):
    _sink_attend(q_ref, sink_ref, [kx_ref[...]], [vx_ref[...]], [None], o_ref, q_ref.shape[0])


def _attn_c(rows, q, k, v, sink, ctx_out):
    t = q.shape[0]
    bq = C_WINDOW
    n_blk = rows.n_lat // bq
    n_ctx = rows.n_ctx
    ctx_blk = rows.t_lat // n_ctx
    kvw = C_KV_HEADS * C_KV_PAD
    del t
    smem = pl.BlockSpec(memory_space=pltpu.SMEM)
    prev = lambda b, n: (b * n_blk + jnp.maximum(n - 1, 0), 0)
    cur = lambda b, n: (b * n_blk + n, 0)
    nxt = lambda b, n: (b * n_blk + jnp.minimum(n + 1, n_blk - 1), 0)
    cx = lambda b, n: (ctx_blk + b, 0)
    kv_specs = [pl.BlockSpec((bq, kvw), prev), pl.BlockSpec((bq, kvw), cur),
                pl.BlockSpec((bq, kvw), nxt), pl.BlockSpec((n_ctx, kvw), cx)]
    lat = pl.pallas_call(
        functools.partial(_attn_c_lat_kernel, n_blk=n_blk),
        out_shape=jax.ShapeDtypeStruct((rows.t_lat, C_Q_W), BF16),
        grid=(rows.n_b, n_blk),
        in_specs=[smem, pl.BlockSpec((bq, C_Q_W), cur)] + kv_specs + kv_specs,
        out_specs=pl.BlockSpec((bq, C_Q_W), cur),
        compiler_params=_cparams(2),
        name="attn_c_lat",
    )(sink, q, k, k, k, k, v, v, v, v)
    if not ctx_out:
        return lat, None
    cxb = lambda b: (ctx_blk + b, 0)
    ctx = pl.pallas_call(
        _attn_c_ctx_kernel,
        out_shape=jax.ShapeDtypeStruct((rows.n_b * n_ctx, C_Q_W), BF16),
        grid=(rows.n_b,),
        in_specs=[smem, pl.BlockSpec((n_ctx, C_Q_W), cxb), pl.BlockSpec((n_ctx, kvw), cxb),
                  pl.BlockSpec((n_ctx, kvw), cxb)],
        out_specs=pl.BlockSpec((n_ctx, C_Q_W), lambda b: (b, 0)),
        compiler_params=_cparams(1),
        name="attn_c_ctx",
    )(sink, q, k, v)
    return lat, ctx


def _moe_kernel(be_ref, ne_ref, nu_ref, x_ref, wg_hbm, wu_hbm, wd_hbm, y_ref,
                wg_f, wu_f, wd_f, wgu_s, wd_s, sem, *, layer):
    i = pl.program_id(0)
    used = i < nu_ref[0]
    expert = be_ref[i]
    fresh = (i == 0) | (expert != be_ref[jnp.maximum(i - 1, 0)])

    def weight_copies(e):
        return (pltpu.make_async_copy(wg_hbm.at[layer, e], wg_f, sem.at[0]),
                pltpu.make_async_copy(wu_hbm.at[layer, e], wu_f, sem.at[1]),
                pltpu.make_async_copy(wd_hbm.at[layer, e], wd_f, sem.at[2]))

    @pl.when(used & (i == 0))
    def _():
        for c in weight_copies(expert):
            c.start()

    @pl.when(used & fresh)
    def _():
        for c in weight_copies(expert):
            c.wait()
        wgu_s[:, :D_EXPERT] = wg_f[...].astype(BF16)
        wgu_s[:, D_EXPERT:] = wu_f[...].astype(BF16)
        wd_s[...] = wd_f[...].astype(BF16)
        nxt = ne_ref[i]

        @pl.when(nxt >= 0)
        def _():
            for c in weight_copies(nxt):
                c.start()

    @pl.when(used)
    def _():
        gu = _dot(x_ref[...], wgu_s[...])
        g = gu[:, :D_EXPERT]
        a = (g * jax.nn.sigmoid(g)) * gu[:, D_EXPERT:]
        y_ref[...] = _dot(a.astype(BF16), wd_s[...]).astype(BF16)

    @pl.when(jnp.logical_not(used))
    def _():
        y_ref[...] = jnp.zeros_like(y_ref)


def _moe_blocks(layer, xs, block_e, next_e, n_used, w_gate, w_up, w_down):
    r, d = xs.shape
    bm = MOE_ROWS
    nb = r // bm
    hbm = pl.BlockSpec(memory_space=pl.ANY)
    return pl.pallas_call(
        functools.partial(_moe_kernel, layer=layer),
        out_shape=jax.ShapeDtypeStruct((r, d), BF16),
        grid_spec=pltpu.PrefetchScalarGridSpec(
            num_scalar_prefetch=3,
            grid=(nb,),
            in_specs=[pl.BlockSpec((bm, d), lambda i, be, ne, nu: (jnp.minimum(i, nu[0] - 1), 0)), hbm, hbm, hbm],
            out_specs=pl.BlockSpec((bm, d), lambda i, be, ne, nu: (i, 0)),
            scratch_shapes=[pltpu.VMEM((d, D_EXPERT), F32), pltpu.VMEM((d, D_EXPERT), F32),
                            pltpu.VMEM((D_EXPERT, d), F32),
                            pltpu.VMEM((d, 2 * D_EXPERT), BF16), pltpu.VMEM((D_EXPERT, d), BF16),
                            pltpu.SemaphoreType.DMA((3,))]),
        compiler_params=_cparams(1),
        name="moe_experts",
    )(block_e, next_e, n_used, xs, w_gate, w_up, w_down)


def _plan(ri, cnt, bm):
    n_t = ri.shape[1]
    n_tk = n_t * TOP_K
    nb = -(-(n_tk + N_EXPERTS * (bm - 1)) // bm)
    counts = cnt[0, N_GROUPS:N_GROUPS + N_EXPERTS].astype(jnp.int32)
    padded = (counts + bm - 1) // bm * bm
    pends = jnp.cumsum(padded)
    is_e = ri[None, :TOP_K] == jnp.arange(N_EXPERTS, dtype=jnp.int32)[:, None, None]
    dest = jnp.sum(jnp.where(is_e, (pends - padded)[:, None, None], 0), axis=0) + ri[TOP_K:2 * TOP_K]
    row_tok = (jnp.arange(nb * bm, dtype=jnp.int32) % n_t).at[dest.reshape(-1)].set(
        jnp.arange(n_tk, dtype=jnp.int32) % n_t, unique_indices=True, mode='promise_in_bounds')
    n_used = (pends[-1] // bm).astype(jnp.int32)
    blk = jnp.arange(nb, dtype=jnp.int32)
    block_e = jnp.sum((blk[:, None] * bm >= pends[None, :]).astype(jnp.int32), axis=1)
    block_e = jnp.minimum(block_e, N_EXPERTS - 1)
    block_e = jnp.where(blk < n_used, block_e, block_e[n_used - 1])
    later = (block_e[None, :] > block_e[:, None]) & (blk[None, :] < n_used)
    next_e = jnp.min(jnp.where(later, block_e[None, :], N_EXPERTS), axis=1)
    next_e = jnp.where(next_e < N_EXPERTS, next_e, -1).astype(jnp.int32)
    return row_tok, block_e, next_e, n_used.reshape(1), dest


def _final_kernel(h_ref, y0_ref, y1_ref, rw_ref, m_ref, g_ref, o_ref):
    o_ref[...] = _rms(_moe_residual(h_ref, y0_ref, y1_ref, rw_ref, m_ref)) * g_ref[...]


def _final(rows, n_blocks, h, y0, y1, rw, mod, final_g):
    d = h.shape[1]
    bm = rows.bm
    row = lambda i: (i, 0)
    return pl.pallas_call(
        _final_kernel,
        out_shape=jax.ShapeDtypeStruct((n_blocks * bm, d), F32),
        grid=(n_blocks,),
        in_specs=[pl.BlockSpec((bm, d), row), pl.BlockSpec((bm, d), row), pl.BlockSpec((bm, d), row),
                  pl.BlockSpec((bm, ROUTER_W), row),
                  pl.BlockSpec((None, N_MOD, d), lambda i: (rows.mod_index(i), 0, 0)),
                  pl.BlockSpec((1, d), lambda i: (0, 0))],
        out_specs=pl.BlockSpec((bm, d), row),
        compiler_params=_cparams(1),
        name="final",
    )(h, y0, y1, rw, mod, final_g.reshape(1, d))


def kernel(x, c, ctx, c_ctx, mod_w, mod_b, norm1_g, norm2_g, final_g, a_w_in, a_q_norm_g, a_kv_norm_g, a_w_uq,
           a_w_ukv, a_w_pool, a_pool_scale, a_w_out, c_w_in, c_sink, c_w_out, r_w_group, r_b_group, r_w_expert,
           r_b_expert, e_w_gate, e_w_up, e_w_down):
    n_b, n_lat, d = x.shape
    n_ctx = ctx.shape[1]
    rows = _Rows(n_b, n_lat, n_ctx, ROW_BLOCK)
    in_rows = _Rows(n_b, n_lat, n_ctx, IN_ROWS)
    pool_rows = _Rows(n_b, n_lat, n_ctx, POOL_ROWS)
    depth = mod_w.shape[0]

    cvec = jnp.zeros((8, d), F32).at[:n_b].set(c).at[n_b].set(c_ctx)
    mods = _modulation(cvec, mod_w, mod_b).reshape(depth, 8, N_MOD, d)

    cos, sin = _rope_tables(n_lat, ROW_BLOCK)
    zeros = jnp.zeros_like(cos)
    cos_a, sin_a = jnp.concatenate([cos, zeros], axis=1), jnp.concatenate([sin, zeros], axis=1)
    cos_c, sin_c = jnp.concatenate([cos, cos], axis=1), jnp.concatenate([sin, sin], axis=1)

    src = jnp.concatenate([x.reshape(-1, d), ctx.reshape(-1, d)], axis=0)
    for i in range(depth):
        ctx_out = i < depth - 1
        j = i // 2
        mod = mods[i]
        if i % 2 == 0:
            q, k, v, pp, *h = _in_a(in_rows, src, mod, norm1_g[i], a_w_in[j], a_q_norm_g[j], a_w_uq[j],
                                    a_kv_norm_g[j], a_w_ukv[j], cos_a, sin_a)
            a_lat, a_ctx = _attn_a(rows, q, k, v, ctx_out)
            halves = [(a_lat, a_ctx, 0), (_pool(pool_rows, pp, a_w_pool[j], a_pool_scale[j]), None, 0)]
            w_out = a_w_out[j]
        else:
            q, k, v, *h = _in_c(in_rows, src, mod, norm1_g[i], c_w_in[j], cos_c, sin_c)
            a_lat, a_ctx = _attn_c(rows, q, k, v, c_sink[j], ctx_out)
            halves = [(a_lat, a_ctx, 0), (a_lat, a_ctx, 1)]
            w_out = c_w_out[j]
        h = h[0] if h else src
        n_blocks = rows.all_blocks if ctx_out else rows.lat_blocks
        wr = jnp.zeros((d, ROUTER_W), F32).at[:, :N_GROUPS].set(r_w_group[i])
        wr = wr.at[:, N_GROUPS:N_GROUPS + N_EXPERTS].set(r_w_expert[i])
        br = jnp.zeros((1, ROUTER_W), F32).at[0, :N_GROUPS].set(r_b_group[i])
        br = br.at[0, N_GROUPS:N_GROUPS + N_EXPERTS].set(r_b_expert[i])
        hn, z, ri, rw, cnt = _out_proj(rows, n_blocks, halves, w_out, h, mod, norm2_g[i], wr, br)

        row_tok, block_e, next_e, n_used, dest = _plan(ri, cnt, MOE_ROWS)
        xs = z.at[row_tok].get(mode='promise_in_bounds')
        y = _moe_blocks(i, xs, block_e, next_e, n_used, e_w_gate, e_w_up, e_w_down)
        y0 = y.at[dest[0]].get(mode='promise_in_bounds')
        y1 = y.at[dest[1]].get(mode='promise_in_bounds')
        src = (hn, y0, y1, rw, mod)
    return _final(rows, rows.lat_blocks, *src, final_g).reshape(n_b, n_lat, d)
```

```python
import functools

import numpy as np
import jax
import jax.numpy as jnp
from jax import lax
from jax.experimental import pallas as pl
from jax.experimental.pallas import tpu as pltpu

F32 = jnp.float32
BF16 = jnp.bfloat16

D_MODEL = 2048
DEPTH = 4
GRID_W = 64
EPS = 1e-6
ROPE_BASE = 10000.0
NEG_INF = -1e30
N_MOD = 6

A_NOPE = 128
A_ROPE = 64
A_V = 128
A_HEADS = 8
A_Q_RANK = 512
A_KV_RANK = 256
A_QK_PAD = 256
B_WINDOWS = (2, 4, 8, 16)
B_GROUP_W = 256
B_WIDTH = 1024
POOL_HALO = 8

C_HEAD_DIM = 64
C_HEADS = 32
C_KV_HEADS = 4
C_GROUP = 8
C_WINDOW = 128
C_Q_W = C_HEADS * C_HEAD_DIM
C_KV_W = C_KV_HEADS * C_HEAD_DIM

N_GROUPS = 4
EXPERTS_PER_GROUP = 8
N_EXPERTS = 32
TOP_K = 2
D_EXPERT = 512
ROUTER_W = 128

ROW_BLOCK = 512
IN_ROWS = 256
POOL_ROWS = 256
MOE_ROWS = 256
LANES = 128
VMEM_LIMIT = 56 * 1024 * 1024
LOG2E = 1.4426950408889634
ATTN_A_ROWS = 256
ATTN_A_GROUP = 8
ATTN_A_KEYS = 512
C_KV_PAD = 128


def _cparams(n_axes):
    return pltpu.CompilerParams(dimension_semantics=("arbitrary",) * n_axes,
                                vmem_limit_bytes=VMEM_LIMIT)


def _dot(a, b):
    return jnp.dot(a, b, preferred_element_type=F32)


def _dot_t(a, b):
    return lax.dot_general(a, b, (((1,), (1,)), ((), ())), preferred_element_type=F32)


def _rms(x):
    return x * lax.rsqrt(jnp.mean(x * x, axis=-1, keepdims=True) + EPS)


def _norm_mod(h, g, shift, scale):
    return (_rms(h) * g) * (1 + scale) + shift


def _mod_kernel(s_ref, w_ref, b_ref, o_ref):
    s = s_ref[...]
    s = s * jax.nn.sigmoid(s)
    o_ref[...] = _dot(s.astype(BF16), w_ref[...].astype(BF16)) + b_ref[...]


def _modulation(cvec, mod_w, mod_b):
    depth, d, n = mod_w.shape
    tn = 1024
    return pl.pallas_call(
        _mod_kernel,
        out_shape=jax.ShapeDtypeStruct((depth, 8, n), F32),
        grid=(depth, n // tn),
        in_specs=[pl.BlockSpec((8, d), lambda l, j: (0, 0)),
                  pl.BlockSpec((None, d, tn), lambda l, j: (l, 0, j)),
                  pl.BlockSpec((None, 1, tn), lambda l, j: (l, 0, j))],
        out_specs=pl.BlockSpec((None, 8, tn), lambda l, j: (l, 0, j)),
        compiler_params=_cparams(2),
        name="modulation",
    )(cvec, mod_w, mod_b.reshape(depth, 1, n))


class _Rows:
    def __init__(self, n_b, n_lat, n_ctx, bm):
        self.n_b, self.n_lat, self.n_ctx, self.bm = n_b, n_lat, n_ctx, bm
        self.t_lat = n_b * n_lat
        self.t_all = self.t_lat + n_b * n_ctx
        assert n_lat % bm == 0 and (n_b * n_ctx) % bm == 0
        self.lat_blocks = self.t_lat // bm
        self.all_blocks = self.t_all // bm
        self.blocks_per_seq = n_lat // bm

    def mod_index(self, i):
        return jnp.minimum(i // self.blocks_per_seq, self.n_b)

    def pos_index(self, i):
        return jnp.where(i < self.lat_blocks, i % self.blocks_per_seq, self.blocks_per_seq)


def _rope_tables(n_lat, n_ctx):
    axis_dim = A_ROPE // 2
    inv_freq = ROPE_BASE ** (-jnp.arange(axis_dim // 2, dtype=F32) * 2.0 / axis_dim)
    rows = n_lat // GRID_W
    row = jnp.repeat(jnp.arange(rows, dtype=F32), GRID_W)
    col = jnp.tile(jnp.arange(GRID_W, dtype=F32), rows)
    ang_r = row[:, None] * inv_freq
    ang_c = col[:, None] * inv_freq
    cr, sr, cc, sc = jnp.cos(ang_r), jnp.sin(ang_r), jnp.cos(ang_c), jnp.sin(ang_c)
    cos = jnp.concatenate([cr, cr, cc, cc], axis=-1)
    sin = jnp.concatenate([-sr, sr, -sc, sc], axis=-1)
    cos = jnp.concatenate([cos, jnp.ones((n_ctx, 64), F32)], axis=0)
    sin = jnp.concatenate([sin, jnp.zeros((n_ctx, 64), F32)], axis=0)
    return cos, sin


_ROPE_SWAP = np.concatenate([np.arange(16, 32), np.arange(0, 16), np.arange(48, 64), np.arange(32, 48)])


def _moe_residual(hn_ref, y0_ref, y1_ref, rw_ref, m_ref):
    rw = rw_ref[...]
    y = rw[:, 0:1] * y0_ref[...].astype(F32) + rw[:, 1:2] * y1_ref[...].astype(F32)
    return hn_ref[...] + m_ref[...][5:6] * y


def _in_a_kernel(*refs, q_scale, pending):
    n_src = 5 if pending else 1
    (m_ref, g_ref, win_ref, gq_ref, wuq_ref, gkv_ref, wk_ref, wv_ref, cos_ref, sin_ref) = refs[n_src:n_src + 10]
    q_ref, k_ref, v_ref, pp_ref = refs[n_src + 10:n_src + 14]
    if pending:
        h = _moe_residual(*refs[:5])
        refs[n_src + 14][...] = h
    else:
        h = refs[0][...]
    m = m_ref[...]
    z = _norm_mod(h, g_ref[...], m[0:1], m[1:2])
    p = _dot(z.astype(BF16), win_ref[...])
    cos = cos_ref[...]
    sin = sin_ref[...]

    cqn = _rms(p[:, :A_Q_RANK]) * gq_ref[...]
    qraw = _dot(cqn.astype(BF16), wuq_ref[...])
    for hd in range(A_HEADS):
        c0 = hd * A_QK_PAD
        t = qraw[:, c0 + A_NOPE:c0 + A_QK_PAD]
        rot = t * cos + pltpu.roll(t, 64, 1) * sin
        q_ref[:, c0:c0 + A_NOPE] = (qraw[:, c0:c0 + A_NOPE] * q_scale).astype(BF16)
        q_ref[:, c0 + A_NOPE:c0 + A_QK_PAD] = (rot * q_scale).astype(BF16)

    ckvn = (_rms(p[:, A_Q_RANK:A_Q_RANK + A_KV_RANK]) * gkv_ref[...]).astype(BF16)
    kn = _dot(ckvn, wk_ref[...])
    kt = p[:, 768:896]
    krot = (kt * cos + pltpu.roll(kt, 64, 1) * sin).astype(BF16)
    for hd in range(A_HEADS):
        c0 = hd * A_QK_PAD
        k_ref[:, c0:c0 + A_NOPE] = kn[:, hd * A_NOPE:(hd + 1) * A_NOPE].astype(BF16)
        k_ref[:, c0 + A_NOPE:c0 + A_QK_PAD] = krot
    v_ref[...] = _dot(ckvn, wv_ref[...]).astype(BF16)
    pp_ref[...] = p[:, 896:]


def _stream_specs(rows, src):
    bm = rows.bm
    row = lambda i: (i, 0)
    if not isinstance(src, tuple):
        return [pl.BlockSpec((bm, src.shape[1]), row)], [src], [], []
    hn, y0, y1, rw, mod_prev = src
    d = hn.shape[1]
    specs = [pl.BlockSpec((bm, d), row), pl.BlockSpec((bm, d), row), pl.BlockSpec((bm, d), row),
             pl.BlockSpec((bm, ROUTER_W), row),
             pl.BlockSpec((None, N_MOD, d), lambda i: (rows.mod_index(i), 0, 0))]
    return specs, list(src), [jax.ShapeDtypeStruct(hn.shape, F32)], [pl.BlockSpec((bm, d), row)]


def _in_a(rows, src, mod, g1, w_in, gq, w_uq, gkv, w_ukv, cos2, sin2):
    s_specs, s_args, h_shape, h_spec = _stream_specs(rows, src)
    t, d = s_args[0].shape
    off_rope = A_Q_RANK + A_KV_RANK
    win = jnp.concatenate([w_in[:, :off_rope + A_ROPE], w_in[:, off_rope + _ROPE_SWAP],
                           w_in[:, off_rope + A_ROPE:]], axis=1).astype(BF16)
    wq = w_uq.reshape(A_Q_RANK, A_HEADS, A_NOPE + A_ROPE)
    wuq = jnp.concatenate([wq, wq[:, :, A_NOPE + _ROPE_SWAP]], axis=-1).reshape(A_Q_RANK, A_HEADS * A_QK_PAD)
    wkv = w_ukv.reshape(A_KV_RANK, A_HEADS, A_NOPE + A_V)
    wk = wkv[:, :, :A_NOPE].reshape(A_KV_RANK, A_HEADS * A_NOPE).astype(BF16)
    wv = wkv[:, :, A_NOPE:].reshape(A_KV_RANK, A_HEADS * A_V).astype(BF16)
    n_in = win.shape[1]
    bm = rows.bm
    const = lambda i: (0, 0)
    row = lambda i: (i, 0)
    return pl.pallas_call(
        functools.partial(_in_a_kernel, q_scale=float((A_NOPE + A_ROPE) ** -0.5 * LOG2E),
                          pending=isinstance(src, tuple)),
        out_shape=[jax.ShapeDtypeStruct((t, A_HEADS * A_QK_PAD), BF16),
                   jax.ShapeDtypeStruct((t, A_HEADS * A_QK_PAD), BF16),
                   jax.ShapeDtypeStruct((t, A_HEADS * A_V), BF16),
                   jax.ShapeDtypeStruct((t, B_WIDTH), F32)] + h_shape,
        grid=(rows.all_blocks,),
        in_specs=s_specs + [
                  pl.BlockSpec((None, N_MOD, d), lambda i: (rows.mod_index(i), 0, 0)),
                  pl.BlockSpec((1, d), const),
                  pl.BlockSpec((d, n_in), const),
                  pl.BlockSpec((1, A_Q_RANK), const),
                  pl.BlockSpec((A_Q_RANK, A_HEADS * A_QK_PAD), const),
                  pl.BlockSpec((1, A_KV_RANK), const),
                  pl.BlockSpec((A_KV_RANK, A_HEADS * A_NOPE), const),
                  pl.BlockSpec((A_KV_RANK, A_HEADS * A_V), const),
                  pl.BlockSpec((bm, LANES), lambda i: (rows.pos_index(i), 0)),
                  pl.BlockSpec((bm, LANES), lambda i: (rows.pos_index(i), 0))],
        out_specs=[pl.BlockSpec((bm, A_HEADS * A_QK_PAD), row),
                   pl.BlockSpec((bm, A_HEADS * A_QK_PAD), row),
                   pl.BlockSpec((bm, A_HEADS * A_V), row),
                   pl.BlockSpec((bm, B_WIDTH), row)] + h_spec,
        compiler_params=_cparams(1),
        name="in_proj_a",
    )(*s_args, mod, g1.reshape(1, d), win, gq.reshape(1, -1), wuq.astype(BF16), gkv.reshape(1, -1), wk, wv, cos2, sin2)


def _attn_a_kernel(*refs, n_kv):
    q_ref = refs[0]
    k_refs = refs[1:1 + n_kv]
    v_refs = refs[1 + n_kv:1 + 2 * n_kv]
    o_ref = refs[1 + 2 * n_kv]
    chunks = []
    for k, v in zip(k_refs, v_refs):
        for r0 in range(0, k.shape[0], ATTN_A_KEYS):
            chunks.append((k, v, r0, min(ATTN_A_KEYS, k.shape[0] - r0)))
    bq = min(ATTN_A_ROWS, q_ref.shape[0])

    def scores(rows):
        q = q_ref[rows, :]
        s = []
        top = None
        for k, _, r0, n in chunks:
            si = _dot_t(q, k[r0:r0 + n, :])
            s.append(si)
            for j in range(n // LANES):
                tile = si[:, j * LANES:(j + 1) * LANES]
                top = tile if top is None else jnp.maximum(top, tile)
        return s, top.max(axis=-1, keepdims=True)

    def attend(rows, s, mx):
        o = None
        den = None
        for si, (_, v, r0, n) in zip(s, chunks):
            e = jnp.exp2(si - mx)
            for j in range(n // LANES):
                tile = e[:, j * LANES:(j + 1) * LANES]
                den = tile if den is None else den + tile
            part = _dot(e.astype(BF16), v[r0:r0 + n, :])
            o = part if o is None else o + part
        o_ref[rows, :] = (o / den.sum(axis=-1, keepdims=True)).astype(BF16)

    pair = ATTN_A_GROUP if q_ref.shape[0] % (ATTN_A_GROUP * bq) == 0 else 1

    def block(i, carry):
        rows = [pl.ds(pl.multiple_of((i * pair + u) * bq, bq), bq) for u in range(pair)]
        staged = [scores(r) for r in rows]
        for r, (s, mx) in zip(rows, staged):
            attend(r, s, mx)
        return carry

    lax.fori_loop(0, q_ref.shape[0] // (pair * bq), block, 0)


def _attn_a(rows, q, k, v, ctx_out):
    n_lat, n_ctx = rows.n_lat, rows.n_ctx
    ctx_blk = rows.t_lat // n_ctx
    lat = pl.pallas_call(
        functools.partial(_attn_a_kernel, n_kv=2),
        out_shape=jax.ShapeDtypeStruct((rows.t_lat, A_HEADS * A_V), BF16),
        grid=(rows.n_b, A_HEADS),
        in_specs=[pl.BlockSpec((n_lat, A_QK_PAD), lambda b, h: (b, h)),
                  pl.BlockSpec((n_lat, A_QK_PAD), lambda b, h: (b, h)),
                  pl.BlockSpec((n_ctx, A_QK_PAD), lambda b, h: (ctx_blk + b, h)),
                  pl.BlockSpec((n_lat, A_V), lambda b, h: (b, h)),
                  pl.BlockSpec((n_ctx, A_V), lambda b, h: (ctx_blk + b, h))],
        out_specs=pl.BlockSpec((n_lat, A_V), lambda b, h: (b, h)),
        compiler_params=_cparams(2),
        name="attn_a_lat",
    )(q, k, k, v, v)
    if not ctx_out:
        return lat, None
    ctx = pl.pallas_call(
        functools.partial(_attn_a_kernel, n_kv=1),
        out_shape=jax.ShapeDtypeStruct((rows.n_b * n_ctx, A_HEADS * A_V), BF16),
        grid=(rows.n_b, A_HEADS),
        in_specs=[pl.BlockSpec((n_ctx, A_QK_PAD), lambda b, h: (ctx_blk + b, h)),
                  pl.BlockSpec((n_ctx, A_QK_PAD), lambda b, h: (ctx_blk + b, h)),
                  pl.BlockSpec((n_ctx, A_V), lambda b, h: (ctx_blk + b, h))],
        out_specs=pl.BlockSpec((n_ctx, A_V), lambda b, h: (b, h)),
        compiler_params=_cparams(2),
        name="attn_a_ctx",
    )(q, k, v)
    return lat, ctx


def _pool_kernel(prev_ref, cur_ref, next_ref, w_ref, s_ref, o_ref, ext_ref, *, rows):
    i = pl.program_id(0)
    bm = rows.bm
    is_ctx = i >= rows.lat_blocks
    n_l = jnp.where(is_ctx, rows.n_ctx, rows.n_lat)
    pos0 = jnp.where(is_ctx, 0, (i % rows.blocks_per_seq) * bm)
    ext_ref[0:POOL_HALO, :] = jnp.where(pos0 > 0, prev_ref[...], 0.0)
    ext_ref[POOL_HALO:POOL_HALO + bm, :] = cur_ref[...]
    ext_ref[POOL_HALO + bm:, :] = jnp.where(pos0 + bm < n_l, next_ref[...], 0.0)
    pos = pos0 + lax.broadcasted_iota(jnp.int32, (bm, 1), 0)
    for gi, w in enumerate(B_WINDOWS):
        cols = slice(gi * B_GROUP_W, (gi + 1) * B_GROUP_W)
        acc = ext_ref[POOL_HALO - w // 2:POOL_HALO - w // 2 + bm, cols]
        for dlt in range(-(w // 2) + 1, w // 2):
            acc = acc + ext_ref[POOL_HALO + dlt:POOL_HALO + dlt + bm, cols]
        cnt = jnp.minimum(pos + (w // 2 - 1), n_l - 1) - jnp.maximum(pos - w // 2, 0) + 1
        dev = acc / cnt.astype(F32) - cur_ref[:, cols]
        y = _dot(dev.astype(BF16), w_ref[gi]) * s_ref[:, cols]
        o_ref[:, cols] = y.astype(BF16)


def _pool(rows, pp, w_pool, pool_scale):
    t = pp.shape[0]
    bm = rows.bm
    assert rows.n_ctx == bm
    per = bm // POOL_HALO
    last = t // POOL_HALO - 1
    return pl.pallas_call(
        functools.partial(_pool_kernel, rows=rows),
        out_shape=jax.ShapeDtypeStruct((t, B_WIDTH), BF16),
        grid=(rows.all_blocks,),
        in_specs=[pl.BlockSpec((POOL_HALO, B_WIDTH), lambda i: (jnp.maximum(i * per - 1, 0), 0)),
                  pl.BlockSpec((bm, B_WIDTH), lambda i: (i, 0)),
                  pl.BlockSpec((POOL_HALO, B_WIDTH), lambda i: (jnp.minimum((i + 1) * per, last), 0)),
                  pl.BlockSpec((len(B_WINDOWS), B_GROUP_W, B_GROUP_W), lambda i: (0, 0, 0)),
                  pl.BlockSpec((1, B_WIDTH), lambda i: (0, 0))],
        out_specs=pl.BlockSpec((bm, B_WIDTH), lambda i: (i, 0)),
        scratch_shapes=[pltpu.VMEM((bm + 2 * POOL_HALO, B_WIDTH), F32)],
        compiler_params=_cparams(1),
        name="pool",
    )(pp, pp, pp, w_pool.astype(BF16), pool_scale.reshape(1, -1))


def _out_kernel(*refs, split, lat_blocks):
    i = pl.program_id(0)
    halves = []
    n = 0
    for is_pair in split:
        if is_pair:
            halves.append(jnp.where(i < lat_blocks, refs[n][...], refs[n + 1][...]))
        else:
            halves.append(refs[n][...])
        n += 2 if is_pair else 1
    (w1_ref, w2_ref, h_ref, m_ref, g_ref, wr1_ref, wr2_ref, br_ref,
     hn_ref, z_ref, ri_ref, rw_ref, cnt_ref, carry_ref) = refs[n:]
    bm = h_ref.shape[0]

    @pl.when(i == 0)
    def _():
        carry_ref[...] = jnp.zeros_like(carry_ref)

    m = m_ref[...]
    o = _dot(halves[0], w1_ref[...]) + _dot(halves[1], w2_ref[...])
    hn = h_ref[...] + m[2:3] * o
    hn_ref[...] = hn
    z = _norm_mod(hn, g_ref[...], m[3:4], m[4:5])
    z_hi = z.astype(BF16)
    z_ref[...] = z_hi
    z_lo = (z - z_hi.astype(F32)).astype(BF16)
    l2 = _dot(z_hi, wr1_ref[...])
    lg = l2[:, :ROUTER_W] + l2[:, ROUTER_W:] + _dot(z_lo, wr2_ref[...]) + br_ref[...]

    lane = lax.broadcasted_iota(jnp.int32, (bm, ROUTER_W), 1)
    low = jnp.float32(-3e38)
    is_g = lane < N_GROUPS
    glog = jnp.where(is_g, lg, low)
    gmax = glog.max(axis=-1, keepdims=True)
    g_idx = jnp.where(glog == gmax, lane, ROUTER_W).min(axis=-1, keepdims=True)
    g_gate = 1.0 / jnp.where(is_g, jnp.exp(lg - gmax), 0.0).sum(axis=-1, keepdims=True)
    lo = N_GROUPS + EXPERTS_PER_GROUP * g_idx
    el = jnp.where((lane >= lo) & (lane < lo + EXPERTS_PER_GROUP), lg, low)
    v1 = el.max(axis=-1, keepdims=True)
    i1 = jnp.where(el == v1, lane, ROUTER_W).min(axis=-1, keepdims=True)
    el2 = jnp.where(lane == i1, low, el)
    v2 = el2.max(axis=-1, keepdims=True)
    i2 = jnp.where(el2 == v2, lane, ROUTER_W).min(axis=-1, keepdims=True)
    e21 = jnp.exp(v2 - v1)
    w1 = g_gate * (1.0 / (1.0 + e21))
    w2 = g_gate * (e21 / (1.0 + e21))

    hit1 = lane == i1
    hit2 = lane == i2
    onehot = jnp.where(hit1 | hit2, 1.0, 0.0)
    r_i = lax.broadcasted_iota(jnp.int32, (bm, bm), 0)
    c_i = lax.broadcasted_iota(jnp.int32, (bm, bm), 1)
    before = _dot(jnp.where(r_i > c_i, 1.0, 0.0).astype(BF16), onehot.astype(BF16)) + carry_ref[0:1, :]
    rank1 = jnp.where(hit1, before, 0.0).sum(axis=-1, keepdims=True).astype(jnp.int32)
    rank2 = jnp.where(hit2, before, 0.0).sum(axis=-1, keepdims=True).astype(jnp.int32)
    total = carry_ref[0:1, :] + onehot.sum(axis=0, keepdims=True)
    carry_ref[...] = jnp.broadcast_to(total, carry_ref.shape)
    cnt_ref[...] = jnp.broadcast_to(total, cnt_ref.shape)

    ri = jnp.where(lane == 0, i1 - N_GROUPS, jnp.where(lane == 1, i2 - N_GROUPS,
                   jnp.where(lane == 2, rank1, jnp.where(lane == 3, rank2, 0))))
    ri_ref[...] = ri.T[:8]
    rw_ref[...] = jnp.where(lane == 0, w1, jnp.where(lane == 1, w2, 0.0))


def _out_proj(rows, n_blocks, halves, w_out, h, mod, g2, wr, br):
    t, d = h.shape
    bm = rows.bm
    t_out = n_blocks * bm
    half = w_out.shape[0] // 2
    w = w_out.astype(BF16)
    wr_hi = wr.astype(BF16)
    wr_lo = (wr - wr_hi.astype(F32)).astype(BF16)
    const = lambda i: (0, 0)
    row = lambda i: (i, 0)
    lat_blocks = rows.lat_blocks
    a_specs, a_args = [], []
    for lat, ctx, col in halves:
        if ctx is None:
            a_specs.append(pl.BlockSpec((bm, half), lambda i, col=col: (i, col)))
            a_args.append(lat)
        else:
            a_specs.append(pl.BlockSpec((bm, half), lambda i, col=col: (jnp.minimum(i, lat_blocks - 1), col)))
            a_specs.append(pl.BlockSpec((bm, half), lambda i, col=col: (jnp.maximum(i - lat_blocks, 0), col)))
            a_args += [lat, ctx]
    return pl.pallas_call(
        functools.partial(_out_kernel, split=tuple(ctx is not None for _, ctx, _ in halves), lat_blocks=lat_blocks),
        out_shape=(jax.ShapeDtypeStruct((t_out, d), F32),
                   jax.ShapeDtypeStruct((t_out, d), BF16),
                   jax.ShapeDtypeStruct((8, t_out), jnp.int32),
                   jax.ShapeDtypeStruct((t_out, ROUTER_W), F32),
                   jax.ShapeDtypeStruct((8, ROUTER_W), F32)),
        grid=(n_blocks,),
        in_specs=a_specs + [
                  pl.BlockSpec((half, d), lambda i: (0, 0)),
                  pl.BlockSpec((half, d), lambda i: (1, 0)),
                  pl.BlockSpec((bm, d), row),
                  pl.BlockSpec((None, N_MOD, d), lambda i: (rows.mod_index(i), 0, 0)),
                  pl.BlockSpec((1, d), const),
                  pl.BlockSpec((d, 2 * ROUTER_W), const),
                  pl.BlockSpec((d, ROUTER_W), const),
                  pl.BlockSpec((1, ROUTER_W), const)],
        out_specs=(pl.BlockSpec((bm, d), row),
                   pl.BlockSpec((bm, d), row),
                   pl.BlockSpec((8, bm), lambda i: (0, i)),
                   pl.BlockSpec((bm, ROUTER_W), row),
                   pl.BlockSpec((8, ROUTER_W), const)),
        scratch_shapes=[pltpu.VMEM((8, ROUTER_W), F32)],
        compiler_params=_cparams(1),
        name="out_proj",
    )(*a_args, w, w, h, mod, g2.reshape(1, d), jnp.concatenate([wr_hi, wr_lo], axis=1), wr_hi, br)


def _in_c_kernel(*refs, q_scale, pending):
    n_src = 5 if pending else 1
    m_ref, g_ref, win_ref, cos_ref, sin_ref = refs[n_src:n_src + 5]
    q_ref, k_ref, v_ref = refs[n_src + 5:n_src + 8]
    if pending:
        h = _moe_residual(*refs[:5])
        refs[n_src + 8][...] = h
    else:
        h = refs[0][...]
    m = m_ref[...]
    z = _norm_mod(h, g_ref[...], m[0:1], m[1:2])
    p = _dot(z.astype(BF16), win_ref[...])
    cos = cos_ref[...]
    sin = sin_ref[...]
    lane = lax.broadcasted_iota(jnp.int32, (1, LANES), 1)
    first = (lane % 32) < 16

    def rope(x):
        partner = jnp.where(first, pltpu.roll(x, LANES - 16, 1), pltpu.roll(x, 16, 1))
        return x * cos + partner * sin

    for tile in range(C_Q_W // LANES):
        cols = slice(tile * LANES, (tile + 1) * LANES)
        q_ref[:, cols] = (rope(p[:, cols]) * q_scale).astype(BF16)
    low_half = lane < C_HEAD_DIM
    for tile in range(C_KV_W // LANES):
        kk = rope(p[:, C_Q_W + tile * LANES:C_Q_W + (tile + 1) * LANES])
        vv = p[:, C_Q_W + C_KV_W + tile * LANES:C_Q_W + C_KV_W + (tile + 1) * LANES]
        ones = jnp.where(lane == C_HEAD_DIM, 1.0, 0.0)
        for half, (kh, vh) in enumerate(((kk, vv), (pltpu.roll(kk, C_HEAD_DIM, 1), pltpu.roll(vv, C_HEAD_DIM, 1)))):
            c0 = (2 * tile + half) * C_KV_PAD
            k_ref[:, c0:c0 + C_KV_PAD] = jnp.where(low_half, kh, 0.0).astype(BF16)
            v_ref[:, c0:c0 + C_KV_PAD] = jnp.where(low_half, vh, ones).astype(BF16)


def _in_c(rows, src, mod, g1, w_in, cos2, sin2):
    s_specs, s_args, h_shape, h_spec = _stream_specs(rows, src)
    t, d = s_args[0].shape
    bm = rows.bm
    n_in = w_in.shape[1]
    const = lambda i: (0, 0)
    row = lambda i: (i, 0)
    return pl.pallas_call(
        functools.partial(_in_c_kernel, q_scale=float(C_HEAD_DIM ** -0.5 * LOG2E), pending=isinstance(src, tuple)),
        out_shape=[jax.ShapeDtypeStruct((t, C_Q_W), BF16),
                   jax.ShapeDtypeStruct((t, C_KV_HEADS * C_KV_PAD), BF16),
                   jax.ShapeDtypeStruct((t, C_KV_HEADS * C_KV_PAD), BF16)] + h_shape,
        grid=(rows.all_blocks,),
        in_specs=s_specs + [
                  pl.BlockSpec((None, N_MOD, d), lambda i: (rows.mod_index(i), 0, 0)),
                  pl.BlockSpec((1, d), const),
                  pl.BlockSpec((d, n_in), const),
                  pl.BlockSpec((bm, LANES), lambda i: (rows.pos_index(i), 0)),
                  pl.BlockSpec((bm, LANES), lambda i: (rows.pos_index(i), 0))],
        out_specs=[pl.BlockSpec((bm, C_Q_W), row),
                   pl.BlockSpec((bm, C_KV_HEADS * C_KV_PAD), row),
                   pl.BlockSpec((bm, C_KV_HEADS * C_KV_PAD), row)] + h_spec,
        compiler_params=_cparams(1),
        name="in_proj_c",
    )(*s_args, mod, g1.reshape(1, d), w_in.astype(BF16), cos2, sin2)


def _sink_attend(q_ref, sink_ref, keys, vals, masks, o_ref, bq):
    def scores(kv):
        q8 = jnp.concatenate([q_ref[:, (kv * C_GROUP + g) * C_HEAD_DIM:(kv * C_GROUP + g + 1) * C_HEAD_DIM]
                              for g in range(C_GROUP)], axis=0)
        return [_dot_t(q8, k[:, kv * C_KV_PAD:kv * C_KV_PAD + C_HEAD_DIM]) for k in keys]

    s_next = scores(0)
    for kv in range(C_KV_HEADS):
        s = s_next
        if kv + 1 < C_KV_HEADS:
            s_next = scores(kv + 1)
        e, mxs = [], []
        for g in range(C_GROUP):
            sg = [si[g * bq:(g + 1) * bq] for si in s]
            sg = [si if msk is None else jnp.where(msk, si, NEG_INF) for si, msk in zip(sg, masks)]
            sink = sink_ref[kv * C_GROUP + g] * LOG2E
            tiles = [si[:, j * LANES:(j + 1) * LANES] for si in sg for j in range(si.shape[1] // LANES)]
            mx = jnp.maximum(sink, functools.reduce(jnp.maximum, tiles).max(axis=-1, keepdims=True))
            e.append([jnp.exp2(si - mx).astype(BF16) for si in sg])
            mxs.append((sink, mx))
        oe = None
        for piece, v in enumerate(vals):
            part = _dot(jnp.concatenate([eg[piece] for eg in e], axis=0), v[:, kv * C_KV_PAD:(kv + 1) * C_KV_PAD])
            oe = part if oe is None else oe + part
        for g, (sink, mx) in enumerate(mxs):
            og = oe[g * bq:(g + 1) * bq]
            den = og[:, C_HEAD_DIM:C_HEAD_DIM + 1] + jnp.exp2(sink - mx)
            c0 = (kv * C_GROUP + g) * C_HEAD_DIM
            o_ref[:, c0:c0 + C_HEAD_DIM] = (og[:, :C_HEAD_DIM] / den).astype(BF16)


def _attn_c_lat_kernel(sink_ref, q_ref, kp_ref, kc_ref, kn_ref, kx_ref, vp_ref, vc_ref, vn_ref, vx_ref, o_ref,
                       *, n_blk):
    n = pl.program_id(1)
    bq = C_WINDOW
    band = 3 * bq
    kband = jnp.concatenate([kp_ref[...], kc_ref[...], kn_ref[...]], axis=0)
    vband = jnp.concatenate([vp_ref[...], vc_ref[...], vn_ref[...]], axis=0)
    qi = lax.broadcasted_iota(jnp.int32, (bq, band), 0)
    kj = lax.broadcasted_iota(jnp.int32, (bq, band), 1)
    rel = qi - (kj - bq)
    ok = (jnp.abs(rel) <= C_WINDOW) & ((kj >= bq) | (n > 0)) & ((kj < 2 * bq) | (n < n_blk - 1))
    _sink_attend(q_ref, sink_ref, [kband, kx_ref[...]], [vband, vx_ref[...]], [ok, None], o_ref, bq)


def _attn_c_ctx_kernel(sink_ref, q_ref, kx_ref, vx_ref, o_ref):
    _sink_attend(q_ref, sink_ref, [kx_ref[...]], [vx_ref[...]], [None], o_ref, q_ref.shape[0])


def _attn_c(rows, q, k, v, sink, ctx_out):
    t = q.shape[0]
    bq = C_WINDOW
    n_blk = rows.n_lat // bq
    n_ctx = rows.n_ctx
    ctx_blk = rows.t_lat // n_ctx
    kvw = C_KV_HEADS * C_KV_PAD
    del t
    smem = pl.BlockSpec(memory_space=pltpu.SMEM)
    prev = lambda b, n: (b * n_blk + jnp.maximum(n - 1, 0), 0)
    cur = lambda b, n: (b * n_blk + n, 0)
    nxt = lambda b, n: (b * n_blk + jnp.minimum(n + 1, n_blk - 1), 0)
    cx = lambda b, n: (ctx_blk + b, 0)
    kv_specs = [pl.BlockSpec((bq, kvw), prev), pl.BlockSpec((bq, kvw), cur),
                pl.BlockSpec((bq, kvw), nxt), pl.BlockSpec((n_ctx, kvw), cx)]
    lat = pl.pallas_call(
        functools.partial(_attn_c_lat_kernel, n_blk=n_blk),
        out_shape=jax.ShapeDtypeStruct((rows.t_lat, C_Q_W), BF16),
        grid=(rows.n_b, n_blk),
        in_specs=[smem, pl.BlockSpec((bq, C_Q_W), cur)] + kv_specs + kv_specs,
        out_specs=pl.BlockSpec((bq, C_Q_W), cur),
        compiler_params=_cparams(2),
        name="attn_c_lat",
    )(sink, q, k, k, k, k, v, v, v, v)
    if not ctx_out:
        return lat, None
    cxb = lambda b: (ctx_blk + b, 0)
    ctx = pl.pallas_call(
        _attn_c_ctx_kernel,
        out_shape=jax.ShapeDtypeStruct((rows.n_b * n_ctx, C_Q_W), BF16),
        grid=(rows.n_b,),
        in_specs=[smem, pl.BlockSpec((n_ctx, C_Q_W), cxb), pl.BlockSpec((n_ctx, kvw), cxb),
                  pl.BlockSpec((n_ctx, kvw), cxb)],
        out_specs=pl.BlockSpec((n_ctx, C_Q_W), lambda b: (b, 0)),
        compiler_params=_cparams(1),
        name="attn_c_ctx",
    )(sink, q, k, v)
    return lat, ctx


def _moe_kernel(be_ref, ne_ref, nu_ref, x_ref, wg_hbm, wu_hbm, wd_hbm, y_ref,
                wg_f, wu_f, wd_f, wgu_s, wd_s, sem, *, layer):
    i = pl.program_id(0)
    used = i < nu_ref[0]
    expert = be_ref[i]
    fresh = (i == 0) | (expert != be_ref[jnp.maximum(i - 1, 0)])

    def weight_copies(e):
        return (pltpu.make_async_copy(wg_hbm.at[layer, e], wg_f, sem.at[0]),
                pltpu.make_async_copy(wu_hbm.at[layer, e], wu_f, sem.at[1]),
                pltpu.make_async_copy(wd_hbm.at[layer, e], wd_f, sem.at[2]))

    @pl.when(used & (i == 0))
    def _():
        for c in weight_copies(expert):
            c.start()

    @pl.when(used & fresh)
    def _():
        for c in weight_copies(expert):
            c.wait()
        wgu_s[:, :D_EXPERT] = wg_f[...].astype(BF16)
        wgu_s[:, D_EXPERT:] = wu_f[...].astype(BF16)
        wd_s[...] = wd_f[...].astype(BF16)
        nxt = ne_ref[i]

        @pl.when(nxt >= 0)
        def _():
            for c in weight_copies(nxt):
                c.start()

    @pl.when(used)
    def _():
        gu = _dot(x_ref[...], wgu_s[...])
        g = gu[:, :D_EXPERT]
        a = (g * jax.nn.sigmoid(g)) * gu[:, D_EXPERT:]
        y_ref[...] = _dot(a.astype(BF16), wd_s[...]).astype(BF16)

    @pl.when(jnp.logical_not(used))
    def _():
        y_ref[...] = jnp.zeros_like(y_ref)


def _moe_blocks(layer, xs, block_e, next_e, n_used, w_gate, w_up, w_down):
    r, d = xs.shape
    bm = MOE_ROWS
    nb = r // bm
    hbm = pl.BlockSpec(memory_space=pl.ANY)
    return pl.pallas_call(
        functools.partial(_moe_kernel, layer=layer),
        out_shape=jax.ShapeDtypeStruct((r, d), BF16),
        grid_spec=pltpu.PrefetchScalarGridSpec(
            num_scalar_prefetch=3,
            grid=(nb,),
            in_specs=[pl.BlockSpec((bm, d), lambda i, be, ne, nu: (jnp.minimum(i, nu[0] - 1), 0)), hbm, hbm, hbm],
            out_specs=pl.BlockSpec((bm, d), lambda i, be, ne, nu: (i, 0)),
            scratch_shapes=[pltpu.VMEM((d, D_EXPERT), F32), pltpu.VMEM((d, D_EXPERT), F32),
                            pltpu.VMEM((D_EXPERT, d), F32),
                            pltpu.VMEM((d, 2 * D_EXPERT), BF16), pltpu.VMEM((D_EXPERT, d), BF16),
                            pltpu.SemaphoreType.DMA((3,))]),
        compiler_params=_cparams(1),
        name="moe_experts",
    )(block_e, next_e, n_used, xs, w_gate, w_up, w_down)


def _plan(ri, cnt, bm):
    n_t = ri.shape[1]
    n_tk = n_t * TOP_K
    nb = -(-(n_tk + N_EXPERTS * (bm - 1)) // bm)
    counts = cnt[0, N_GROUPS:N_GROUPS + N_EXPERTS].astype(jnp.int32)
    padded = (counts + bm - 1) // bm * bm
    pends = jnp.cumsum(padded)
    is_e = ri[None, :TOP_K] == jnp.arange(N_EXPERTS, dtype=jnp.int32)[:, None, None]
    dest = jnp.sum(jnp.where(is_e, (pends - padded)[:, None, None], 0), axis=0) + ri[TOP_K:2 * TOP_K]
    row_tok = (jnp.arange(nb * bm, dtype=jnp.int32) % n_t).at[dest.reshape(-1)].set(
        jnp.arange(n_tk, dtype=jnp.int32) % n_t, unique_indices=True, mode='promise_in_bounds')
    n_used = (pends[-1] // bm).astype(jnp.int32)
    blk = jnp.arange(nb, dtype=jnp.int32)
    block_e = jnp.sum((blk[:, None] * bm >= pends[None, :]).astype(jnp.int32), axis=1)
    block_e = jnp.minimum(block_e, N_EXPERTS - 1)
    block_e = jnp.where(blk < n_used, block_e, block_e[n_used - 1])
    later = (block_e[None, :] > block_e[:, None]) & (blk[None, :] < n_used)
    next_e = jnp.min(jnp.where(later, block_e[None, :], N_EXPERTS), axis=1)
    next_e = jnp.where(next_e < N_EXPERTS, next_e, -1).astype(jnp.int32)
    return row_tok, block_e, next_e, n_used.reshape(1), dest


def _final_kernel(h_ref, y0_ref, y1_ref, rw_ref, m_ref, g_ref, o_ref):
    o_ref[...] = _rms(_moe_residual(h_ref, y0_ref, y1_ref, rw_ref, m_ref)) * g_ref[...]


def _final(rows, n_blocks, h, y0, y1, rw, mod, final_g):
    d = h.shape[1]
    bm = rows.bm
    row = lambda i: (i, 0)
    return pl.pallas_call(
        _final_kernel,
        out_shape=jax.ShapeDtypeStruct((n_blocks * bm, d), F32),
        grid=(n_blocks,),
        in_specs=[pl.BlockSpec((bm, d), row), pl.BlockSpec((bm, d), row), pl.BlockSpec((bm, d), row),
                  pl.BlockSpec((bm, ROUTER_W), row),
                  pl.BlockSpec((None, N_MOD, d), lambda i: (rows.mod_index(i), 0, 0)),
                  pl.BlockSpec((1, d), lambda i: (0, 0))],
        out_specs=pl.BlockSpec((bm, d), row),
        compiler_params=_cparams(1),
        name="final",
    )(h, y0, y1, rw, mod, final_g.reshape(1, d))


def kernel(x, c, ctx, c_ctx, mod_w, mod_b, norm1_g, norm2_g, final_g, a_w_in, a_q_norm_g, a_kv_norm_g, a_w_uq,
           a_w_ukv, a_w_pool, a_pool_scale, a_w_out, c_w_in, c_sink, c_w_out, r_w_group, r_b_group, r_w_expert,
           r_b_expert, e_w_gate, e_w_up, e_w_down):
    n_b, n_lat, d = x.shape
    n_ctx = ctx.shape[1]
    rows = _Rows(n_b, n_lat, n_ctx, ROW_BLOCK)
    in_rows = _Rows(n_b, n_lat, n_ctx, IN_ROWS)
    pool_rows = _Rows(n_b, n_lat, n_ctx, POOL_ROWS)
    depth = mod_w.shape[0]

    cvec = jnp.zeros((8, d), F32).at[:n_b].set(c).at[n_b].set(c_ctx)
    mods = _modulation(cvec, mod_w, mod_b).reshape(depth, 8, N_MOD, d)

    cos, sin = _rope_tables(n_lat, ROW_BLOCK)
    zeros = jnp.zeros_like(cos)
    cos_a, sin_a = jnp.concatenate([cos, zeros], axis=1), jnp.concatenate([sin, zeros], axis=1)
    cos_c, sin_c = jnp.concatenate([cos, cos], axis=1), jnp.concatenate([sin, sin], axis=1)

    src = jnp.concatenate([x.reshape(-1, d), ctx.reshape(-1, d)], axis=0)
    for i in range(depth):
        ctx_out = i < depth - 1
        j = i // 2
        mod = mods[i]
        if i % 2 == 0:
            q, k, v, pp, *h = _in_a(in_rows, src, mod, norm1_g[i], a_w_in[j], a_q_norm_g[j], a_w_uq[j],
                                    a_kv_norm_g[j], a_w_ukv[j], cos_a, sin_a)
            a_lat, a_ctx = _attn_a(rows, q, k, v, ctx_out)
            halves = [(a_lat, a_ctx, 0), (_pool(pool_rows, pp, a_w_pool[j], a_pool_scale[j]), None, 0)]
            w_out = a_w_out[j]
        else:
            q, k, v, *h = _in_c(in_rows, src, mod, norm1_g[i], c_w_in[j], cos_c, sin_c)
            a_lat, a_ctx = _attn_c(rows, q, k, v, c_sink[j], ctx_out)
            halves = [(a_lat, a_ctx, 0), (a_lat, a_ctx, 1)]
            w_out = c_w_out[j]
        h = h[0] if h else src
        n_blocks = rows.all_blocks if ctx_out else rows.lat_blocks
        wr = jnp.zeros((d, ROUTER_W), F32).at[:, :N_GROUPS].set(r_w_group[i])
        wr = wr.at[:, N_GROUPS:N_GROUPS + N_EXPERTS].set(r_w_expert[i])
        br = jnp.zeros((1, ROUTER_W), F32).at[0, :N_GROUPS].set(r_b_group[i])
        br = br.at[0, N_GROUPS:N_GROUPS + N_EXPERTS].set(r_b_expert[i])
        hn, z, ri, rw, cnt = _out_proj(rows, n_blocks, halves, w_out, h, mod, norm2_g[i], wr, br)

        row_tok, block_e, next_e, n_used, dest = _plan(ri, cnt, MOE_ROWS)
        xs = z.at[row_tok].get(mode='promise_in_bounds')
        y = _moe_blocks(i, xs, block_e, next_e, n_used, e_w_gate, e_w_up, e_w_down)
        y0 = y.at[dest[0]].get(mode='promise_in_bounds')
        y1 = y.at[dest[1]].get(mode='promise_in_bounds')
        src = (hn, y0, y1, rw, mod)
    return _final(rows, rows.lat_blocks, *src, final_g).reshape(n_b, n_lat, d)
```
